```python
import math
import jax, jax.numpy as jnp
from jax import lax
import numpy as np

D_MODEL = 1024
BATCH = 8
SEQ = 4096
DEPTH = 4

CHUNK = 64

N_B_LAYERS = DEPTH // 2
N_A_LAYERS = DEPTH - N_B_LAYERS

SSM_EXPAND = 2
SSM_D_INNER = SSM_EXPAND * D_MODEL
SSM_HEAD_DIM = 64
SSM_HEADS = SSM_D_INNER // SSM_HEAD_DIM
SSM_GROUPS = 8
SSM_HEADS_PER_GROUP = SSM_HEADS // SSM_GROUPS
SSM_STATE = 128
SSM_CONV = 4
SSM_CONV_DIM = SSM_D_INNER + 2 * SSM_GROUPS * SSM_STATE
SSM_IN_DIM = 2 * SSM_D_INNER + 2 * SSM_GROUPS * SSM_STATE + SSM_HEADS
SSD_CHUNK = CHUNK
DT_MIN = 0.001
DT_MAX = 0.1

DIFF_HEAD_DIM = 64
DIFF_V_DIM = 2 * DIFF_HEAD_DIM
DIFF_HEADS = D_MODEL // DIFF_V_DIM
DIFF_WIDTH = DIFF_HEADS * DIFF_V_DIM
KV_DIM = DIFF_HEADS * 2 * DIFF_HEAD_DIM + DIFF_WIDTH
Q_BLOCK = 128
ALIBI_MAX_EXP = 8.0

EPS = 1e-5

kernel_name = "yoco_mamba2_diffattn_alibi_trunk"


def rms_norm(x, g):
    xf = x.astype(jnp.float32)
    y = xf * lax.rsqrt(jnp.mean(xf * xf, axis=-1, keepdims=True) + EPS)
    return (y * g.astype(jnp.float32)).astype(x.dtype)


def gated_group_rms_norm(y, z, g):
    u = y.astype(jnp.float32) * jax.nn.silu(z.astype(jnp.float32))
    shp = u.shape
    u = u.reshape(shp[:-1] + (SSM_GROUPS, SSM_D_INNER // SSM_GROUPS))
    u = u * lax.rsqrt(jnp.mean(u * u, axis=-1, keepdims=True) + EPS)
    return u.reshape(shp) * g.astype(jnp.float32)


def causal_depthwise_conv(u, w, bias):
    c = u.shape[-1]
    out = lax.conv_general_dilated(
        u, w[:, None, :].astype(u.dtype), window_strides=(1,),
        padding=[(SSM_CONV - 1, 0)], dimension_numbers=("NWC", "WIO", "NWC"),
        feature_group_count=c)
    return out + bias.astype(u.dtype)


def ssd_chunked_scan(xdt, adt, bm, cm):
    b, s = xdt.shape[:2]
    nc = s // SSD_CHUNK

    def to_chunks(t):
        return jnp.moveaxis(t.reshape((b, nc, SSD_CHUNK) + t.shape[2:]), 1, 0)

    causal = jnp.tril(jnp.ones((SSD_CHUNK, SSD_CHUNK), dtype=bool))[None, :, :, None, None]

    def step(state, inp):
        x_c, a_c, b_c, c_c = inp
        acs = jnp.cumsum(a_c, axis=1)
        seg = acs[:, :, None] - acs[:, None, :]
        decay = jnp.exp(jnp.where(causal, seg, -jnp.inf))
        cb = jnp.einsum("blgn,bsgn->blsg", c_c, b_c)
        y_diag = jnp.einsum("blsgr,bsgrp->blgrp", cb[..., None] * decay, x_c)
        y_off = jnp.einsum("blgn,bgrpn->blgrp", c_c, state) * jnp.exp(acs)[..., None]
        to_end = jnp.exp(acs[:, -1:] - acs)
        new_state = (state * jnp.exp(acs[:, -1])[..., None, None]
                     + jnp.einsum("blgn,blgrp->bgrpn", b_c, x_c * to_end[..., None]))
        return new_state, y_diag + y_off

    state0 = jnp.zeros((b, SSM_GROUPS, SSM_HEADS_PER_GROUP, SSM_HEAD_DIM, SSM_STATE), jnp.float32)
    _, ys = lax.scan(step, state0, (to_chunks(xdt), to_chunks(adt), to_chunks(bm), to_chunks(cm)))
    return jnp.moveaxis(ys, 0, 1).reshape(xdt.shape)


def mamba2_layer(x, norm_g, w_in, conv_w, conv_b, dt_bias, a_log, d_skip, gate_norm_g, w_out):
    b, s, _ = x.shape
    G, R, P, N = SSM_GROUPS, SSM_HEADS_PER_GROUP, SSM_HEAD_DIM, SSM_STATE
    h = rms_norm(x, norm_g)
    zxbcdt = h @ w_in
    z = zxbcdt[..., :SSM_D_INNER]
    xbc = zxbcdt[..., SSM_D_INNER:SSM_D_INNER + SSM_CONV_DIM]
    dt_raw = zxbcdt[..., SSM_D_INNER + SSM_CONV_DIM:]
    xbc = jax.nn.silu(causal_depthwise_conv(xbc, conv_w, conv_b))
    gn = G * N
    xs = xbc[..., :SSM_D_INNER].reshape(b, s, G, R, P).astype(jnp.float32)
    bm = xbc[..., SSM_D_INNER:SSM_D_INNER + gn].reshape(b, s, G, N).astype(jnp.float32)
    cm = xbc[..., SSM_D_INNER + gn:].reshape(b, s, G, N).astype(jnp.float32)
    dt = jax.nn.softplus(dt_raw.astype(jnp.float32) + dt_bias.astype(jnp.float32)).reshape(b, s, G, R)
    a = -jnp.exp(a_log.astype(jnp.float32)).reshape(G, R)
    y = ssd_chunked_scan(xs * dt[..., None], a * dt, bm, cm)
    y = y + d_skip.astype(jnp.float32).reshape(G, R, 1) * xs
    y = gated_group_rms_norm(y.reshape(b, s, SSM_D_INNER), z, gate_norm_g).astype(x.dtype)
    return x + y @ w_out


def shared_kv(x, kv_norm_g, w_kv):
    b, s, _ = x.shape
    kv = rms_norm(x, kv_norm_g) @ w_kv
    k = kv[..., :DIFF_HEADS * 2 * DIFF_HEAD_DIM].reshape(b, s, DIFF_HEADS, 2, DIFF_HEAD_DIM)
    v = kv[..., DIFF_HEADS * 2 * DIFF_HEAD_DIM:].reshape(b, s, DIFF_HEADS, DIFF_V_DIM)
    return k.transpose(0, 2, 3, 1, 4), v.transpose(0, 2, 1, 3)


def diff_attention_layer(x, k, v, norm_g, w_in, lam_qk, sub_g, w_out, lambda_init):
    b, s, _ = x.shape
    H, d = DIFF_HEADS, DIFF_HEAD_DIM
    nblk = s // Q_BLOCK
    qg = rms_norm(x, norm_g) @ w_in
    q = qg[..., :DIFF_WIDTH].reshape(b, nblk, Q_BLOCK, H, 2, d).transpose(1, 0, 3, 4, 2, 5)
    gate = qg[..., DIFF_WIDTH:]
    lf = lam_qk.astype(jnp.float32)
    lam = jnp.exp(jnp.sum(lf[0] * lf[1])) - jnp.exp(jnp.sum(lf[2] * lf[3])) + lambda_init
    slopes = 2.0 ** (-ALIBI_MAX_EXP * jnp.arange(1, H + 1, dtype=jnp.float32) / H)
    key_pos = jnp.arange(s)
    scale = 1.0 / math.sqrt(d)

    def attend_block(args):
        qb, blk = args
        qpos = blk * Q_BLOCK + jnp.arange(Q_BLOCK)
        dist = jnp.abs(qpos[:, None] - key_pos[None, :]).astype(jnp.float32)
        allowed = (key_pos // CHUNK)[None, :] <= (qpos // CHUNK)[:, None]
        scores = jnp.einsum("bhiqd,bhikd->bhiqk", qb, k).astype(jnp.float32) * scale
        scores = scores - slopes[None, :, None, None, None] * dist
        probs = jax.nn.softmax(jnp.where(allowed, scores, -jnp.inf), axis=-1)
        weights = probs[:, :, 0] - lam * probs[:, :, 1]
        return jnp.einsum("bhqk,bhke->bqhe", weights.astype(v.dtype), v)

    out = lax.map(attend_block, (q, jnp.arange(nblk)))
    out = jnp.moveaxis(out, 0, 1).reshape(b, s, H, DIFF_V_DIM)
    out = (rms_norm(out, sub_g) * (1.0 - lambda_init)).reshape(b, s, DIFF_WIDTH)
    return x + (out * jax.nn.silu(gate)) @ w_out


def setup_inputs(seed: int = 0) -> dict:
    key = jax.random.key(seed)
    ks = jax.random.split(key, 18)
    f32 = jnp.float32

    def nrm(k, shape, scale):
        return jax.random.normal(k, shape, f32) * scale

    def gain(k, shape):
        return 1.0 + 0.05 * jax.random.normal(k, shape, f32)

    x = jax.random.normal(ks[0], (BATCH, SEQ, D_MODEL), f32)
    a_norm_g = gain(ks[1], (N_A_LAYERS, D_MODEL))
    a_w_in = nrm(ks[2], (N_A_LAYERS, D_MODEL, SSM_IN_DIM), D_MODEL ** -0.5)
    a_conv_w = nrm(ks[3], (N_A_LAYERS, SSM_CONV, SSM_CONV_DIM), SSM_CONV ** -0.5)
    a_conv_b = nrm(ks[4], (N_A_LAYERS, SSM_CONV_DIM), 0.01)
    u = jax.random.uniform(ks[5], (N_A_LAYERS, SSM_HEADS), f32)
    dt0 = jnp.exp(u * (math.log(DT_MAX) - math.log(DT_MIN)) + math.log(DT_MIN))
    a_dt_bias = dt0 + jnp.log(-jnp.expm1(-dt0))
    a_a_log = jnp.log(jax.random.uniform(ks[6], (N_A_LAYERS, SSM_HEADS), f32, 1.0, 16.0))
    a_d_skip = gain(ks[7], (N_A_LAYERS, SSM_HEADS))
    a_gate_norm_g = gain(ks[8], (N_A_LAYERS, SSM_D_INNER))
    a_w_out = nrm(ks[9], (N_A_LAYERS, SSM_D_INNER, D_MODEL), SSM_D_INNER ** -0.5)
    kv_norm_g = gain(ks[10], (D_MODEL,))
    w_kv = nrm(ks[11], (D_MODEL, KV_DIM), D_MODEL ** -0.5)
    b_norm_g = gain(ks[12], (N_B_LAYERS, D_MODEL))
    b_w_in = nrm(ks[13], (N_B_LAYERS, D_MODEL, 2 * DIFF_WIDTH), D_MODEL ** -0.5)
    b_lambda = nrm(ks[14], (N_B_LAYERS, 4, DIFF_HEAD_DIM), 0.1)
    b_sub_g = gain(ks[15], (N_B_LAYERS, DIFF_V_DIM))
    b_w_out = nrm(ks[16], (N_B_LAYERS, DIFF_WIDTH, D_MODEL), DIFF_WIDTH ** -0.5)
    final_norm_g = gain(ks[17], (D_MODEL,))
    return {"x": x, "a_norm_g": a_norm_g, "a_w_in": a_w_in, "a_conv_w": a_conv_w,
            "a_conv_b": a_conv_b, "a_dt_bias": a_dt_bias, "a_a_log": a_a_log,
            "a_d_skip": a_d_skip, "a_gate_norm_g": a_gate_norm_g, "a_w_out": a_w_out,
            "kv_norm_g": kv_norm_g, "w_kv": w_kv, "b_norm_g": b_norm_g, "b_w_in": b_w_in,
            "b_lambda": b_lambda, "b_sub_g": b_sub_g, "b_w_out": b_w_out,
            "final_norm_g": final_norm_g}


def reference(x, a_norm_g, a_w_in, a_conv_w, a_conv_b, a_dt_bias, a_a_log, a_d_skip,
              a_gate_norm_g, a_w_out, kv_norm_g, w_kv, b_norm_g, b_w_in, b_lambda,
              b_sub_g, b_w_out, final_norm_g):
    k = v = None
    for i in range(DEPTH):
        if i < N_A_LAYERS:
            x = mamba2_layer(x, a_norm_g[i], a_w_in[i], a_conv_w[i], a_conv_b[i],
                             a_dt_bias[i], a_a_log[i], a_d_skip[i], a_gate_norm_g[i], a_w_out[i])
            if i == N_A_LAYERS - 1:
                k, v = shared_kv(x, kv_norm_g, w_kv)
        else:
            j = i - N_A_LAYERS
            lambda_init = 0.8 - 0.6 * math.exp(-0.3 * i)
            x = diff_attention_layer(x, k, v, b_norm_g[j], b_w_in[j], b_lambda[j],
                                     b_sub_g[j], b_w_out[j], lambda_init)
    return rms_norm(x, final_norm_g)
```

```python
import functools
import math

import jax
import jax.numpy as jnp
import ml_dtypes
import numpy as np
from jax import lax
from jax.experimental import pallas as pl
from jax.experimental.pallas import tpu as pltpu

F32 = jnp.float32
BF16 = jnp.bfloat16

D_MODEL = 1024
DEPTH = 4
CHUNK = 64
N_B_LAYERS = DEPTH // 2
N_A_LAYERS = DEPTH - N_B_LAYERS
SSM_D_INNER = 2 * D_MODEL
SSM_HEAD_DIM = 64
SSM_HEADS = SSM_D_INNER // SSM_HEAD_DIM
SSM_GROUPS = 8
SSM_HPG = SSM_HEADS // SSM_GROUPS
SSM_STATE = 128
SSM_CONV = 4
SSM_GN = SSM_GROUPS * SSM_STATE
SSM_CONV_DIM = SSM_D_INNER + 2 * SSM_GN
DIFF_HEAD_DIM = 64
DIFF_V_DIM = 2 * DIFF_HEAD_DIM
DIFF_HEADS = D_MODEL // DIFF_V_DIM
DIFF_WIDTH = DIFF_HEADS * DIFF_V_DIM
ALIBI_MAX_EXP = 8.0
EPS = 1e-5

LANES = 128
SUBLANES = 8
VMEM_LIMIT = 56 * 1024 * 1024

ROW_TILE = 512
SSD_Q = 256
ATT_TQ = 256
ATT_TK = 256
N_CHUNK = 512
QK_PAD = 2 * DIFF_HEAD_DIM


def _cparams(sem):
    return pltpu.CompilerParams(dimension_semantics=sem, vmem_limit_bytes=VMEM_LIMIT)


def _rms(x, g):
    return x * lax.rsqrt(jnp.mean(x * x, axis=-1, keepdims=True) + EPS) * g


def _silu(x):
    return x / (1.0 + jnp.exp(-x))


def _softplus(x):
    return jnp.maximum(x, 0.0) + jnp.log1p(jnp.exp(-jnp.abs(x)))


def _trunc_bf16(a):
    bits = lax.bitcast_convert_type(a, jnp.int32) & jnp.int32(-65536)
    return lax.bitcast_convert_type(bits, F32)


def _split3(a):
    a1 = _trunc_bf16(a)
    r1 = a - a1
    a2 = _trunc_bf16(r1)
    return a1.astype(BF16), a2.astype(BF16), (r1 - a2).astype(BF16)


def _dot(a, b):
    return jnp.dot(a, b, preferred_element_type=F32)


def _dot_nt(a, b):
    return lax.dot_general(a, b, (((1,), (1,)), ((), ())), preferred_element_type=F32)


def _dot_tn(a, b):
    return lax.dot_general(a, b, (((0,), (0,)), ((), ())), preferred_element_type=F32)


def _a_in_kernel(x_ref, g_ref, w_ref, wdt_ref, o_ref, dt_ref):
    h = _rms(x_ref[...], g_ref[...]).astype(BF16)
    n_main = o_ref.shape[-1]
    for n0 in range(0, n_main, N_CHUNK):
        o_ref[:, n0:n0 + N_CHUNK] = _dot(h, w_ref[:, n0:n0 + N_CHUNK]).astype(BF16)
    dt_ref[...] = _dot(h, wdt_ref[...])


def _a_in_proj(x2, g, w_main, w_dt):
    t, d = x2.shape
    n_main = w_main.shape[1]
    return pl.pallas_call(
        _a_in_kernel,
        grid=(t // ROW_TILE,),
        in_specs=[
            pl.BlockSpec((ROW_TILE, d), lambda i: (i, 0)),
            pl.BlockSpec((1, d), lambda i: (0, 0)),
            pl.BlockSpec((d, n_main), lambda i: (0, 0)),
            pl.BlockSpec((d, LANES), lambda i: (0, 0)),
        ],
        out_specs=[
            pl.BlockSpec((ROW_TILE, n_main), lambda i: (i, 0)),
            pl.BlockSpec((ROW_TILE, LANES), lambda i: (i, 0)),
        ],
        out_shape=[
            jax.ShapeDtypeStruct((t, n_main), BF16),
            jax.ShapeDtypeStruct((t, LANES), F32),
        ],
        compiler_params=_cparams(("parallel",)),
        name="mamba_in_proj",
    )(x2, g, w_main, w_dt)


def _ssd_kernel(zx_ref, dtc_ref, dtr_ref, cw_ref, cb_ref, hp_row_ref, hp_col_ref, gng_ref,
                o_ref, buf_ref, tail_ref, state_ref):
    q = zx_ref.shape[0]
    g_n, r_n, p_n, n_n = SSM_GROUPS, SSM_HPG, SSM_HEAD_DIM, SSM_STATE
    di = SSM_D_INNER
    hn = SSM_HEADS
    c = pl.program_id(1)

    @pl.when(c == 0)
    def _():
        tail_ref[...] = jnp.zeros_like(tail_ref)
        state_ref[...] = jnp.zeros_like(state_ref)

    dtb_row, alog_row, dskip_row = hp_row_ref[0:1, :], hp_row_ref[1:2, :], hp_row_ref[2:3, :]
    dtb_col, alog_col = hp_col_ref[:, 0:1], hp_col_ref[:, 1:2]
    dt_c = _softplus(dtc_ref[:, 0:hn] + dtb_row)
    adt_c = dt_c * (-jnp.exp(alog_row))
    dt_r = _softplus(dtr_ref[...] + dtb_col)
    adt_r = dt_r * (-jnp.exp(alog_col))
    ri = lax.broadcasted_iota(jnp.int32, (q, q), 0)
    ci = lax.broadcasted_iota(jnp.int32, (q, q), 1)
    causal = ri >= ci
    tril = jnp.where(causal, 1.0, 0.0).astype(BF16)
    triu = jnp.where(ri <= ci, 1.0, 0.0).astype(BF16)
    acs_c = sum(_dot(tril, part) for part in _split3(adt_c))
    acs_r = sum(_dot(part, triu) for part in _split3(adt_r))
    tot_row = acs_c[q - 1:q, :]
    eacs_c = jnp.exp(acs_c)
    toend_c = jnp.exp(tot_row - acs_c) * dt_c
    etot_row = jnp.exp(tot_row)

    buf_ref[0:SUBLANES, :] = tail_ref[...]
    buf_ref[SUBLANES:SUBLANES + q, :] = zx_ref[:, di:].astype(F32)
    tail_ref[...] = buf_ref[q:q + SUBLANES, :]

    def conv_silu(c0, width):
        acc = cb_ref[:, c0:c0 + width]
        for k in range(SSM_CONV):
            off = SUBLANES - (SSM_CONV - 1) + k
            acc = acc + cw_ref[k:k + 1, c0:c0 + width] * buf_ref[off:off + q, c0:c0 + width]
        return _silu(acc)

    lane_head = lax.broadcasted_iota(jnp.int32, (q, r_n * p_n), 1) // p_n

    def per_head_lanes(col, g):
        out = jnp.broadcast_to(col[:, g * r_n:g * r_n + 1], (q, r_n * p_n))
        for r in range(1, r_n):
            hcol = jnp.broadcast_to(col[:, g * r_n + r:g * r_n + r + 1], (q, r_n * p_n))
            out = jnp.where(lane_head == r, hcol, out)
        return out

    lane_head_row = lane_head[0:1, :]

    def per_head_row(row, g):
        out = jnp.broadcast_to(row[:, g * r_n:g * r_n + 1], (1, r_n * p_n))
        for r in range(1, r_n):
            hval = jnp.broadcast_to(row[:, g * r_n + r:g * r_n + r + 1], (1, r_n * p_n))
            out = jnp.where(lane_head_row == r, hval, out)
        return out

    for g in range(g_n):
        xs = conv_silu(g * r_n * p_n, r_n * p_n)
        bm = conv_silu(di + g * n_n, n_n)
        cm = conv_silu(di + SSM_GN + g * n_n, n_n)
        bm16 = bm.astype(BF16)
        cm16 = cm.astype(BF16)
        cb = _dot_nt(cm16, bm16)
        xdt = xs * per_head_lanes(dt_c, g)
        st = state_ref[g]
        y = _dot(cm16, st.astype(BF16)) * per_head_lanes(eacs_c, g)
        for r in range(r_n):
            h = g * r_n + r
            seg = jnp.broadcast_to(acs_c[:, h:h + 1], (q, q)) - acs_r[h:h + 1, :]
            lmat = jnp.where(causal, cb * jnp.exp(seg), 0.0).astype(BF16)
            y = y + _dot(lmat, jnp.where(lane_head == r, xdt, 0.0).astype(BF16))
        y = y + per_head_row(dskip_row, g) * xs
        xend16 = (xs * per_head_lanes(toend_c, g)).astype(BF16)
        state_ref[g] = st * per_head_row(etot_row, g) + _dot_tn(bm16, xend16)
        z = zx_ref[:, g * r_n * p_n:(g + 1) * r_n * p_n].astype(F32)
        u = y * _silu(z)
        u = u * lax.rsqrt(jnp.mean(u * u, axis=-1, keepdims=True) + EPS)
        o_ref[:, g * r_n * p_n:(g + 1) * r_n * p_n] = (
            u * gng_ref[:, g * r_n * p_n:(g + 1) * r_n * p_n]).astype(BF16)


def _ssd(zx, dt_col, dt_row, conv_w, conv_b, hp_row, hp_col, gate_g, batch, seq):
    q = SSD_Q
    nc = seq // q
    n_main = zx.shape[1]
    hn = SSM_HEADS
    return pl.pallas_call(
        _ssd_kernel,
        grid=(batch, nc),
        in_specs=[
            pl.BlockSpec((q, n_main), lambda b, c: (b * nc + c, 0)),
            pl.BlockSpec((q, LANES), lambda b, c: (b * nc + c, 0)),
            pl.BlockSpec((None, hn, q), lambda b, c: (b, 0, c)),
            pl.BlockSpec((SSM_CONV, SSM_CONV_DIM), lambda b, c: (0, 0)),
            pl.BlockSpec((1, SSM_CONV_DIM), lambda b, c: (0, 0)),
            pl.BlockSpec((SUBLANES, hn), lambda b, c: (0, 0)),
            pl.BlockSpec((hn, SUBLANES), lambda b, c: (0, 0)),
            pl.BlockSpec((1, SSM_D_INNER), lambda b, c: (0, 0)),
        ],
        out_specs=pl.BlockSpec((q, SSM_D_INNER), lambda b, c: (b * nc + c, 0)),
        out_shape=jax.ShapeDtypeStruct((batch * seq, SSM_D_INNER), BF16),
        scratch_shapes=[
            pltpu.VMEM((q + SUBLANES, SSM_CONV_DIM), F32),
            pltpu.VMEM((SUBLANES, SSM_CONV_DIM), F32),
            pltpu.VMEM((SSM_GROUPS, SSM_STATE, SSM_HPG * SSM_HEAD_DIM), F32),
        ],
        compiler_params=_cparams(("parallel", "arbitrary")),
        name="mamba_ssd",
    )(zx, dt_col, dt_row, conv_w, conv_b, hp_row, hp_col, gate_g)


def _out_proj_kernel(y_ref, w_ref, x_ref, o_ref):
    o_ref[...] = x_ref[...] + _dot(y_ref[...], w_ref[...])


def _out_proj(y, w, x2):
    t, k = y.shape
    d = w.shape[1]
    return pl.pallas_call(
        _out_proj_kernel,
        grid=(t // ROW_TILE,),
        in_specs=[
            pl.BlockSpec((ROW_TILE, k), lambda i: (i, 0)),
            pl.BlockSpec((k, d), lambda i: (0, 0)),
            pl.BlockSpec((ROW_TILE, d), lambda i: (i, 0)),
        ],
        out_specs=pl.BlockSpec((ROW_TILE, d), lambda i: (i, 0)),
        out_shape=jax.ShapeDtypeStruct((t, d), F32),
        compiler_params=_cparams(("parallel",)),
        name="mamba_out_proj",
    )(y, w, x2)


def _gated_out_proj_kernel(a_ref, gate_ref, w_ref, x_ref, fg_ref, o_ref, *, final_norm):
    gate = gate_ref[...].astype(F32)
    y = (a_ref[...] * _silu(gate)).astype(BF16)
    out = x_ref[...] + _dot(y, w_ref[...])
    if final_norm:
        out = _rms(out, fg_ref[...])
    o_ref[...] = out


def _gated_out_proj(attn, gate, w, x2, final_g, final_norm):
    t, k = attn.shape
    d = w.shape[1]
    return pl.pallas_call(
        functools.partial(_gated_out_proj_kernel, final_norm=final_norm),
        grid=(t // ROW_TILE,),
        in_specs=[
            pl.BlockSpec((ROW_TILE, k), lambda i: (i, 0)),
            pl.BlockSpec((ROW_TILE, k), lambda i: (i, 0)),
            pl.BlockSpec((k, d), lambda i: (0, 0)),
            pl.BlockSpec((ROW_TILE, d), lambda i: (i, 0)),
            pl.BlockSpec((1, d), lambda i: (0, 0)),
        ],
        out_specs=pl.BlockSpec((ROW_TILE, d), lambda i: (i, 0)),
        out_shape=jax.ShapeDtypeStruct((t, d), F32),
        compiler_params=_cparams(("parallel",)),
        name="attn_out_proj",
    )(attn, gate, w, x2, final_g)


def _kv_kernel(x_ref, g_ref, wk_ref, wv_ref, aug_ref, k_ref, vt_ref):
    h = _rms(x_ref[...], g_ref[...]).astype(BF16)
    hn = DIFF_HEADS
    for n0 in range(0, wk_ref.shape[1], N_CHUNK):
        kc = _dot(h, wk_ref[:, n0:n0 + N_CHUNK])
        for j in range(N_CHUNK // QK_PAD):
            idx = n0 // QK_PAD + j
            k_ref[idx // 2, idx % 2] = (kc[:, j * QK_PAD:(j + 1) * QK_PAD]
                                        + aug_ref[idx // 2]).astype(BF16)
    for n0 in range(0, wv_ref.shape[1], N_CHUNK):
        vc = _dot(h, wv_ref[:, n0:n0 + N_CHUNK])
        for j in range(N_CHUNK // DIFF_V_DIM):
            head = n0 // DIFF_V_DIM + j
            vt_ref[head] = vc[:, j * DIFF_V_DIM:(j + 1) * DIFF_V_DIM].T.astype(BF16)
    del hn


def _kv_proj(x2, g, wk, wv, k_aug, batch, seq):
    d = x2.shape[1]
    nt = seq // ROW_TILE
    hn = DIFF_HEADS
    return pl.pallas_call(
        _kv_kernel,
        grid=(batch, nt),
        in_specs=[
            pl.BlockSpec((ROW_TILE, d), lambda b, i: (b * nt + i, 0)),
            pl.BlockSpec((1, d), lambda b, i: (0, 0)),
            pl.BlockSpec(wk.shape, lambda b, i: (0, 0)),
            pl.BlockSpec(wv.shape, lambda b, i: (0, 0)),
            pl.BlockSpec((hn, ROW_TILE, QK_PAD), lambda b, i: (0, i, 0)),
        ],
        out_specs=[
            pl.BlockSpec((None, hn, 2, ROW_TILE, QK_PAD), lambda b, i: (b, 0, 0, i, 0)),
            pl.BlockSpec((None, hn, DIFF_V_DIM, ROW_TILE), lambda b, i: (b, 0, 0, i)),
        ],
        out_shape=[
            jax.ShapeDtypeStruct((batch, hn, 2, seq, QK_PAD), BF16),
            jax.ShapeDtypeStruct((batch, hn, DIFF_V_DIM, seq), BF16),
        ],
        compiler_params=_cparams(("parallel", "parallel")),
        name="kv_proj",
    )(x2, g, wk, wv, k_aug)


def _b_in_kernel(x_ref, g_ref, wq_ref, wg_ref, aug_ref, q_ref, gate_ref):
    h = _rms(x_ref[...], g_ref[...]).astype(BF16)
    for n0 in range(0, wq_ref.shape[1], N_CHUNK):
        qc = _dot(h, wq_ref[:, n0:n0 + N_CHUNK])
        for j in range(N_CHUNK // QK_PAD):
            idx = n0 // QK_PAD + j
            q_ref[idx // 2, idx % 2] = (qc[:, j * QK_PAD:(j + 1) * QK_PAD] + aug_ref[...]).astype(BF16)
    for n0 in range(0, wg_ref.shape[1], N_CHUNK):
        gate_ref[:, n0:n0 + N_CHUNK] = _dot(h, wg_ref[:, n0:n0 + N_CHUNK]).astype(BF16)


def _b_in_proj(x2, g, wq, wg, q_aug, batch, seq):
    d = x2.shape[1]
    nt = seq // ROW_TILE
    hn = DIFF_HEADS
    return pl.pallas_call(
        _b_in_kernel,
        grid=(batch, nt),
        in_specs=[
            pl.BlockSpec((ROW_TILE, d), lambda b, i: (b * nt + i, 0)),
            pl.BlockSpec((1, d), lambda b, i: (0, 0)),
            pl.BlockSpec(wq.shape, lambda b, i: (0, 0)),
            pl.BlockSpec(wg.shape, lambda b, i: (0, 0)),
            pl.BlockSpec((ROW_TILE, QK_PAD), lambda b, i: (i, 0)),
        ],
        out_specs=[
            pl.BlockSpec((None, hn, 2, ROW_TILE, QK_PAD), lambda b, i: (b, 0, 0, i, 0)),
            pl.BlockSpec((ROW_TILE, DIFF_WIDTH), lambda b, i: (b * nt + i, 0)),
        ],
        out_shape=[
            jax.ShapeDtypeStruct((batch, hn, 2, seq, QK_PAD), BF16),
            jax.ShapeDtypeStruct((batch * seq, DIFF_WIDTH), BF16),
        ],
        compiler_params=_cparams(("parallel", "parallel")),
        name="attn_in_proj",
    )(x2, g, wq, wg, q_aug)


def _attn_kernel(lam_ref, q_ref, k_ref, vt_ref, dbias_ref, subg_ref, o_ref, *, lambda_init):
    tq, tk = ATT_TQ, ATT_TK
    seq = q_ref.shape[1]
    n_qt = seq // tq
    lf = lam_ref[...]
    lam = (jnp.exp(jnp.sum(lf[0:1, :] * lf[1:2, :], axis=-1, keepdims=True))
           - jnp.exp(jnp.sum(lf[2:3, :] * lf[3:4, :], axis=-1, keepdims=True)) + lambda_init)
    dbias = dbias_ref[...]
    neg = jnp.float32(-1e30)

    def q_tile(qi, carry):
        q0 = pl.multiple_of(qi * tq, tq)
        qs = [q_ref[i, pl.ds(q0, tq), :] for i in range(2)]

        def tile_update(j, st, diag):
            k0 = pl.multiple_of(j * tk, tk)
            vt = vt_ref[:, pl.ds(k0, tk)]
            new = []
            for i in range(2):
                m, l, acc = st[i]
                s = _dot_nt(k_ref[i, pl.ds(k0, tk), :], qs[i])
                if diag:
                    s = s + dbias
                m_new = jnp.maximum(m, jnp.max(s, axis=0, keepdims=True))
                alpha = jnp.exp(m - m_new)
                p = jnp.exp(s - m_new)
                l = alpha * l + jnp.sum(p, axis=0, keepdims=True)
                acc = alpha * acc + _dot(vt, p.astype(BF16))
                new.append((m_new, l, acc))
            return tuple(new)

        init = tuple((jnp.full((1, tq), neg, F32), jnp.zeros((1, tq), F32),
                      jnp.zeros((DIFF_V_DIM, tq), F32)) for _ in range(2))
        st = lax.fori_loop(0, qi, lambda j, s_: tile_update(j, s_, False), init)
        st = tile_update(qi, st, True)
        (_, l1, a1), (_, l2, a2) = st
        out_t = a1 / l1 - lam * (a2 / l2)
        out = out_t.T
        out = _rms(out, subg_ref[...]) * (1.0 - lambda_init)
        o_ref[pl.ds(q0, tq), :] = out
        return carry

    lax.fori_loop(0, n_qt, q_tile, 0)


def _attention(lam_qk, q, k, vt, dbias, sub_g, lambda_init):
    batch, hn, _, seq, _ = q.shape
    return pl.pallas_call(
        functools.partial(_attn_kernel, lambda_init=lambda_init),
        grid=(batch, hn),
        in_specs=[
            pl.BlockSpec(lam_qk.shape, lambda b, h: (0, 0)),
            pl.BlockSpec((None, None, 2, seq, QK_PAD), lambda b, h: (b, h, 0, 0, 0)),
            pl.BlockSpec((None, None, 2, seq, QK_PAD), lambda b, h: (b, h, 0, 0, 0)),
            pl.BlockSpec((None, None, DIFF_V_DIM, seq), lambda b, h: (b, h, 0, 0)),
            pl.BlockSpec((None, ATT_TK, ATT_TQ), lambda b, h: (h, 0, 0)),
            pl.BlockSpec((1, DIFF_V_DIM), lambda b, h: (0, 0)),
        ],
        out_specs=pl.BlockSpec((None, seq, DIFF_V_DIM), lambda b, h: (b, 0, h)),
        out_shape=jax.ShapeDtypeStruct((batch, seq, DIFF_WIDTH), F32),
        compiler_params=_cparams(("parallel", "parallel")),
        name="diff_attention",
    )(lam_qk, q, k, vt, dbias, sub_g)


def _np_split3(a):
    a = a.astype(np.float32)
    a1 = a.astype(ml_dtypes.bfloat16).astype(np.float32)
    r1 = a - a1
    a2 = r1.astype(ml_dtypes.bfloat16).astype(np.float32)
    a3 = (r1 - a2).astype(ml_dtypes.bfloat16).astype(np.float32)
    return a1, a2, a3


def _alibi_tables(seq):
    hn = DIFF_HEADS
    slopes = np.float32(2.0) ** (-np.float32(ALIBI_MAX_EXP) * np.arange(1, hn + 1, dtype=np.float32)
                                 / np.float32(hn))
    pos = np.arange(seq, dtype=np.float32)
    kp = _np_split3(slopes[:, None] * pos[None, :])
    sl = _np_split3(np.broadcast_to(slopes[:, None], (hn, seq)))
    pos_hi = np.floor(pos / 256.0) * 256.0
    pos_lo = pos - pos_hi
    ones = np.ones((seq,), np.float32)
    k_cols = list(kp)
    q_cols = [ones, ones, ones]
    for t in sl:
        k_cols += [-t, -t]
        q_cols += [pos_hi, pos_lo]
    lane_pad = (DIFF_HEAD_DIM, QK_PAD - DIFF_HEAD_DIM - len(k_cols))
    k_aug = jnp.pad(jnp.asarray(np.stack(k_cols, axis=-1)), ((0, 0), (0, 0), lane_pad))
    q_aug = jnp.pad(jnp.asarray(np.stack(q_cols, axis=-1)), ((0, 0), lane_pad))
    kk = np.arange(ATT_TK)[:, None]
    qq = np.arange(ATT_TQ)[None, :]
    allowed = (kk // CHUNK) <= (qq // CHUNK)
    fix = np.where(kk > qq, -2.0 * (kk - qq), 0.0).astype(np.float32)
    dbias = np.where(allowed[None], slopes[:, None, None] * fix[None], -np.inf).astype(np.float32)
    return k_aug, q_aug, jnp.asarray(dbias)


def _pad_heads(w, n_blocks, scale):
    d = w.shape[0]
    w = (w * scale).reshape(d, n_blocks, DIFF_HEAD_DIM)
    w = jnp.concatenate([w, jnp.zeros_like(w)], axis=-1)
    return w.reshape(d, n_blocks * QK_PAD).astype(BF16)


def kernel(x, a_norm_g, a_w_in, a_conv_w, a_conv_b, a_dt_bias, a_a_log, a_d_skip, a_gate_norm_g,
           a_w_out, kv_norm_g, w_kv, b_norm_g, b_w_in, b_lambda, b_sub_g, b_w_out, final_norm_g):
    batch, seq, d = x.shape
    t = batch * seq
    hn = SSM_HEADS
    x2 = x.reshape(t, d)

    for i in range(N_A_LAYERS):
        w_in = a_w_in[i]
        n_main = SSM_D_INNER + SSM_CONV_DIM
        w_main = w_in[:, :n_main].astype(BF16)
        w_dt = jnp.pad(w_in[:, n_main:], ((0, 0), (0, LANES - hn))).astype(BF16)
        zx, dt_raw = _a_in_proj(x2, a_norm_g[i][None], w_main, w_dt)
        dt_row = dt_raw[:, :hn].reshape(batch, seq, hn).transpose(0, 2, 1)
        hp_row = jnp.zeros((SUBLANES, hn), F32).at[0].set(a_dt_bias[i]).at[1].set(a_a_log[i]).at[2].set(a_d_skip[i])
        hp_col = hp_row.T
        y = _ssd(zx, dt_raw, dt_row, a_conv_w[i], a_conv_b[i][None], hp_row, hp_col,
                 a_gate_norm_g[i][None], batch, seq)
        x2 = _out_proj(y, a_w_out[i].astype(BF16), x2)

    k_aug, q_aug, dbias = _alibi_tables(seq)
    n_k = DIFF_HEADS * 2 * DIFF_HEAD_DIM
    wk = _pad_heads(w_kv[:, :n_k], DIFF_HEADS * 2, 1.0)
    wv = w_kv[:, n_k:].astype(BF16)
    k, vt = _kv_proj(x2, kv_norm_g[None], wk, wv, k_aug, batch, seq)

    scale = 1.0 / math.sqrt(DIFF_HEAD_DIM)
    for j in range(N_B_LAYERS):
        layer = N_A_LAYERS + j
        lambda_init = 0.8 - 0.6 * math.exp(-0.3 * layer)
        wq = _pad_heads(b_w_in[j][:, :DIFF_WIDTH], DIFF_HEADS * 2, scale)
        wg = b_w_in[j][:, DIFF_WIDTH:].astype(BF16)
        q, gate = _b_in_proj(x2, b_norm_g[j][None], wq, wg, q_aug, batch, seq)
        attn = _attention(b_lambda[j], q, k, vt, dbias, b_sub_g[j][None], lambda_init)
        x2 = _gated_out_proj(attn.reshape(t, DIFF_WIDTH), gate, b_w_out[j].astype(BF16), x2,
                             final_norm_g[None], final_norm=(j == N_B_LAYERS - 1))
    return x2.reshape(batch, seq, d)
```

```python
import functools
import math

import jax
import jax.numpy as jnp
import ml_dtypes
import numpy as np
from jax import lax
from jax.experimental import pallas as pl
from jax.experimental.pallas import tpu as pltpu

F32 = jnp.float32
BF16 = jnp.bfloat16

D_MODEL = 1024
DEPTH = 4
CHUNK = 64
N_B_LAYERS = DEPTH // 2
N_A_LAYERS = DEPTH - N_B_LAYERS
SSM_D_INNER = 2 * D_MODEL
SSM_HEAD_DIM = 64
SSM_HEADS = SSM_D_INNER // SSM_HEAD_DIM
SSM_GROUPS = 8
SSM_HPG = SSM_HEADS // SSM_GROUPS
SSM_STATE = 128
SSM_CONV = 4
SSM_GN = SSM_GROUPS * SSM_STATE
SSM_CONV_DIM = SSM_D_INNER + 2 * SSM_GN
DIFF_HEAD_DIM = 64
DIFF_V_DIM = 2 * DIFF_HEAD_DIM
DIFF_HEADS = D_MODEL // DIFF_V_DIM
DIFF_WIDTH = DIFF_HEADS * DIFF_V_DIM
ALIBI_MAX_EXP = 8.0
EPS = 1e-5
LOG2E = math.log2(math.e)
QK_SCALE = LOG2E / math.sqrt(DIFF_HEAD_DIM)

LANES = 128
SUBLANES = 8
VMEM_LIMIT = 56 * 1024 * 1024

ROW_TILE = 512
SSD_Q = 256
ATT_TQ = 256
ATT_TK = 256
N_CHUNK = 512
QK_PAD = 2 * DIFF_HEAD_DIM


def _cparams(sem):
    return pltpu.CompilerParams(dimension_semantics=sem, vmem_limit_bytes=VMEM_LIMIT)


def _rms(x, g):
    return x * lax.rsqrt(jnp.mean(x * x, axis=-1, keepdims=True) + EPS) * g


def _silu(x):
    return x / (1.0 + jnp.exp(-x))


def _softplus(x):
    return jnp.maximum(x, 0.0) + jnp.log1p(jnp.exp(-jnp.abs(x)))


def _trunc_bf16(a):
    bits = lax.bitcast_convert_type(a, jnp.int32) & jnp.int32(-65536)
    return lax.bitcast_convert_type(bits, F32)


def _split3(a):
    a1 = _trunc_bf16(a)
    r1 = a - a1
    a2 = _trunc_bf16(r1)
    return a1.astype(BF16), a2.astype(BF16), (r1 - a2).astype(BF16)


def _dot(a, b):
    return jnp.dot(a, b, preferred_element_type=F32)


def _dot_nt(a, b):
    return lax.dot_general(a, b, (((1,), (1,)), ((), ())), preferred_element_type=F32)


def _dot_tn(a, b):
    return lax.dot_general(a, b, (((0,), (0,)), ((), ())), preferred_element_type=F32)


def _a_in_kernel(x_ref, g_ref, w_ref, wdt_ref, o_ref, dt_ref):
    h = _rms(x_ref[...], g_ref[...]).astype(BF16)
    n_main = o_ref.shape[-1]
    for n0 in range(0, n_main, N_CHUNK):
        o_ref[:, n0:n0 + N_CHUNK] = _dot(h, w_ref[:, n0:n0 + N_CHUNK]).astype(BF16)
    dt_ref[...] = _dot(h, wdt_ref[...])


def _a_in_proj(x2, g, w_main, w_dt):
    t, d = x2.shape
    n_main = w_main.shape[1]
    return pl.pallas_call(
        _a_in_kernel,
        grid=(t // ROW_TILE,),
        in_specs=[
            pl.BlockSpec((ROW_TILE, d), lambda i: (i, 0)),
            pl.BlockSpec((1, d), lambda i: (0, 0)),
            pl.BlockSpec((d, n_main), lambda i: (0, 0)),
            pl.BlockSpec((d, LANES), lambda i: (0, 0)),
        ],
        out_specs=[
            pl.BlockSpec((ROW_TILE, n_main), lambda i: (i, 0)),
            pl.BlockSpec((ROW_TILE, LANES), lambda i: (i, 0)),
        ],
        out_shape=[
            jax.ShapeDtypeStruct((t, n_main), BF16),
            jax.ShapeDtypeStruct((t, LANES), F32),
        ],
        compiler_params=_cparams(("parallel",)),
        name="mamba_in_proj",
    )(x2, g, w_main, w_dt)


def _ssd_kernel(zx_ref, dtc_ref, dtr_ref, cw_ref, cb_ref, hp_row_ref, hp_col_ref, gng_ref,
                o_ref, buf_ref, tail_ref, state_ref):
    q = zx_ref.shape[0]
    g_n, r_n, p_n, n_n = SSM_GROUPS, SSM_HPG, SSM_HEAD_DIM, SSM_STATE
    di = SSM_D_INNER
    hn = SSM_HEADS
    c = pl.program_id(1)

    @pl.when(c == 0)
    def _():
        tail_ref[...] = jnp.zeros_like(tail_ref)
        state_ref[...] = jnp.zeros_like(state_ref)

    dtb_row, alog_row, dskip_row = hp_row_ref[0:1, :], hp_row_ref[1:2, :], hp_row_ref[2:3, :]
    dtb_col, alog_col = hp_col_ref[:, 0:1], hp_col_ref[:, 1:2]
    dt_c = _softplus(dtc_ref[:, 0:hn] + dtb_row)
    adt_c = dt_c * (-jnp.exp(alog_row))
    dt_r = _softplus(dtr_ref[...] + dtb_col)
    adt_r = dt_r * (-jnp.exp(alog_col))
    ri = lax.broadcasted_iota(jnp.int32, (q, q), 0)
    ci = lax.broadcasted_iota(jnp.int32, (q, q), 1)
    causal = ri >= ci
    tril = jnp.where(causal, 1.0, 0.0).astype(BF16)
    triu = jnp.where(ri <= ci, 1.0, 0.0).astype(BF16)
    acs_c = sum(_dot(tril, part) for part in _split3(adt_c))
    acs_r = sum(_dot(part, triu) for part in _split3(adt_r))
    tot_row = acs_c[q - 1:q, :]
    eacs_c = jnp.exp(acs_c)
    toend_c = jnp.exp(tot_row - acs_c) * dt_c
    etot_row = jnp.exp(tot_row)

    buf_ref[0:SUBLANES, :] = tail_ref[...]
    buf_ref[SUBLANES:SUBLANES + q, :] = zx_ref[:, di:].astype(F32)
    tail_ref[...] = buf_ref[q:q + SUBLANES, :]

    def conv_silu(c0, width):
        acc = cb_ref[:, c0:c0 + width]
        for k in range(SSM_CONV):
            off = SUBLANES - (SSM_CONV - 1) + k
            acc = acc + cw_ref[k:k + 1, c0:c0 + width] * buf_ref[off:off + q, c0:c0 + width]
        return _silu(acc)

    lane_head = lax.broadcasted_iota(jnp.int32, (q, r_n * p_n), 1) // p_n

    def per_head_lanes(col, g):
        out = jnp.broadcast_to(col[:, g * r_n:g * r_n + 1], (q, r_n * p_n))
        for r in range(1, r_n):
            hcol = jnp.broadcast_to(col[:, g * r_n + r:g * r_n + r + 1], (q, r_n * p_n))
            out = jnp.where(lane_head == r, hcol, out)
        return out

    lane_head_row = lane_head[0:1, :]

    def per_head_row(row, g):
        out = jnp.broadcast_to(row[:, g * r_n:g * r_n + 1], (1, r_n * p_n))
        for r in range(1, r_n):
            hval = jnp.broadcast_to(row[:, g * r_n + r:g * r_n + r + 1], (1, r_n * p_n))
            out = jnp.where(lane_head_row == r, hval, out)
        return out

    for g in range(g_n):
        xs = conv_silu(g * r_n * p_n, r_n * p_n)
        bm = conv_silu(di + g * n_n, n_n)
        cm = conv_silu(di + SSM_GN + g * n_n, n_n)
        bm16 = bm.astype(BF16)
        cm16 = cm.astype(BF16)
        cb = _dot_nt(cm16, bm16)
        xdt = xs * per_head_lanes(dt_c, g)
        st = state_ref[g]
        y = _dot(cm16, st.astype(BF16)) * per_head_lanes(eacs_c, g)
        for r in range(r_n):
            h = g * r_n + r
            seg = jnp.broadcast_to(acs_c[:, h:h + 1], (q, q)) - acs_r[h:h + 1, :]
            lmat = jnp.where(causal, cb * jnp.exp(seg), 0.0).astype(BF16)
            y = y + _dot(lmat, jnp.where(lane_head == r, xdt, 0.0).astype(BF16))
        y = y + per_head_row(dskip_row, g) * xs
        xend16 = (xs * per_head_lanes(toend_c, g)).astype(BF16)
        state_ref[g] = st * per_head_row(etot_row, g) + _dot_tn(bm16, xend16)
        z = zx_ref[:, g * r_n * p_n:(g + 1) * r_n * p_n].astype(F32)
        u = y * _silu(z)
        u = u * lax.rsqrt(jnp.mean(u * u, axis=-1, keepdims=True) + EPS)
        o_ref[:, g * r_n * p_n:(g + 1) * r_n * p_n] = (
            u * gng_ref[:, g * r_n * p_n:(g + 1) * r_n * p_n]).astype(BF16)


def _ssd(zx, dt_col, dt_row, conv_w, conv_b, hp_row, hp_col, gate_g, batch, seq):
    q = SSD_Q
    nc = seq // q
    n_main = zx.shape[1]
    hn = SSM_HEADS
    return pl.pallas_call(
        _ssd_kernel,
        grid=(batch, nc),
        in_specs=[
            pl.BlockSpec((q, n_main), lambda b, c: (b * nc + c, 0)),
            pl.BlockSpec((q, LANES), lambda b, c: (b * nc + c, 0)),
            pl.BlockSpec((None, hn, q), lambda b, c: (b, 0, c)),
            pl.BlockSpec((SSM_CONV, SSM_CONV_DIM), lambda b, c: (0, 0)),
            pl.BlockSpec((1, SSM_CONV_DIM), lambda b, c: (0, 0)),
            pl.BlockSpec((SUBLANES, hn), lambda b, c: (0, 0)),
            pl.BlockSpec((hn, SUBLANES), lambda b, c: (0, 0)),
            pl.BlockSpec((1, SSM_D_INNER), lambda b, c: (0, 0)),
        ],
        out_specs=pl.BlockSpec((q, SSM_D_INNER), lambda b, c: (b * nc + c, 0)),
        out_shape=jax.ShapeDtypeStruct((batch * seq, SSM_D_INNER), BF16),
        scratch_shapes=[
            pltpu.VMEM((q + SUBLANES, SSM_CONV_DIM), F32),
            pltpu.VMEM((SUBLANES, SSM_CONV_DIM), F32),
            pltpu.VMEM((SSM_GROUPS, SSM_STATE, SSM_HPG * SSM_HEAD_DIM), F32),
        ],
        compiler_params=_cparams(("parallel", "arbitrary")),
        name="mamba_ssd",
    )(zx, dt_col, dt_row, conv_w, conv_b, hp_row, hp_col, gate_g)


def _out_proj_kernel(y_ref, w_ref, x_ref, o_ref):
    o_ref[...] = x_ref[...] + _dot(y_ref[...], w_ref[...])


def _out_proj(y, w, x2):
    t, k = y.shape
    d = w.shape[1]
    return pl.pallas_call(
        _out_proj_kernel,
        grid=(t // ROW_TILE,),
        in_specs=[
            pl.BlockSpec((ROW_TILE, k), lambda i: (i, 0)),
            pl.BlockSpec((k, d), lambda i: (0, 0)),
            pl.BlockSpec((ROW_TILE, d), lambda i: (i, 0)),
        ],
        out_specs=pl.BlockSpec((ROW_TILE, d), lambda i: (i, 0)),
        out_shape=jax.ShapeDtypeStruct((t, d), F32),
        compiler_params=_cparams(("parallel",)),
        name="mamba_out_proj",
    )(y, w, x2)


def _gated_out_proj_kernel(a_ref, gate_ref, w_ref, x_ref, fg_ref, o_ref, *, final_norm):
    gate = gate_ref[...].astype(F32)
    y = (a_ref[...] * _silu(gate)).astype(BF16)
    out = x_ref[...] + _dot(y, w_ref[...])
    if final_norm:
        out = _rms(out, fg_ref[...])
    o_ref[...] = out


def _gated_out_proj(attn, gate, w, x2, final_g, final_norm):
    t, k = attn.shape
    d = w.shape[1]
    return pl.pallas_call(
        functools.partial(_gated_out_proj_kernel, final_norm=final_norm),
        grid=(t // ROW_TILE,),
        in_specs=[
            pl.BlockSpec((ROW_TILE, k), lambda i: (i, 0)),
            pl.BlockSpec((ROW_TILE, k), lambda i: (i, 0)),
            pl.BlockSpec((k, d), lambda i: (0, 0)),
            pl.BlockSpec((ROW_TILE, d), lambda i: (i, 0)),
            pl.BlockSpec((1, d), lambda i: (0, 0)),
        ],
        out_specs=pl.BlockSpec((ROW_TILE, d), lambda i: (i, 0)),
        out_shape=jax.ShapeDtypeStruct((t, d), F32),
        compiler_params=_cparams(("parallel",)),
        name="attn_out_proj",
    )(attn, gate, w, x2, final_g)


def _kv_kernel(x_ref, g_ref, wk_ref, wv_ref, aug_ref, k_ref, vt_ref):
    h = _rms(x_ref[...], g_ref[...]).astype(BF16)
    hn = DIFF_HEADS
    for n0 in range(0, wk_ref.shape[1], N_CHUNK):
        kc = _dot(h, wk_ref[:, n0:n0 + N_CHUNK])
        for j in range(N_CHUNK // QK_PAD):
            idx = n0 // QK_PAD + j
            k_ref[idx // 2, idx % 2] = (kc[:, j * QK_PAD:(j + 1) * QK_PAD]
                                        + aug_ref[idx // 2]).astype(BF16)
    for n0 in range(0, wv_ref.shape[1], N_CHUNK):
        vc = _dot(h, wv_ref[:, n0:n0 + N_CHUNK])
        for j in range(N_CHUNK // DIFF_V_DIM):
            head = n0 // DIFF_V_DIM + j
            vt_ref[head] = vc[:, j * DIFF_V_DIM:(j + 1) * DIFF_V_DIM].T.astype(BF16)
    del hn


def _kv_proj(x2, g, wk, wv, k_aug, batch, seq):
    d = x2.shape[1]
    nt = seq // ROW_TILE
    hn = DIFF_HEADS
    return pl.pallas_call(
        _kv_kernel,
        grid=(batch, nt),
        in_specs=[
            pl.BlockSpec((ROW_TILE, d), lambda b, i: (b * nt + i, 0)),
            pl.BlockSpec((1, d), lambda b, i: (0, 0)),
            pl.BlockSpec(wk.shape, lambda b, i: (0, 0)),
            pl.BlockSpec(wv.shape, lambda b, i: (0, 0)),
            pl.BlockSpec((hn, ROW_TILE, QK_PAD), lambda b, i: (0, i, 0)),
        ],
        out_specs=[
            pl.BlockSpec((None, hn, 2, ROW_TILE, QK_PAD), lambda b, i: (b, 0, 0, i, 0)),
            pl.BlockSpec((None, hn, DIFF_V_DIM, ROW_TILE), lambda b, i: (b, 0, 0, i)),
        ],
        out_shape=[
            jax.ShapeDtypeStruct((batch, hn, 2, seq, QK_PAD), BF16),
            jax.ShapeDtypeStruct((batch, hn, DIFF_V_DIM, seq), BF16),
        ],
        compiler_params=_cparams(("parallel", "parallel")),
        name="kv_proj",
    )(x2, g, wk, wv, k_aug)


def _b_in_kernel(x_ref, g_ref, wq_ref, wg_ref, aug_ref, q_ref, gate_ref):
    h = _rms(x_ref[...], g_ref[...]).astype(BF16)
    for n0 in range(0, wq_ref.shape[1], N_CHUNK):
        qc = _dot(h, wq_ref[:, n0:n0 + N_CHUNK]) * QK_SCALE
        for j in range(N_CHUNK // QK_PAD):
            idx = n0 // QK_PAD + j
            q_ref[idx // 2, idx % 2] = (qc[:, j * QK_PAD:(j + 1) * QK_PAD] + aug_ref[...]).astype(BF16)
    for n0 in range(0, wg_ref.shape[1], N_CHUNK):
        gate_ref[:, n0:n0 + N_CHUNK] = _dot(h, wg_ref[:, n0:n0 + N_CHUNK]).astype(BF16)


def _b_in_proj(x2, g, wq, wg, q_aug, batch, seq):
    d = x2.shape[1]
    nt = seq // ROW_TILE
    hn = DIFF_HEADS
    return pl.pallas_call(
        _b_in_kernel,
        grid=(batch, nt),
        in_specs=[
            pl.BlockSpec((ROW_TILE, d), lambda b, i: (b * nt + i, 0)),
            pl.BlockSpec((1, d), lambda b, i: (0, 0)),
            pl.BlockSpec(wq.shape, lambda b, i: (0, 0)),
            pl.BlockSpec(wg.shape, lambda b, i: (0, 0)),
            pl.BlockSpec((ROW_TILE, QK_PAD), lambda b, i: (i, 0)),
        ],
        out_specs=[
            pl.BlockSpec((None, hn, 2, ROW_TILE, QK_PAD), lambda b, i: (b, 0, 0, i, 0)),
            pl.BlockSpec((ROW_TILE, DIFF_WIDTH), lambda b, i: (b * nt + i, 0)),
        ],
        out_shape=[
            jax.ShapeDtypeStruct((batch, hn, 2, seq, QK_PAD), BF16),
            jax.ShapeDtypeStruct((batch * seq, DIFF_WIDTH), BF16),
        ],
        compiler_params=_cparams(("parallel", "parallel")),
        name="attn_in_proj",
    )(x2, g, wq, wg, q_aug)


def _attn_kernel(lam_ref, q_ref, k_ref, vt_ref, dbias_ref, subg_ref, o_ref,
                 s_a, s_b, p_a, p_b, *, lambda_init):
    tq, tk = ATT_TQ, ATT_TK
    seq = q_ref.shape[1]
    n_qt = seq // tq
    lf = lam_ref[...]
    lam = (jnp.exp(jnp.sum(lf[0:1, :] * lf[1:2, :], axis=-1, keepdims=True))
           - jnp.exp(jnp.sum(lf[2:3, :] * lf[3:4, :], axis=-1, keepdims=True)) + lambda_init)
    neg = jnp.float32(-1e30)

    def fold(x, op):
        return op(x.reshape(tk // SUBLANES, SUBLANES, tq), axis=0)

    def q_tile(qi, carry):
        q0 = pl.multiple_of(qi * tq, tq)
        qs = [q_ref[i, pl.ds(q0, tq), :] for i in range(2)]

        def stage_a(t, s_buf):
            k0 = pl.multiple_of(t * tk, tk)
            mparts = []
            for i in range(2):
                s = _dot_nt(k_ref[i, pl.ds(k0, tk), :], qs[i])
                s_buf[i] = s
                mparts.append(fold(s, jnp.max))
            return tuple(mparts)

        def stage_b(s_buf, p_buf, mparts, ms, ls, diag):
            new_m, new_l, alphas = [], [], []
            for i in range(2):
                if diag:
                    s = s_buf[i] + dbias_ref[...]
                    mpart = fold(s, jnp.max)
                else:
                    s = s_buf[i]
                    mpart = mparts[i]
                m_new = jnp.maximum(ms[i], jnp.max(mpart, axis=0, keepdims=True))
                alpha = jnp.exp2(ms[i] - m_new)
                p = jnp.exp2(s - m_new)
                new_l.append(alpha * ls[i] + fold(p, jnp.sum))
                p_buf[i] = p.astype(BF16)
                new_m.append(m_new)
                alphas.append(alpha)
            return tuple(new_m), tuple(new_l), tuple(alphas)

        def stage_c(t, p_buf, alphas, accs):
            k0 = pl.multiple_of(jnp.maximum(t, 0) * tk, tk)
            vt = vt_ref[:, pl.ds(k0, tk)]
            return tuple(alphas[i] * accs[i] + _dot(vt, p_buf[i]) for i in range(2))

        def step(j, bufs, st):
            s_cur, s_nxt, p_cur, p_prv = bufs
            mparts, ms, ls, alphas, accs = st
            mparts_next = stage_a(j + 1, s_nxt)
            ms, ls, alphas_new = stage_b(s_cur, p_cur, mparts, ms, ls, False)
            accs = stage_c(j - 1, p_prv, alphas, accs)
            return mparts_next, ms, ls, alphas_new, accs

        def finish(bufs, st):
            s_cur, _, p_cur, p_prv = bufs
            _, ms, ls, alphas, accs = st
            _, ls, alphas_new = stage_b(s_cur, p_cur, None, ms, ls, True)
            accs = stage_c(qi - 1, p_prv, alphas, accs)
            accs = stage_c(qi, p_cur, alphas_new, accs)
            l1 = jnp.sum(ls[0], axis=0, keepdims=True)
            l2 = jnp.sum(ls[1], axis=0, keepdims=True)
            out_t = accs[0] / l1 - lam * (accs[1] / l2)
            out = _rms(out_t.T, subg_ref[...]) * (1.0 - lambda_init)
            o_ref[pl.ds(q0, tq), :] = out

        even = (s_a, s_b, p_a, p_b)
        odd = (s_b, s_a, p_b, p_a)
        p_b[...] = jnp.zeros_like(p_b)
        init = (stage_a(0, s_a),
                tuple(jnp.full((1, tq), neg, F32) for _ in range(2)),
                tuple(jnp.zeros((SUBLANES, tq), F32) for _ in range(2)),
                tuple(jnp.ones((1, tq), F32) for _ in range(2)),
                tuple(jnp.zeros((DIFF_V_DIM, tq), F32) for _ in range(2)))
        st = lax.fori_loop(0, qi // 2,
                           lambda jj, c: step(2 * jj + 1, odd, step(2 * jj, even, c)), init)

        @pl.when(qi % 2 == 1)
        def _():
            finish(odd, step(qi - 1, even, st))

        @pl.when(qi % 2 == 0)
        def _():
            finish(even, st)

        return carry

    lax.fori_loop(0, n_qt, q_tile, 0)


def _attention(lam_qk, q, k, vt, dbias, sub_g, lambda_init):
    batch, hn, _, seq, _ = q.shape
    return pl.pallas_call(
        functools.partial(_attn_kernel, lambda_init=lambda_init),
        grid=(batch, hn),
        in_specs=[
            pl.BlockSpec(lam_qk.shape, lambda b, h: (0, 0)),
            pl.BlockSpec((None, None, 2, seq, QK_PAD), lambda b, h: (b, h, 0, 0, 0)),
            pl.BlockSpec((None, None, 2, seq, QK_PAD), lambda b, h: (b, h, 0, 0, 0)),
            pl.BlockSpec((None, None, DIFF_V_DIM, seq), lambda b, h: (b, h, 0, 0)),
            pl.BlockSpec((None, ATT_TK, ATT_TQ), lambda b, h: (h, 0, 0)),
            pl.BlockSpec((1, DIFF_V_DIM), lambda b, h: (0, 0)),
        ],
        out_specs=pl.BlockSpec((None, seq, DIFF_V_DIM), lambda b, h: (b, 0, h)),
        out_shape=jax.ShapeDtypeStruct((batch, seq, DIFF_WIDTH), F32),
        scratch_shapes=[
            pltpu.VMEM((2, ATT_TK, ATT_TQ), F32),
            pltpu.VMEM((2, ATT_TK, ATT_TQ), F32),
            pltpu.VMEM((2, ATT_TK, ATT_TQ), BF16),
            pltpu.VMEM((2, ATT_TK, ATT_TQ), BF16),
        ],
        compiler_params=_cparams(("parallel", "parallel")),
        name="diff_attention",
    )(lam_qk, q, k, vt, dbias, sub_g)


def _np_split3(a):
    a = a.astype(np.float32)
    a1 = a.astype(ml_dtypes.bfloat16).astype(np.float32)
    r1 = a - a1
    a2 = r1.astype(ml_dtypes.bfloat16).astype(np.float32)
    a3 = (r1 - a2).astype(ml_dtypes.bfloat16).astype(np.float32)
    return a1, a2, a3


def _alibi_tables(seq):
    hn = DIFF_HEADS
    slopes = np.float32(2.0) ** (-np.float32(ALIBI_MAX_EXP) * np.arange(1, hn + 1, dtype=np.float32)
                                 / np.float32(hn))
    slopes = (slopes.astype(np.float64) * LOG2E).astype(np.float32)
    pos = np.arange(seq, dtype=np.float32)
    kp = _np_split3(slopes[:, None] * pos[None, :])
    sl = _np_split3(np.broadcast_to(slopes[:, None], (hn, seq)))
    pos_hi = np.floor(pos / 256.0) * 256.0
    pos_lo = pos - pos_hi
    ones = np.ones((seq,), np.float32)
    k_cols = list(kp)
    q_cols = [ones, ones, ones]
    for t in sl:
        k_cols += [-t, -t]
        q_cols += [pos_hi, pos_lo]
    lane_pad = (DIFF_HEAD_DIM, QK_PAD - DIFF_HEAD_DIM - len(k_cols))
    k_aug = jnp.pad(jnp.asarray(np.stack(k_cols, axis=-1)), ((0, 0), (0, 0), lane_pad))
    q_aug = jnp.pad(jnp.asarray(np.stack(q_cols, axis=-1)), ((0, 0), lane_pad))
    kk = np.arange(ATT_TK)[:, None]
    qq = np.arange(ATT_TQ)[None, :]
    allowed = (kk // CHUNK) <= (qq // CHUNK)
    fix = np.where(kk > qq, -2.0 * (kk - qq), 0.0).astype(np.float32)
    dbias = np.where(allowed[None], slopes[:, None, None] * fix[None], -np.inf).astype(np.float32)
    return k_aug, q_aug, jnp.asarray(dbias)


def _pad_heads(w, n_blocks):
    d = w.shape[0]
    w = w.reshape(d, n_blocks, DIFF_HEAD_DIM)
    w = jnp.concatenate([w, jnp.zeros_like(w)], axis=-1)
    return w.reshape(d, n_blocks * QK_PAD).astype(BF16)


def kernel(x, a_norm_g, a_w_in, a_conv_w, a_conv_b, a_dt_bias, a_a_log, a_d_skip, a_gate_norm_g,
           a_w_out, kv_norm_g, w_kv, b_norm_g, b_w_in, b_lambda, b_sub_g, b_w_out, final_norm_g):
    batch, seq, d = x.shape
    t = batch * seq
    hn = SSM_HEADS
    x2 = x.reshape(t, d)

    for i in range(N_A_LAYERS):
        w_in = a_w_in[i]
        n_main = SSM_D_INNER + SSM_CONV_DIM
        w_main = w_in[:, :n_main].astype(BF16)
        w_dt = jnp.pad(w_in[:, n_main:], ((0, 0), (0, LANES - hn))).astype(BF16)
        zx, dt_raw = _a_in_proj(x2, a_norm_g[i][None], w_main, w_dt)
        dt_row = dt_raw[:, :hn].reshape(batch, seq, hn).transpose(0, 2, 1)
        hp_row = jnp.zeros((SUBLANES, hn), F32).at[0].set(a_dt_bias[i]).at[1].set(a_a_log[i]).at[2].set(a_d_skip[i])
        hp_col = hp_row.T
        y = _ssd(zx, dt_raw, dt_row, a_conv_w[i], a_conv_b[i][None], hp_row, hp_col,
                 a_gate_norm_g[i][None], batch, seq)
        x2 = _out_proj(y, a_w_out[i].astype(BF16), x2)

    k_aug, q_aug, dbias = _alibi_tables(seq)
    n_k = DIFF_HEADS * 2 * DIFF_HEAD_DIM
    wk = _pad_heads(w_kv[:, :n_k], DIFF_HEADS * 2)
    wv = w_kv[:, n_k:].astype(BF16)
    k, vt = _kv_proj(x2, kv_norm_g[None], wk, wv, k_aug, batch, seq)

    for j in range(N_B_LAYERS):
        layer = N_A_LAYERS + j
        lambda_init = 0.8 - 0.6 * math.exp(-0.3 * layer)
        wq = _pad_heads(b_w_in[j][:, :DIFF_WIDTH], DIFF_HEADS * 2)
        wg = b_w_in[j][:, DIFF_WIDTH:].astype(BF16)
        q, gate = _b_in_proj(x2, b_norm_g[j][None], wq, wg, q_aug, batch, seq)
        attn = _attention(b_lambda[j], q, k, vt, dbias, b_sub_g[j][None], lambda_init)
        x2 = _gated_out_proj(attn.reshape(t, DIFF_WIDTH), gate, b_w_out[j].astype(BF16), x2,
                             final_norm_g[None], final_norm=(j == N_B_LAYERS - 1))
    return x2.reshape(batch, seq, d)
```

```python
import functools
import math

import jax
import jax.numpy as jnp
import ml_dtypes
import numpy as np
from jax import lax
from jax.experimental import pallas as pl
from jax.experimental.pallas import tpu as pltpu

F32 = jnp.float32
BF16 = jnp.bfloat16

D_MODEL = 1024
DEPTH = 4
CHUNK = 64
N_B_LAYERS = DEPTH // 2
N_A_LAYERS = DEPTH - N_B_LAYERS
SSM_D_INNER = 2 * D_MODEL
SSM_HEAD_DIM = 64
SSM_HEADS = SSM_D_INNER // SSM_HEAD_DIM
SSM_GROUPS = 8
SSM_HPG = SSM_HEADS // SSM_GROUPS
SSM_STATE = 128
SSM_CONV = 4
SSM_GN = SSM_GROUPS * SSM_STATE
SSM_CONV_DIM = SSM_D_INNER + 2 * SSM_GN
DIFF_HEAD_DIM = 64
DIFF_V_DIM = 2 * DIFF_HEAD_DIM
DIFF_HEADS = D_MODEL // DIFF_V_DIM
DIFF_WIDTH = DIFF_HEADS * DIFF_V_DIM
ALIBI_MAX_EXP = 8.0
EPS = 1e-5
LOG2E = math.log2(math.e)
QK_SCALE = LOG2E / math.sqrt(DIFF_HEAD_DIM)

LANES = 128
SUBLANES = 8
VMEM_LIMIT = 56 * 1024 * 1024

ROW_TILE = 512
SSD_Q = 256
ATT_TQ = 256
ATT_TK = 256
ATT_UNROLL = 4
N_CHUNK = 256
CONV_GROUPS = 8
QK_PAD = 2 * DIFF_HEAD_DIM


def _cparams(sem):
    return pltpu.CompilerParams(dimension_semantics=sem, vmem_limit_bytes=VMEM_LIMIT)


def _rms(x, g):
    return x * lax.rsqrt(jnp.mean(x * x, axis=-1, keepdims=True) + EPS) * g


def _silu_of_half(h):
    return h * jnp.tanh(h) + h


def _silu(x):
    return _silu_of_half(0.5 * x)


def _softplus(x):
    return jnp.maximum(x, 0.0) + jnp.log1p(jnp.exp(-jnp.abs(x)))


def _trunc_bf16(a):
    bits = lax.bitcast_convert_type(a, jnp.int32) & jnp.int32(-65536)
    return lax.bitcast_convert_type(bits, F32)


def _split3(a):
    a1 = _trunc_bf16(a)
    r1 = a - a1
    a2 = _trunc_bf16(r1)
    return a1.astype(BF16), a2.astype(BF16), (r1 - a2).astype(BF16)


def _dot(a, b):
    return jnp.dot(a, b, preferred_element_type=F32)


def _dot_nt(a, b):
    return lax.dot_general(a, b, (((1,), (1,)), ((), ())), preferred_element_type=F32)


def _dot_tn(a, b):
    return lax.dot_general(a, b, (((0,), (0,)), ((), ())), preferred_element_type=F32)


def _a_in_kernel(x_ref, g_ref, w_ref, wdt_ref, cw_ref, cb_ref, o_ref, dt_ref, tail_ref,
                 *, tiles_per_seq):
    rows = x_ref.shape[0]
    di = SSM_D_INNER
    n_main = o_ref.shape[-1]

    @pl.when(pl.program_id(0) % tiles_per_seq == 0)
    def _():
        tail_ref[...] = jnp.zeros_like(tail_ref)

    h = _rms(x_ref[...], g_ref[...]).astype(BF16)
    sub = lax.broadcasted_iota(jnp.int32, (1, SUBLANES, N_CHUNK), 1)
    z_chunks = list(range(0, di, N_CHUNK))
    conv_chunks = list(range(di, n_main, N_CHUNK))
    order = []
    while z_chunks or conv_chunks:
        order += conv_chunks[:2] + z_chunks[:1]
        conv_chunks, z_chunks = conv_chunks[2:], z_chunks[1:]
    nxt = _dot(h, w_ref[:, order[0]:order[0] + N_CHUNK])
    for idx, n0 in enumerate(order):
        a = nxt
        if idx + 1 < len(order):
            nxt = _dot(h, w_ref[:, order[idx + 1]:order[idx + 1] + N_CHUNK])
        if n0 < di:
            o_ref[:, n0:n0 + N_CHUNK] = _silu_of_half(a).astype(BF16)
            continue
        c0 = n0 - di
        tail = tail_ref[:, c0:c0 + N_CHUNK][None]
        tail_ref[:, c0:c0 + N_CHUNK] = a[rows - SUBLANES:rows, :]
        a3 = a.reshape(rows // SUBLANES, SUBLANES, N_CHUNK)
        for g0 in range(0, rows // SUBLANES, CONV_GROUPS):
            rot = a3[g0:g0 + CONV_GROUPS]
            prev = tail if g0 == 0 else a3[g0 - 1:g0]
            acc = cb_ref[:, c0:c0 + N_CHUNK] + cw_ref[SSM_CONV - 1:SSM_CONV, c0:c0 + N_CHUNK] * rot
            for j in range(1, SSM_CONV):
                rot = pltpu.roll(rot, 1, axis=1)
                prev = pltpu.roll(prev, 1, axis=1)
                above = jnp.concatenate([prev, rot[:-1]], axis=0)
                k = SSM_CONV - 1 - j
                acc = acc + cw_ref[k:k + 1, c0:c0 + N_CHUNK] * jnp.where(sub >= j, rot, above)
            r0 = g0 * SUBLANES
            o_ref[r0:r0 + CONV_GROUPS * SUBLANES, n0:n0 + N_CHUNK] = (
                _silu_of_half(acc).reshape(CONV_GROUPS * SUBLANES, N_CHUNK).astype(BF16))
    dt_ref[...] = _dot(h, wdt_ref[...])


def _a_in_proj(x2, g, w_main, w_dt, conv_w, conv_b, seq):
    t, d = x2.shape
    n_main = w_main.shape[1]
    return pl.pallas_call(
        functools.partial(_a_in_kernel, tiles_per_seq=seq // ROW_TILE),
        grid=(t // ROW_TILE,),
        in_specs=[
            pl.BlockSpec((ROW_TILE, d), lambda i: (i, 0)),
            pl.BlockSpec((1, d), lambda i: (0, 0)),
            pl.BlockSpec((d, n_main), lambda i: (0, 0)),
            pl.BlockSpec((d, LANES), lambda i: (0, 0)),
            pl.BlockSpec((SSM_CONV, SSM_CONV_DIM), lambda i: (0, 0)),
            pl.BlockSpec((1, SSM_CONV_DIM), lambda i: (0, 0)),
        ],
        out_specs=[
            pl.BlockSpec((ROW_TILE, n_main), lambda i: (i, 0)),
            pl.BlockSpec((ROW_TILE, LANES), lambda i: (i, 0)),
        ],
        out_shape=[
            jax.ShapeDtypeStruct((t, n_main), BF16),
            jax.ShapeDtypeStruct((t, LANES), F32),
        ],
        scratch_shapes=[pltpu.VMEM((SUBLANES, SSM_CONV_DIM), F32)],
        compiler_params=_cparams(("arbitrary",)),
        name="mamba_in_proj",
    )(x2, g, w_main, w_dt, conv_w, conv_b)


def _ssd_kernel(zx_ref, dtc_ref, dtr_ref, hp_row_ref, hp_col_ref, gng_ref, o_ref, state_ref):
    q = zx_ref.shape[0]
    g_n, r_n, p_n, n_n = SSM_GROUPS, SSM_HPG, SSM_HEAD_DIM, SSM_STATE
    di = SSM_D_INNER
    hn = SSM_HEADS
    gw = r_n * p_n

    @pl.when(pl.program_id(1) == 0)
    def _():
        state_ref[...] = jnp.zeros_like(state_ref)

    dtb_row, alog_row, dskip_row = hp_row_ref[0:1, :], hp_row_ref[1:2, :], hp_row_ref[2:3, :]
    dtb_col, alog_col = hp_col_ref[:, 0:1], hp_col_ref[:, 1:2]
    dt_c = _softplus(dtc_ref[:, 0:hn] + dtb_row)
    adt_c = dt_c * (-jnp.exp(alog_row))
    dt_r = _softplus(dtr_ref[...] + dtb_col)
    adt_r = dt_r * (-jnp.exp(alog_col))
    ri = lax.broadcasted_iota(jnp.int32, (q, q), 0)
    ci = lax.broadcasted_iota(jnp.int32, (q, q), 1)
    causal = ri >= ci
    tril = jnp.where(causal, 1.0, 0.0).astype(BF16)
    triu = jnp.where(ri <= ci, 1.0, 0.0).astype(BF16)
    acs_c = sum(_dot(tril, part) for part in _split3(adt_c))
    acs_r = sum(_dot(part, triu) for part in _split3(adt_r))
    tot_row = acs_c[q - 1:q, :]
    eacs_c = jnp.exp(acs_c)
    toend_c = jnp.exp(tot_row - acs_c) * dt_c
    etot_row = jnp.exp(tot_row)
    acs2_c = acs_c * LOG2E
    acs2_r = acs_r * LOG2E

    lane_head = lax.broadcasted_iota(jnp.int32, (hn, gw), 1) // p_n
    row_head = lax.broadcasted_iota(jnp.int32, (hn, gw), 0)
    dt16 = dt_c.astype(BF16)
    toend16 = toend_c.astype(BF16)
    eacs_hi = _trunc_bf16(eacs_c)
    eacs_lo = (eacs_c - eacs_hi).astype(BF16)
    eacs_hi = eacs_hi.astype(BF16)
    lane_head_row = lane_head[0:1, :]
    head_mask16 = [jnp.where(lane_head_row == r, 1.0, 0.0).astype(BF16) for r in range(r_n)]

    def per_head_row(row, g):
        out = jnp.broadcast_to(row[:, g * r_n:g * r_n + 1], (1, gw))
        for r in range(1, r_n):
            hval = jnp.broadcast_to(row[:, g * r_n + r:g * r_n + r + 1], (1, gw))
            out = jnp.where(lane_head_row == r, hval, out)
        return out

    for g in range(g_n):
        spread = jnp.where(row_head == lane_head + g * r_n, 1.0, 0.0).astype(BF16)
        xs16 = zx_ref[:, di + g * gw:di + (g + 1) * gw]
        bm16 = zx_ref[:, 2 * di + g * n_n:2 * di + (g + 1) * n_n]
        cm16 = zx_ref[:, 2 * di + SSM_GN + g * n_n:2 * di + SSM_GN + (g + 1) * n_n]
        xs = xs16.astype(F32)
        cb = _dot_nt(cm16, bm16)
        xdt16 = (xs * _dot(dt16, spread)).astype(BF16)
        st = state_ref[g]
        y = _dot(cm16, st.astype(BF16)) * (_dot(eacs_hi, spread) + _dot(eacs_lo, spread))
        for r in range(r_n):
            h = g * r_n + r
            seg2 = jnp.broadcast_to(acs2_c[:, h:h + 1], (q, q)) - acs2_r[h:h + 1, :]
            lmat = (cb * jnp.where(causal, jnp.exp2(seg2), 0.0)).astype(BF16)
            y = y + _dot(lmat, xdt16 * head_mask16[r])
        y = y + per_head_row(dskip_row, g) * xs
        xend16 = (xs * _dot(toend16, spread)).astype(BF16)
        state_ref[g] = st * per_head_row(etot_row, g) + _dot_tn(bm16, xend16)
        u = y * zx_ref[:, g * gw:(g + 1) * gw].astype(F32)
        u = u * lax.rsqrt(jnp.mean(u * u, axis=-1, keepdims=True) + EPS)
        o_ref[:, g * gw:(g + 1) * gw] = (u * gng_ref[:, g * gw:(g + 1) * gw]).astype(BF16)


def _ssd(zx, dt_col, dt_row, hp_row, hp_col, gate_g, batch, seq):
    q = SSD_Q
    nc = seq // q
    n_main = zx.shape[1]
    hn = SSM_HEADS
    return pl.pallas_call(
        _ssd_kernel,
        grid=(batch, nc),
        in_specs=[
            pl.BlockSpec((q, n_main), lambda b, c: (b * nc + c, 0)),
            pl.BlockSpec((q, LANES), lambda b, c: (b * nc + c, 0)),
            pl.BlockSpec((None, hn, q), lambda b, c: (b, 0, c)),
            pl.BlockSpec((SUBLANES, hn), lambda b, c: (0, 0)),
            pl.BlockSpec((hn, SUBLANES), lambda b, c: (0, 0)),
            pl.BlockSpec((1, SSM_D_INNER), lambda b, c: (0, 0)),
        ],
        out_specs=pl.BlockSpec((q, SSM_D_INNER), lambda b, c: (b * nc + c, 0)),
        out_shape=jax.ShapeDtypeStruct((batch * seq, SSM_D_INNER), BF16),
        scratch_shapes=[
            pltpu.VMEM((SSM_GROUPS, SSM_STATE, SSM_HPG * SSM_HEAD_DIM), F32),
        ],
        compiler_params=_cparams(("parallel", "arbitrary")),
        name="mamba_ssd",
    )(zx, dt_col, dt_row, hp_row, hp_col, gate_g)


def _out_proj_kernel(y_ref, w_ref, x_ref, o_ref):
    o_ref[...] = x_ref[...] + _dot(y_ref[...], w_ref[...])


def _out_proj(y, w, x2):
    t, k = y.shape
    d = w.shape[1]
    return pl.pallas_call(
        _out_proj_kernel,
        grid=(t // ROW_TILE,),
        in_specs=[
            pl.BlockSpec((ROW_TILE, k), lambda i: (i, 0)),
            pl.BlockSpec((k, d), lambda i: (0, 0)),
            pl.BlockSpec((ROW_TILE, d), lambda i: (i, 0)),
        ],
        out_specs=pl.BlockSpec((ROW_TILE, d), lambda i: (i, 0)),
        out_shape=jax.ShapeDtypeStruct((t, d), F32),
        compiler_params=_cparams(("parallel",)),
        name="mamba_out_proj",
    )(y, w, x2)


def _gated_out_proj_kernel(a_ref, gate_ref, w_ref, x_ref, fg_ref, o_ref, *, final_norm):
    gate = gate_ref[...].astype(F32)
    y = (a_ref[...] * _silu(gate)).astype(BF16)
    out = x_ref[...] + _dot(y, w_ref[...])
    if final_norm:
        out = _rms(out, fg_ref[...])
    o_ref[...] = out


def _gated_out_proj(attn, gate, w, x2, final_g, final_norm):
    t, k = attn.shape
    d = w.shape[1]
    return pl.pallas_call(
        functools.partial(_gated_out_proj_kernel, final_norm=final_norm),
        grid=(t // ROW_TILE,),
        in_specs=[
            pl.BlockSpec((ROW_TILE, k), lambda i: (i, 0)),
            pl.BlockSpec((ROW_TILE, k), lambda i: (i, 0)),
            pl.BlockSpec((k, d), lambda i: (0, 0)),
            pl.BlockSpec((ROW_TILE, d), lambda i: (i, 0)),
            pl.BlockSpec((1, d), lambda i: (0, 0)),
        ],
        out_specs=pl.BlockSpec((ROW_TILE, d), lambda i: (i, 0)),
        out_shape=jax.ShapeDtypeStruct((t, d), F32),
        compiler_params=_cparams(("parallel",)),
        name="attn_out_proj",
    )(attn, gate, w, x2, final_g)


def _kv_kernel(x_ref, g_ref, wk_ref, wv_ref, aug_ref, k_ref, vt_ref):
    h = _rms(x_ref[...], g_ref[...]).astype(BF16)
    hn = DIFF_HEADS
    for n0 in range(0, wk_ref.shape[1], N_CHUNK):
        kc = _dot(h, wk_ref[:, n0:n0 + N_CHUNK])
        for j in range(N_CHUNK // QK_PAD):
            idx = n0 // QK_PAD + j
            k_ref[idx // 2, idx % 2] = (kc[:, j * QK_PAD:(j + 1) * QK_PAD]
                                        + aug_ref[idx // 2]).astype(BF16)
    for n0 in range(0, wv_ref.shape[1], N_CHUNK):
        vc = _dot(h, wv_ref[:, n0:n0 + N_CHUNK])
        for j in range(N_CHUNK // DIFF_V_DIM):
            head = n0 // DIFF_V_DIM + j
            vt_ref[head] = vc[:, j * DIFF_V_DIM:(j + 1) * DIFF_V_DIM].T.astype(BF16)
    del hn


def _kv_proj(x2, g, wk, wv, k_aug, batch, seq):
    d = x2.shape[1]
    nt = seq // ROW_TILE
    hn = DIFF_HEADS
    return pl.pallas_call(
        _kv_kernel,
        grid=(batch, nt),
        in_specs=[
            pl.BlockSpec((ROW_TILE, d), lambda b, i: (b * nt + i, 0)),
            pl.BlockSpec((1, d), lambda b, i: (0, 0)),
            pl.BlockSpec(wk.shape, lambda b, i: (0, 0)),
            pl.BlockSpec(wv.shape, lambda b, i: (0, 0)),
            pl.BlockSpec((hn, ROW_TILE, QK_PAD), lambda b, i: (0, i, 0)),
        ],
        out_specs=[
            pl.BlockSpec((None, hn, 2, ROW_TILE, QK_PAD), lambda b, i: (b, 0, 0, i, 0)),
            pl.BlockSpec((None, hn, DIFF_V_DIM, ROW_TILE), lambda b, i: (b, 0, 0, i)),
        ],
        out_shape=[
            jax.ShapeDtypeStruct((batch, hn, 2, seq, QK_PAD), BF16),
            jax.ShapeDtypeStruct((batch, hn, DIFF_V_DIM, seq), BF16),
        ],
        compiler_params=_cparams(("parallel", "parallel")),
        name="kv_proj",
    )(x2, g, wk, wv, k_aug)


def _b_in_kernel(x_ref, g_ref, wq_ref, wg_ref, aug_ref, q_ref, gate_ref):
    h = _rms(x_ref[...], g_ref[...]).astype(BF16)
    for n0 in range(0, wq_ref.shape[1], N_CHUNK):
        qc = _dot(h, wq_ref[:, n0:n0 + N_CHUNK]) * QK_SCALE
        for j in range(N_CHUNK // QK_PAD):
            idx = n0 // QK_PAD + j
            q_ref[idx // 2, idx % 2] = (qc[:, j * QK_PAD:(j + 1) * QK_PAD] + aug_ref[...]).astype(BF16)
    for n0 in range(0, wg_ref.shape[1], N_CHUNK):
        gate_ref[:, n0:n0 + N_CHUNK] = _dot(h, wg_ref[:, n0:n0 + N_CHUNK]).astype(BF16)


def _b_in_proj(x2, g, wq, wg, q_aug, batch, seq):
    d = x2.shape[1]
    nt = seq // ROW_TILE
    hn = DIFF_HEADS
    return pl.pallas_call(
        _b_in_kernel,
        grid=(batch, nt),
        in_specs=[
            pl.BlockSpec((ROW_TILE, d), lambda b, i: (b * nt + i, 0)),
            pl.BlockSpec((1, d), lambda b, i: (0, 0)),
            pl.BlockSpec(wq.shape, lambda b, i: (0, 0)),
            pl.BlockSpec(wg.shape, lambda b, i: (0, 0)),
            pl.BlockSpec((ROW_TILE, QK_PAD), lambda b, i: (i, 0)),
        ],
        out_specs=[
            pl.BlockSpec((None, hn, 2, ROW_TILE, QK_PAD), lambda b, i: (b, 0, 0, i, 0)),
            pl.BlockSpec((ROW_TILE, DIFF_WIDTH), lambda b, i: (b * nt + i, 0)),
        ],
        out_shape=[
            jax.ShapeDtypeStruct((batch, hn, 2, seq, QK_PAD), BF16),
            jax.ShapeDtypeStruct((batch * seq, DIFF_WIDTH), BF16),
        ],
        compiler_params=_cparams(("parallel", "parallel")),
        name="attn_in_proj",
    )(x2, g, wq, wg, q_aug)


def _attn_kernel(lam_ref, q_ref, k_ref, vt_ref, dbias_ref, subg_ref, o_ref,
                 s_a, s_b, p_a, p_b, *, lambda_init):
    tq, tk = ATT_TQ, ATT_TK
    seq = q_ref.shape[1]
    n_qt = seq // tq
    lf = lam_ref[...]
    lam = (jnp.exp(jnp.sum(lf[0:1, :] * lf[1:2, :], axis=-1, keepdims=True))
           - jnp.exp(jnp.sum(lf[2:3, :] * lf[3:4, :], axis=-1, keepdims=True)) + lambda_init)
    neg = jnp.float32(-1e30)

    def fold(x, op):
        return op(x.reshape(tk // SUBLANES, SUBLANES, tq), axis=0)

    def q_tile(qi, carry):
        q0 = pl.multiple_of(qi * tq, tq)
        qs = [q_ref[i, pl.ds(q0, tq), :] for i in range(2)]

        def stage_a(t, s_buf):
            k0 = pl.multiple_of(t * tk, tk)
            mparts = []
            for i in range(2):
                s = _dot_nt(k_ref[i, pl.ds(k0, tk), :], qs[i])
                s_buf[i] = s
                mparts.append(fold(s, jnp.max))
            return tuple(mparts)

        def stage_b(s_buf, p_buf, mparts, ms, ls, diag):
            new_m, new_l, alphas = [], [], []
            for i in range(2):
                if diag:
                    s = s_buf[i] + dbias_ref[...]
                    mpart = fold(s, jnp.max)
                else:
                    s = s_buf[i]
                    mpart = mparts[i]
                m_new = jnp.maximum(ms[i], jnp.max(mpart, axis=0, keepdims=True))
                alpha = jnp.exp2(ms[i] - m_new)
                p = jnp.exp2(s - m_new)
                new_l.append(alpha * ls[i] + fold(p, jnp.sum))
                p_buf[i] = p.astype(BF16)
                new_m.append(m_new)
                alphas.append(alpha)
            return tuple(new_m), tuple(new_l), tuple(alphas)

        def stage_c(t, p_buf, alphas, accs):
            k0 = pl.multiple_of(jnp.maximum(t, 0) * tk, tk)
            vt = vt_ref[:, pl.ds(k0, tk)]
            return tuple(alphas[i] * accs[i] + _dot(vt, p_buf[i]) for i in range(2))

        def step(j, bufs, st):
            s_cur, s_nxt, p_cur, p_prv = bufs
            mparts, ms, ls, alphas, accs = st
            mparts_next = stage_a(j + 1, s_nxt)
            ms, ls, alphas_new = stage_b(s_cur, p_cur, mparts, ms, ls, False)
            accs = stage_c(j - 1, p_prv, alphas, accs)
            return mparts_next, ms, ls, alphas_new, accs

        def finish(bufs, st):
            s_cur, _, p_cur, p_prv = bufs
            _, ms, ls, alphas, accs = st
            _, ls, alphas_new = stage_b(s_cur, p_cur, None, ms, ls, True)
            accs = stage_c(qi - 1, p_prv, alphas, accs)
            accs = stage_c(qi, p_cur, alphas_new, accs)
            l1 = jnp.sum(ls[0], axis=0, keepdims=True)
            l2 = jnp.sum(ls[1], axis=0, keepdims=True)
            out_t = accs[0] / l1 - lam * (accs[1] / l2)
            out = _rms(out_t.T, subg_ref[...]) * (1.0 - lambda_init)
            o_ref[pl.ds(q0, tq), :] = out

        even = (s_a, s_b, p_a, p_b)
        odd = (s_b, s_a, p_b, p_a)
        p_b[...] = jnp.zeros_like(p_b)
        init = (stage_a(0, s_a),
                tuple(jnp.full((1, tq), neg, F32) for _ in range(2)),
                tuple(jnp.zeros((SUBLANES, tq), F32) for _ in range(2)),
                tuple(jnp.ones((1, tq), F32) for _ in range(2)),
                tuple(jnp.zeros((DIFF_V_DIM, tq), F32) for _ in range(2)))
        def steps(j0, n, st):
            for u in range(n):
                st = step(j0 + u, odd if u % 2 else even, st)
            return st

        st = lax.fori_loop(0, qi // ATT_UNROLL, lambda jj, c: steps(ATT_UNROLL * jj, ATT_UNROLL, c), init)
        base = (qi // ATT_UNROLL) * ATT_UNROLL
        for rem in range(ATT_UNROLL):
            @pl.when(qi % ATT_UNROLL == rem)
            def _(rem=rem):
                finish(odd if rem % 2 else even, steps(base, rem, st))

        return carry

    lax.fori_loop(0, n_qt, q_tile, 0)


def _attention(lam_qk, q, k, vt, dbias, sub_g, lambda_init):
    batch, hn, _, seq, _ = q.shape
    return pl.pallas_call(
        functools.partial(_attn_kernel, lambda_init=lambda_init),
        grid=(batch, hn),
        in_specs=[
            pl.BlockSpec(lam_qk.shape, lambda b, h: (0, 0)),
            pl.BlockSpec((None, None, 2, seq, QK_PAD), lambda b, h: (b, h, 0, 0, 0)),
            pl.BlockSpec((None, None, 2, seq, QK_PAD), lambda b, h: (b, h, 0, 0, 0)),
            pl.BlockSpec((None, None, DIFF_V_DIM, seq), lambda b, h: (b, h, 0, 0)),
            pl.BlockSpec((None, ATT_TK, ATT_TQ), lambda b, h: (h, 0, 0)),
            pl.BlockSpec((1, DIFF_V_DIM), lambda b, h: (0, 0)),
        ],
        out_specs=pl.BlockSpec((None, seq, DIFF_V_DIM), lambda b, h: (b, 0, h)),
        out_shape=jax.ShapeDtypeStruct((batch, seq, DIFF_WIDTH), F32),
        scratch_shapes=[
            pltpu.VMEM((2, ATT_TK, ATT_TQ), F32),
            pltpu.VMEM((2, ATT_TK, ATT_TQ), F32),
            pltpu.VMEM((2, ATT_TK, ATT_TQ), BF16),
            pltpu.VMEM((2, ATT_TK, ATT_TQ), BF16),
        ],
        compiler_params=_cparams(("parallel", "parallel")),
        name="diff_attention",
    )(lam_qk, q, k, vt, dbias, sub_g)


def _np_split3(a):
    a = a.astype(np.float32)
    a1 = a.astype(ml_dtypes.bfloat16).astype(np.float32)
    r1 = a - a1
    a2 = r1.astype(ml_dtypes.bfloat16).astype(np.float32)
    a3 = (r1 - a2).astype(ml_dtypes.bfloat16).astype(np.float32)
    return a1, a2, a3


def _alibi_tables(seq):
    hn = DIFF_HEADS
    slopes = np.float32(2.0) ** (-np.float32(ALIBI_MAX_EXP) * np.arange(1, hn + 1, dtype=np.float32)
                                 / np.float32(hn))
    slopes = (slopes.astype(np.float64) * LOG2E).astype(np.float32)
    pos = np.arange(seq, dtype=np.float32)
    kp = _np_split3(slopes[:, None] * pos[None, :])
    sl = _np_split3(np.broadcast_to(slopes[:, None], (hn, seq)))
    pos_hi = np.floor(pos / 256.0) * 256.0
    pos_lo = pos - pos_hi
    ones = np.ones((seq,), np.float32)
    k_cols = list(kp)
    q_cols = [ones, ones, ones]
    for t in sl:
        k_cols += [-t, -t]
        q_cols += [pos_hi, pos_lo]
    lane_pad = (DIFF_HEAD_DIM, QK_PAD - DIFF_HEAD_DIM - len(k_cols))
    k_aug = jnp.pad(jnp.asarray(np.stack(k_cols, axis=-1)), ((0, 0), (0, 0), lane_pad))
    q_aug = jnp.pad(jnp.asarray(np.stack(q_cols, axis=-1)), ((0, 0), lane_pad))
    kk = np.arange(ATT_TK)[:, None]
    qq = np.arange(ATT_TQ)[None, :]
    allowed = (kk // CHUNK) <= (qq // CHUNK)
    fix = np.where(kk > qq, -2.0 * (kk - qq), 0.0).astype(np.float32)
    dbias = np.where(allowed[None], slopes[:, None, None] * fix[None], -np.inf).astype(np.float32)
    return k_aug, q_aug, jnp.asarray(dbias)


def _pad_heads(w, n_blocks):
    d = w.shape[0]
    w = w.reshape(d, n_blocks, DIFF_HEAD_DIM)
    w = jnp.concatenate([w, jnp.zeros_like(w)], axis=-1)
    return w.reshape(d, n_blocks * QK_PAD).astype(BF16)


def kernel(x, a_norm_g, a_w_in, a_conv_w, a_conv_b, a_dt_bias, a_a_log, a_d_skip, a_gate_norm_g,
           a_w_out, kv_norm_g, w_kv, b_norm_g, b_w_in, b_lambda, b_sub_g, b_w_out, final_norm_g):
    batch, seq, d = x.shape
    t = batch * seq
    hn = SSM_HEADS
    x2 = x.reshape(t, d)

    for i in range(N_A_LAYERS):
        w_in = a_w_in[i]
        n_main = SSM_D_INNER + SSM_CONV_DIM
        w_main = jnp.concatenate([0.5 * w_in[:, :SSM_D_INNER], w_in[:, SSM_D_INNER:n_main]], axis=1).astype(BF16)
        w_dt = jnp.pad(w_in[:, n_main:], ((0, 0), (0, LANES - hn))).astype(BF16)
        zx, dt_raw = _a_in_proj(x2, a_norm_g[i][None], w_main, w_dt, 0.5 * a_conv_w[i],
                                0.5 * a_conv_b[i][None], seq)
        dt_row = dt_raw[:, :hn].reshape(batch, seq, hn).transpose(0, 2, 1)
        hp_row = jnp.zeros((SUBLANES, hn), F32).at[0].set(a_dt_bias[i]).at[1].set(a_a_log[i]).at[2].set(a_d_skip[i])
        hp_col = hp_row.T
        y = _ssd(zx, dt_raw, dt_row, hp_row, hp_col, a_gate_norm_g[i][None], batch, seq)
        x2 = _out_proj(y, a_w_out[i].astype(BF16), x2)

    k_aug, q_aug, dbias = _alibi_tables(seq)
    n_k = DIFF_HEADS * 2 * DIFF_HEAD_DIM
    wk = _pad_heads(w_kv[:, :n_k], DIFF_HEADS * 2)
    wv = w_kv[:, n_k:].astype(BF16)
    k, vt = _kv_proj(x2, kv_norm_g[None], wk, wv, k_aug, batch, seq)

    for j in range(N_B_LAYERS):
        layer = N_A_LAYERS + j
        lambda_init = 0.8 - 0.6 * math.exp(-0.3 * layer)
        wq = _pad_heads(b_w_in[j][:, :DIFF_WIDTH], DIFF_HEADS * 2)
        wg = b_w_in[j][:, DIFF_WIDTH:].astype(BF16)
        q, gate = _b_in_proj(x2, b_norm_g[j][None], wq, wg, q_aug, batch, seq)
        attn = _attention(b_lambda[j], q, k, vt, dbias, b_sub_g[j][None], lambda_init)
        x2 = _gated_out_proj(attn.reshape(t, DIFF_WIDTH), gate, b_w_out[j].astype(BF16), x2,
                             final_norm_g[None], final_norm=(j == N_B_LAYERS - 1))
    return x2.reshape(batch, seq, d)
```

```python
import functools
import math

import jax
import jax.numpy as jnp
import ml_dtypes
import numpy as np
from jax import lax
from jax.experimental import pallas as pl
from jax.experimental.pallas import tpu as pltpu

F32 = jnp.float32
BF16 = jnp.bfloat16

D_MODEL = 1024
DEPTH = 4
CHUNK = 64
N_B_LAYERS = DEPTH // 2
N_A_LAYERS = DEPTH - N_B_LAYERS
SSM_D_INNER = 2 * D_MODEL
SSM_HEAD_DIM = 64
SSM_HEADS = SSM_D_INNER // SSM_HEAD_DIM
SSM_GROUPS = 8
SSM_HPG = SSM_HEADS // SSM_GROUPS
SSM_STATE = 128
SSM_CONV = 4
SSM_GN = SSM_GROUPS * SSM_STATE
SSM_CONV_DIM = SSM_D_INNER + 2 * SSM_GN
DIFF_HEAD_DIM = 64
DIFF_V_DIM = 2 * DIFF_HEAD_DIM
DIFF_HEADS = D_MODEL // DIFF_V_DIM
DIFF_WIDTH = DIFF_HEADS * DIFF_V_DIM
ALIBI_MAX_EXP = 8.0
EPS = 1e-5
LOG2E = math.log2(math.e)
QK_SCALE = LOG2E / math.sqrt(DIFF_HEAD_DIM)

LANES = 128
SUBLANES = 8
VMEM_LIMIT = 56 * 1024 * 1024

ROW_TILE = 512
SSD_Q = 256
ATT_TQ = 256
ATT_TK = 256
ATT_UNROLL = 4
ATT_HEADS = 2
ATT_ROWS = 64
N_CHUNK = 256
CONV_GROUPS = 8
QK_PAD = 2 * DIFF_HEAD_DIM


def _cparams(sem):
    return pltpu.CompilerParams(dimension_semantics=sem, vmem_limit_bytes=VMEM_LIMIT)


def _rms(x, g):
    return x * lax.rsqrt(jnp.mean(x * x, axis=-1, keepdims=True) + EPS) * g


def _silu_of_half(h):
    return h * jnp.tanh(h) + h


def _silu(x):
    return _silu_of_half(0.5 * x)


def _softplus(x):
    return jnp.maximum(x, 0.0) + jnp.log1p(jnp.exp(-jnp.abs(x)))


def _trunc_bf16(a):
    bits = lax.bitcast_convert_type(a, jnp.int32) & jnp.int32(-65536)
    return lax.bitcast_convert_type(bits, F32)


def _split3(a):
    a1 = _trunc_bf16(a)
    r1 = a - a1
    a2 = _trunc_bf16(r1)
    return a1.astype(BF16), a2.astype(BF16), (r1 - a2).astype(BF16)


def _dot(a, b):
    return jnp.dot(a, b, preferred_element_type=F32)


def _dot_nt(a, b):
    return lax.dot_general(a, b, (((1,), (1,)), ((), ())), preferred_element_type=F32)


def _dot_tn(a, b):
    return lax.dot_general(a, b, (((0,), (0,)), ((), ())), preferred_element_type=F32)


def _a_in_kernel(x_ref, g_ref, w_ref, wdt_ref, cw_ref, cb_ref, o_ref, dt_ref, tail_ref,
                 *, tiles_per_seq):
    rows = x_ref.shape[0]
    di = SSM_D_INNER
    n_main = o_ref.shape[-1]

    @pl.when(pl.program_id(0) % tiles_per_seq == 0)
    def _():
        tail_ref[...] = jnp.zeros_like(tail_ref)

    h = _rms(x_ref[...], g_ref[...]).astype(BF16)
    sub = lax.broadcasted_iota(jnp.int32, (1, SUBLANES, N_CHUNK), 1)
    z_chunks = list(range(0, di, N_CHUNK))
    conv_chunks = list(range(di, n_main, N_CHUNK))
    order = []
    while z_chunks or conv_chunks:
        order += conv_chunks[:2] + z_chunks[:1]
        conv_chunks, z_chunks = conv_chunks[2:], z_chunks[1:]
    nxt = _dot(h, w_ref[:, order[0]:order[0] + N_CHUNK])
    for idx, n0 in enumerate(order):
        a = nxt
        if idx + 1 < len(order):
            nxt = _dot(h, w_ref[:, order[idx + 1]:order[idx + 1] + N_CHUNK])
        if n0 < di:
            o_ref[:, n0:n0 + N_CHUNK] = _silu_of_half(a).astype(BF16)
            continue
        c0 = n0 - di
        tail = tail_ref[:, c0:c0 + N_CHUNK][None]
        tail_ref[:, c0:c0 + N_CHUNK] = a[rows - SUBLANES:rows, :]
        a3 = a.reshape(rows // SUBLANES, SUBLANES, N_CHUNK)
        for g0 in range(0, rows // SUBLANES, CONV_GROUPS):
            rot = a3[g0:g0 + CONV_GROUPS]
            prev = tail if g0 == 0 else a3[g0 - 1:g0]
            acc = cb_ref[:, c0:c0 + N_CHUNK] + cw_ref[SSM_CONV - 1:SSM_CONV, c0:c0 + N_CHUNK] * rot
            for j in range(1, SSM_CONV):
                rot = pltpu.roll(rot, 1, axis=1)
                prev = pltpu.roll(prev, 1, axis=1)
                above = jnp.concatenate([prev, rot[:-1]], axis=0)
                k = SSM_CONV - 1 - j
                acc = acc + cw_ref[k:k + 1, c0:c0 + N_CHUNK] * jnp.where(sub >= j, rot, above)
            r0 = g0 * SUBLANES
            o_ref[r0:r0 + CONV_GROUPS * SUBLANES, n0:n0 + N_CHUNK] = (
                _silu_of_half(acc).reshape(CONV_GROUPS * SUBLANES, N_CHUNK).astype(BF16))
    dt_ref[...] = _dot(h, wdt_ref[...])


def _a_in_proj(x2, g, w_main, w_dt, conv_w, conv_b, seq):
    t, d = x2.shape
    n_main = w_main.shape[1]
    return pl.pallas_call(
        functools.partial(_a_in_kernel, tiles_per_seq=seq // ROW_TILE),
        grid=(t // ROW_TILE,),
        in_specs=[
            pl.BlockSpec((ROW_TILE, d), lambda i: (i, 0)),
            pl.BlockSpec((1, d), lambda i: (0, 0)),
            pl.BlockSpec((d, n_main), lambda i: (0, 0)),
            pl.BlockSpec((d, LANES), lambda i: (0, 0)),
            pl.BlockSpec((SSM_CONV, SSM_CONV_DIM), lambda i: (0, 0)),
            pl.BlockSpec((1, SSM_CONV_DIM), lambda i: (0, 0)),
        ],
        out_specs=[
            pl.BlockSpec((ROW_TILE, n_main), lambda i: (i, 0)),
            pl.BlockSpec((ROW_TILE, LANES), lambda i: (i, 0)),
        ],
        out_shape=[
            jax.ShapeDtypeStruct((t, n_main), BF16),
            jax.ShapeDtypeStruct((t, LANES), F32),
        ],
        scratch_shapes=[pltpu.VMEM((SUBLANES, SSM_CONV_DIM), F32)],
        compiler_params=_cparams(("arbitrary",)),
        name="mamba_in_proj",
    )(x2, g, w_main, w_dt, conv_w, conv_b)


def _ssd_kernel(zx_ref, dtc_ref, dtr_ref, hp_row_ref, hp_col_ref, gng_ref, o_ref, state_ref):
    q = zx_ref.shape[0]
    g_n, r_n, p_n, n_n = SSM_GROUPS, SSM_HPG, SSM_HEAD_DIM, SSM_STATE
    di = SSM_D_INNER
    hn = SSM_HEADS
    gw = r_n * p_n

    @pl.when(pl.program_id(1) == 0)
    def _():
        state_ref[...] = jnp.zeros_like(state_ref)

    dtb_row, alog_row, dskip_row = hp_row_ref[0:1, :], hp_row_ref[1:2, :], hp_row_ref[2:3, :]
    dtb_col, alog_col = hp_col_ref[:, 0:1], hp_col_ref[:, 1:2]
    dt_c = _softplus(dtc_ref[:, 0:hn] + dtb_row)
    adt_c = dt_c * (-jnp.exp(alog_row))
    dt_r = _softplus(dtr_ref[...] + dtb_col)
    adt_r = dt_r * (-jnp.exp(alog_col))
    ri = lax.broadcasted_iota(jnp.int32, (q, q), 0)
    ci = lax.broadcasted_iota(jnp.int32, (q, q), 1)
    causal = ri >= ci
    tril = jnp.where(causal, 1.0, 0.0).astype(BF16)
    triu = jnp.where(ri <= ci, 1.0, 0.0).astype(BF16)
    acs_c = sum(_dot(tril, part) for part in _split3(adt_c))
    acs_r = sum(_dot(part, triu) for part in _split3(adt_r))
    tot_row = acs_c[q - 1:q, :]
    eacs_c = jnp.exp(acs_c)
    toend_c = jnp.exp(tot_row - acs_c) * dt_c
    etot_row = jnp.exp(tot_row)
    acs2_c = acs_c * LOG2E
    acs2_r = acs_r * LOG2E

    lane_head = lax.broadcasted_iota(jnp.int32, (hn, gw), 1) // p_n
    row_head = lax.broadcasted_iota(jnp.int32, (hn, gw), 0)
    dt16 = dt_c.astype(BF16)
    toend16 = toend_c.astype(BF16)
    eacs_hi = _trunc_bf16(eacs_c)
    eacs_lo = (eacs_c - eacs_hi).astype(BF16)
    eacs_hi = eacs_hi.astype(BF16)
    lane_head_row = lane_head[0:1, :]
    head_mask16 = [jnp.where(lane_head_row == r, 1.0, 0.0).astype(BF16) for r in range(r_n)]

    def per_head_row(row, g):
        out = jnp.broadcast_to(row[:, g * r_n:g * r_n + 1], (1, gw))
        for r in range(1, r_n):
            hval = jnp.broadcast_to(row[:, g * r_n + r:g * r_n + r + 1], (1, gw))
            out = jnp.where(lane_head_row == r, hval, out)
        return out

    for g in range(g_n):
        spread = jnp.where(row_head == lane_head + g * r_n, 1.0, 0.0).astype(BF16)
        xs16 = zx_ref[:, di + g * gw:di + (g + 1) * gw]
        bm16 = zx_ref[:, 2 * di + g * n_n:2 * di + (g + 1) * n_n]
        cm16 = zx_ref[:, 2 * di + SSM_GN + g * n_n:2 * di + SSM_GN + (g + 1) * n_n]
        xs = xs16.astype(F32)
        cb = _dot_nt(cm16, bm16)
        xdt16 = (xs * _dot(dt16, spread)).astype(BF16)
        st = state_ref[g]
        y = _dot(cm16, st.astype(BF16)) * (_dot(eacs_hi, spread) + _dot(eacs_lo, spread))
        for r in range(r_n):
            h = g * r_n + r
            seg2 = jnp.broadcast_to(acs2_c[:, h:h + 1], (q, q)) - acs2_r[h:h + 1, :]
            lmat = (cb * jnp.where(causal, jnp.exp2(seg2), 0.0)).astype(BF16)
            y = y + _dot(lmat, xdt16 * head_mask16[r])
        y = y + per_head_row(dskip_row, g) * xs
        xend16 = (xs * _dot(toend16, spread)).astype(BF16)
        state_ref[g] = st * per_head_row(etot_row, g) + _dot_tn(bm16, xend16)
        u = y * zx_ref[:, g * gw:(g + 1) * gw].astype(F32)
        u = u * lax.rsqrt(jnp.mean(u * u, axis=-1, keepdims=True) + EPS)
        o_ref[:, g * gw:(g + 1) * gw] = (u * gng_ref[:, g * gw:(g + 1) * gw]).astype(BF16)


def _ssd(zx, dt_col, dt_row, hp_row, hp_col, gate_g, batch, seq):
    q = SSD_Q
    nc = seq // q
    n_main = zx.shape[1]
    hn = SSM_HEADS
    return pl.pallas_call(
        _ssd_kernel,
        grid=(batch, nc),
        in_specs=[
            pl.BlockSpec((q, n_main), lambda b, c: (b * nc + c, 0)),
            pl.BlockSpec((q, LANES), lambda b, c: (b * nc + c, 0)),
            pl.BlockSpec((None, hn, q), lambda b, c: (b, 0, c)),
            pl.BlockSpec((SUBLANES, hn), lambda b, c: (0, 0)),
            pl.BlockSpec((hn, SUBLANES), lambda b, c: (0, 0)),
            pl.BlockSpec((1, SSM_D_INNER), lambda b, c: (0, 0)),
        ],
        out_specs=pl.BlockSpec((q, SSM_D_INNER), lambda b, c: (b * nc + c, 0)),
        out_shape=jax.ShapeDtypeStruct((batch * seq, SSM_D_INNER), BF16),
        scratch_shapes=[
            pltpu.VMEM((SSM_GROUPS, SSM_STATE, SSM_HPG * SSM_HEAD_DIM), F32),
        ],
        compiler_params=_cparams(("parallel", "arbitrary")),
        name="mamba_ssd",
    )(zx, dt_col, dt_row, hp_row, hp_col, gate_g)


def _out_proj_kernel(y_ref, w_ref, x_ref, o_ref):
    o_ref[...] = x_ref[...] + _dot(y_ref[...], w_ref[...])


def _out_proj(y, w, x2):
    t, k = y.shape
    d = w.shape[1]
    return pl.pallas_call(
        _out_proj_kernel,
        grid=(t // ROW_TILE,),
        in_specs=[
            pl.BlockSpec((ROW_TILE, k), lambda i: (i, 0)),
            pl.BlockSpec((k, d), lambda i: (0, 0)),
            pl.BlockSpec((ROW_TILE, d), lambda i: (i, 0)),
        ],
        out_specs=pl.BlockSpec((ROW_TILE, d), lambda i: (i, 0)),
        out_shape=jax.ShapeDtypeStruct((t, d), F32),
        compiler_params=_cparams(("parallel",)),
        name="mamba_out_proj",
    )(y, w, x2)


def _gated_out_proj_kernel(a_ref, gate_ref, w_ref, x_ref, fg_ref, o_ref, *, final_norm):
    gate = gate_ref[...].astype(F32)
    y = (a_ref[...] * _silu(gate)).astype(BF16)
    out = x_ref[...] + _dot(y, w_ref[...])
    if final_norm:
        out = _rms(out, fg_ref[...])
    o_ref[...] = out


def _gated_out_proj(attn, gate, w, x2, final_g, final_norm):
    t, k = attn.shape
    d = w.shape[1]
    return pl.pallas_call(
        functools.partial(_gated_out_proj_kernel, final_norm=final_norm),
        grid=(t // ROW_TILE,),
        in_specs=[
            pl.BlockSpec((ROW_TILE, k), lambda i: (i, 0)),
            pl.BlockSpec((ROW_TILE, k), lambda i: (i, 0)),
            pl.BlockSpec((k, d), lambda i: (0, 0)),
            pl.BlockSpec((ROW_TILE, d), lambda i: (i, 0)),
            pl.BlockSpec((1, d), lambda i: (0, 0)),
        ],
        out_specs=pl.BlockSpec((ROW_TILE, d), lambda i: (i, 0)),
        out_shape=jax.ShapeDtypeStruct((t, d), F32),
        compiler_params=_cparams(("parallel",)),
        name="attn_out_proj",
    )(attn, gate, w, x2, final_g)


def _kv_kernel(x_ref, g_ref, wk_ref, wv_ref, aug_ref, k_ref, vt_ref):
    h = _rms(x_ref[...], g_ref[...]).astype(BF16)
    hn = DIFF_HEADS
    for n0 in range(0, wk_ref.shape[1], N_CHUNK):
        kc = _dot(h, wk_ref[:, n0:n0 + N_CHUNK])
        for j in range(N_CHUNK // QK_PAD):
            idx = n0 // QK_PAD + j
            k_ref[idx // 2, idx % 2] = (kc[:, j * QK_PAD:(j + 1) * QK_PAD]
                                        + aug_ref[idx // 2]).astype(BF16)
    for n0 in range(0, wv_ref.shape[1], N_CHUNK):
        vc = _dot(h, wv_ref[:, n0:n0 + N_CHUNK])
        for j in range(N_CHUNK // DIFF_V_DIM):
            head = n0 // DIFF_V_DIM + j
            vt_ref[head] = vc[:, j * DIFF_V_DIM:(j + 1) * DIFF_V_DIM].T.astype(BF16)
    del hn


def _kv_proj(x2, g, wk, wv, k_aug, batch, seq):
    d = x2.shape[1]
    nt = seq // ROW_TILE
    hn = DIFF_HEADS
    return pl.pallas_call(
        _kv_kernel,
        grid=(batch, nt),
        in_specs=[
            pl.BlockSpec((ROW_TILE, d), lambda b, i: (b * nt + i, 0)),
            pl.BlockSpec((1, d), lambda b, i: (0, 0)),
            pl.BlockSpec(wk.shape, lambda b, i: (0, 0)),
            pl.BlockSpec(wv.shape, lambda b, i: (0, 0)),
            pl.BlockSpec((hn, ROW_TILE, QK_PAD), lambda b, i: (0, i, 0)),
        ],
        out_specs=[
            pl.BlockSpec((None, hn, 2, ROW_TILE, QK_PAD), lambda b, i: (b, 0, 0, i, 0)),
            pl.BlockSpec((None, hn, DIFF_V_DIM, ROW_TILE), lambda b, i: (b, 0, 0, i)),
        ],
        out_shape=[
            jax.ShapeDtypeStruct((batch, hn, 2, seq, QK_PAD), BF16),
            jax.ShapeDtypeStruct((batch, hn, DIFF_V_DIM, seq), BF16),
        ],
        compiler_params=_cparams(("parallel", "parallel")),
        name="kv_proj",
    )(x2, g, wk, wv, k_aug)


def _b_in_kernel(x_ref, g_ref, wq_ref, wg_ref, aug_ref, q_ref, gate_ref):
    h = _rms(x_ref[...], g_ref[...]).astype(BF16)
    for n0 in range(0, wq_ref.shape[1], N_CHUNK):
        qc = _dot(h, wq_ref[:, n0:n0 + N_CHUNK]) * QK_SCALE
        for j in range(N_CHUNK // QK_PAD):
            idx = n0 // QK_PAD + j
            q_ref[idx // 2, idx % 2] = (qc[:, j * QK_PAD:(j + 1) * QK_PAD] + aug_ref[...]).astype(BF16)
    for n0 in range(0, wg_ref.shape[1], N_CHUNK):
        gate_ref[:, n0:n0 + N_CHUNK] = _dot(h, wg_ref[:, n0:n0 + N_CHUNK]).astype(BF16)


def _b_in_proj(x2, g, wq, wg, q_aug, batch, seq):
    d = x2.shape[1]
    nt = seq // ROW_TILE
    hn = DIFF_HEADS
    return pl.pallas_call(
        _b_in_kernel,
        grid=(batch, nt),
        in_specs=[
            pl.BlockSpec((ROW_TILE, d), lambda b, i: (b * nt + i, 0)),
            pl.BlockSpec((1, d), lambda b, i: (0, 0)),
            pl.BlockSpec(wq.shape, lambda b, i: (0, 0)),
            pl.BlockSpec(wg.shape, lambda b, i: (0, 0)),
            pl.BlockSpec((ROW_TILE, QK_PAD), lambda b, i: (i, 0)),
        ],
        out_specs=[
            pl.BlockSpec((None, hn, 2, ROW_TILE, QK_PAD), lambda b, i: (b, 0, 0, i, 0)),
            pl.BlockSpec((ROW_TILE, DIFF_WIDTH), lambda b, i: (b * nt + i, 0)),
        ],
        out_shape=[
            jax.ShapeDtypeStruct((batch, hn, 2, seq, QK_PAD), BF16),
            jax.ShapeDtypeStruct((batch * seq, DIFF_WIDTH), BF16),
        ],
        compiler_params=_cparams(("parallel", "parallel")),
        name="attn_in_proj",
    )(x2, g, wq, wg, q_aug)


def _attn_kernel(lam_ref, q_ref, k_ref, vt_ref, dbias_ref, subg_ref, o_ref,
                 s_a, s_b, p_a, p_b, *, lambda_init):
    tq, tk = ATT_TQ, ATT_TK
    seq = q_ref.shape[2]
    streams = [(hh, i) for hh in range(ATT_HEADS) for i in range(2)]
    n_qt = seq // tq
    lf = lam_ref[...]
    lam = (jnp.exp(jnp.sum(lf[0:1, :] * lf[1:2, :], axis=-1, keepdims=True))
           - jnp.exp(jnp.sum(lf[2:3, :] * lf[3:4, :], axis=-1, keepdims=True)) + lambda_init)
    neg = jnp.float32(-1e30)

    def fold(x, op):
        return op(x.reshape(x.shape[0] // SUBLANES, SUBLANES, tq), axis=0)

    def q_tile(qi, carry):
        q0 = pl.multiple_of(qi * tq, tq)
        qs = [q_ref[hh, i, pl.ds(q0, tq), :] for hh, i in streams]

        def stage_a(t, s_buf):
            k0 = pl.multiple_of(t * tk, tk)
            mparts = []
            for n, (hh, i) in enumerate(streams):
                s = _dot_nt(k_ref[hh, i, pl.ds(k0, tk), :], qs[n])
                s_buf[n] = s
                mparts.append(fold(s, jnp.max))
            return tuple(mparts)

        def stage_b(s_buf, p_buf, mparts, ms, ls, diag):
            new_m, new_l, alphas = [], [], []
            for n, (hh, _) in enumerate(streams):
                def scores(r0):
                    s = s_buf[n, r0:r0 + ATT_ROWS, :]
                    return s + dbias_ref[hh, r0:r0 + ATT_ROWS, :] if diag else s

                if diag:
                    mpart = fold(scores(0), jnp.max)
                    for r0 in range(ATT_ROWS, tk, ATT_ROWS):
                        mpart = jnp.maximum(mpart, fold(scores(r0), jnp.max))
                else:
                    mpart = mparts[n]
                m_new = jnp.maximum(ms[n], jnp.max(mpart, axis=0, keepdims=True))
                alpha = jnp.exp2(ms[n] - m_new)
                lsum = alpha * ls[n]
                for r0 in range(0, tk, ATT_ROWS):
                    p = jnp.exp2(scores(r0) - m_new)
                    lsum = lsum + fold(p, jnp.sum)
                    p_buf[n, r0:r0 + ATT_ROWS, :] = p.astype(BF16)
                new_l.append(lsum)
                new_m.append(m_new)
                alphas.append(alpha)
            return tuple(new_m), tuple(new_l), tuple(alphas)

        def stage_c(t, p_buf, alphas, accs):
            k0 = pl.multiple_of(jnp.maximum(t, 0) * tk, tk)
            vts = [vt_ref[hh, :, pl.ds(k0, tk)] for hh in range(ATT_HEADS)]
            return tuple(alphas[n] * accs[n] + _dot(vts[hh], p_buf[n])
                         for n, (hh, _) in enumerate(streams))

        def step(j, bufs, st):
            s_cur, s_nxt, p_cur, p_prv = bufs
            mparts, ms, ls, alphas, accs = st
            mparts_next = stage_a(j + 1, s_nxt)
            ms, ls, alphas_new = stage_b(s_cur, p_cur, mparts, ms, ls, False)
            accs = stage_c(j - 1, p_prv, alphas, accs)
            return mparts_next, ms, ls, alphas_new, accs

        def finish(bufs, st):
            s_cur, _, p_cur, p_prv = bufs
            _, ms, ls, alphas, accs = st
            _, ls, alphas_new = stage_b(s_cur, p_cur, None, ms, ls, True)
            accs = stage_c(qi - 1, p_prv, alphas, accs)
            accs = stage_c(qi, p_cur, alphas_new, accs)
            for hh in range(ATT_HEADS):
                l1 = jnp.sum(ls[2 * hh], axis=0, keepdims=True)
                l2 = jnp.sum(ls[2 * hh + 1], axis=0, keepdims=True)
                out_t = accs[2 * hh] / l1 - lam * (accs[2 * hh + 1] / l2)
                out = _rms(out_t.T, subg_ref[...]) * (1.0 - lambda_init)
                o_ref[pl.ds(q0, tq), hh * DIFF_V_DIM:(hh + 1) * DIFF_V_DIM] = out

        even = (s_a, s_b, p_a, p_b)
        odd = (s_b, s_a, p_b, p_a)
        p_b[...] = jnp.zeros_like(p_b)
        init = (stage_a(0, s_a),
                tuple(jnp.full((1, tq), neg, F32) for _ in streams),
                tuple(jnp.zeros((SUBLANES, tq), F32) for _ in streams),
                tuple(jnp.ones((1, tq), F32) for _ in streams),
                tuple(jnp.zeros((DIFF_V_DIM, tq), F32) for _ in streams))
        def steps(j0, n, st):
            for u in range(n):
                st = step(j0 + u, odd if u % 2 else even, st)
            return st

        st = lax.fori_loop(0, qi // ATT_UNROLL, lambda jj, c: steps(ATT_UNROLL * jj, ATT_UNROLL, c), init)
        base = (qi // ATT_UNROLL) * ATT_UNROLL
        for rem in range(ATT_UNROLL):
            @pl.when(qi % ATT_UNROLL == rem)
            def _(rem=rem):
                finish(odd if rem % 2 else even, steps(base, rem, st))

        return carry

    lax.fori_loop(0, n_qt, q_tile, 0)


def _attention(lam_qk, q, k, vt, dbias, sub_g, lambda_init):
    batch, hn, _, seq, _ = q.shape
    return pl.pallas_call(
        functools.partial(_attn_kernel, lambda_init=lambda_init),
        grid=(batch, hn // ATT_HEADS),
        in_specs=[
            pl.BlockSpec(lam_qk.shape, lambda b, h: (0, 0)),
            pl.BlockSpec((None, ATT_HEADS, 2, seq, QK_PAD), lambda b, h: (b, h, 0, 0, 0)),
            pl.BlockSpec((None, ATT_HEADS, 2, seq, QK_PAD), lambda b, h: (b, h, 0, 0, 0)),
            pl.BlockSpec((None, ATT_HEADS, DIFF_V_DIM, seq), lambda b, h: (b, h, 0, 0)),
            pl.BlockSpec((ATT_HEADS, ATT_TK, ATT_TQ), lambda b, h: (h, 0, 0)),
            pl.BlockSpec((1, DIFF_V_DIM), lambda b, h: (0, 0)),
        ],
        out_specs=pl.BlockSpec((None, seq, ATT_HEADS * DIFF_V_DIM), lambda b, h: (b, 0, h)),
        out_shape=jax.ShapeDtypeStruct((batch, seq, DIFF_WIDTH), F32),
        scratch_shapes=[
            pltpu.VMEM((2 * ATT_HEADS, ATT_TK, ATT_TQ), F32),
            pltpu.VMEM((2 * ATT_HEADS, ATT_TK, ATT_TQ), F32),
            pltpu.VMEM((2 * ATT_HEADS, ATT_TK, ATT_TQ), BF16),
            pltpu.VMEM((2 * ATT_HEADS, ATT_TK, ATT_TQ), BF16),
        ],
        compiler_params=_cparams(("parallel", "parallel")),
        name="diff_attention",
    )(lam_qk, q, k, vt, dbias, sub_g)


def _np_split3(a):
    a = a.astype(np.float32)
    a1 = a.astype(ml_dtypes.bfloat16).astype(np.float32)
    r1 = a - a1
    a2 = r1.astype(ml_dtypes.bfloat16).astype(np.float32)
    a3 = (r1 - a2).astype(ml_dtypes.bfloat16).astype(np.float32)
    return a1, a2, a3


def _alibi_tables(seq):
    hn = DIFF_HEADS
    slopes = np.float32(2.0) ** (-np.float32(ALIBI_MAX_EXP) * np.arange(1, hn + 1, dtype=np.float32)
                                 / np.float32(hn))
    slopes = (slopes.astype(np.float64) * LOG2E).astype(np.float32)
    pos = np.arange(seq, dtype=np.float32)
    kp = _np_split3(slopes[:, None] * pos[None, :])
    sl = _np_split3(np.broadcast_to(slopes[:, None], (hn, seq)))
    pos_hi = np.floor(pos / 256.0) * 256.0
    pos_lo = pos - pos_hi
    ones = np.ones((seq,), np.float32)
    k_cols = list(kp)
    q_cols = [ones, ones, ones]
    for t in sl:
        k_cols += [-t, -t]
        q_cols += [pos_hi, pos_lo]
    lane_pad = (DIFF_HEAD_DIM, QK_PAD - DIFF_HEAD_DIM - len(k_cols))
    k_aug = jnp.pad(jnp.asarray(np.stack(k_cols, axis=-1)), ((0, 0), (0, 0), lane_pad))
    q_aug = jnp.pad(jnp.asarray(np.stack(q_cols, axis=-1)), ((0, 0), lane_pad))
    kk = np.arange(ATT_TK)[:, None]
    qq = np.arange(ATT_TQ)[None, :]
    allowed = (kk // CHUNK) <= (qq // CHUNK)
    fix = np.where(kk > qq, -2.0 * (kk - qq), 0.0).astype(np.float32)
    dbias = np.where(allowed[None], slopes[:, None, None] * fix[None], -np.inf).astype(np.float32)
    return k_aug, q_aug, jnp.asarray(dbias)


def _pad_heads(w, n_blocks):
    d = w.shape[0]
    w = w.reshape(d, n_blocks, DIFF_HEAD_DIM)
    w = jnp.concatenate([w, jnp.zeros_like(w)], axis=-1)
    return w.reshape(d, n_blocks * QK_PAD).astype(BF16)


def kernel(x, a_norm_g, a_w_in, a_conv_w, a_conv_b, a_dt_bias, a_a_log, a_d_skip, a_gate_norm_g,
           a_w_out, kv_norm_g, w_kv, b_norm_g, b_w_in, b_lambda, b_sub_g, b_w_out, final_norm_g):
    batch, seq, d = x.shape
    t = batch * seq
    hn = SSM_HEADS
    x2 = x.reshape(t, d)

    for i in range(N_A_LAYERS):
        w_in = a_w_in[i]
        n_main = SSM_D_INNER + SSM_CONV_DIM
        w_main = jnp.concatenate([0.5 * w_in[:, :SSM_D_INNER], w_in[:, SSM_D_INNER:n_main]], axis=1).astype(BF16)
        w_dt = jnp.pad(w_in[:, n_main:], ((0, 0), (0, LANES - hn))).astype(BF16)
        zx, dt_raw = _a_in_proj(x2, a_norm_g[i][None], w_main, w_dt, 0.5 * a_conv_w[i],
                                0.5 * a_conv_b[i][None], seq)
        dt_row = dt_raw[:, :hn].reshape(batch, seq, hn).transpose(0, 2, 1)
        hp_row = jnp.zeros((SUBLANES, hn), F32).at[0].set(a_dt_bias[i]).at[1].set(a_a_log[i]).at[2].set(a_d_skip[i])
        hp_col = hp_row.T
        y = _ssd(zx, dt_raw, dt_row, hp_row, hp_col, a_gate_norm_g[i][None], batch, seq)
        x2 = _out_proj(y, a_w_out[i].astype(BF16), x2)

    k_aug, q_aug, dbias = _alibi_tables(seq)
    n_k = DIFF_HEADS * 2 * DIFF_HEAD_DIM
    wk = _pad_heads(w_kv[:, :n_k], DIFF_HEADS * 2)
    wv = w_kv[:, n_k:].astype(BF16)
    k, vt = _kv_proj(x2, kv_norm_g[None], wk, wv, k_aug, batch, seq)

    for j in range(N_B_LAYERS):
        layer = N_A_LAYERS + j
        lambda_init = 0.8 - 0.6 * math.exp(-0.3 * layer)
        wq = _pad_heads(b_w_in[j][:, :DIFF_WIDTH], DIFF_HEADS * 2)
        wg = b_w_in[j][:, DIFF_WIDTH:].astype(BF16)
        q, gate = _b_in_proj(x2, b_norm_g[j][None], wq, wg, q_aug, batch, seq)
        attn = _attention(b_lambda[j], q, k, vt, dbias, b_sub_g[j][None], lambda_init)
        x2 = _gated_out_proj(attn.reshape(t, DIFF_WIDTH), gate, b_w_out[j].astype(BF16), x2,
                             final_norm_g[None], final_norm=(j == N_B_LAYERS - 1))
    return x2.reshape(batch, seq, d)
```

```python
import functools
import math

import jax
import jax.numpy as jnp
import ml_dtypes
import numpy as np
from jax import lax
from jax.experimental import pallas as pl
from jax.experimental.pallas import tpu as pltpu

F32 = jnp.float32
BF16 = jnp.bfloat16

D_MODEL = 1024
DEPTH = 4
CHUNK = 64
N_B_LAYERS = DEPTH // 2
N_A_LAYERS = DEPTH - N_B_LAYERS
SSM_D_INNER = 2 * D_MODEL
SSM_HEAD_DIM = 64
SSM_HEADS = SSM_D_INNER // SSM_HEAD_DIM
SSM_GROUPS = 8
SSM_HPG = SSM_HEADS // SSM_GROUPS
SSM_STATE = 128
SSM_CONV = 4
SSM_GN = SSM_GROUPS * SSM_STATE
SSM_CONV_DIM = SSM_D_INNER + 2 * SSM_GN
DIFF_HEAD_DIM = 64
DIFF_V_DIM = 2 * DIFF_HEAD_DIM
DIFF_HEADS = D_MODEL // DIFF_V_DIM
DIFF_WIDTH = DIFF_HEADS * DIFF_V_DIM
ALIBI_MAX_EXP = 8.0
EPS = 1e-5
LOG2E = math.log2(math.e)
QK_SCALE = LOG2E / math.sqrt(DIFF_HEAD_DIM)

LANES = 128
SUBLANES = 8
VMEM_LIMIT = 56 * 1024 * 1024

ROW_TILE = 512
SSD_Q = 256
ATT_TQ = 256
ATT_TK = 512
ATT_UNROLL = 4
ATT_HEADS = 2
ATT_ROWS = 64
N_CHUNK = 256
CONV_GROUPS = 8
QK_PAD = 2 * DIFF_HEAD_DIM


def _cparams(sem):
    return pltpu.CompilerParams(dimension_semantics=sem, vmem_limit_bytes=VMEM_LIMIT)


def _rms(x, g):
    return x * lax.rsqrt(jnp.mean(x * x, axis=-1, keepdims=True) + EPS) * g


def _silu_of_half(h):
    return h * jnp.tanh(h) + h


def _silu(x):
    return _silu_of_half(0.5 * x)


def _softplus(x):
    return jnp.maximum(x, 0.0) + jnp.log1p(jnp.exp(-jnp.abs(x)))


def _trunc_bf16(a):
    bits = lax.bitcast_convert_type(a, jnp.int32) & jnp.int32(-65536)
    return lax.bitcast_convert_type(bits, F32)


def _split3(a):
    a1 = _trunc_bf16(a)
    r1 = a - a1
    a2 = _trunc_bf16(r1)
    return a1.astype(BF16), a2.astype(BF16), (r1 - a2).astype(BF16)


def _dot(a, b):
    return jnp.dot(a, b, preferred_element_type=F32)


def _dot_nt(a, b):
    return lax.dot_general(a, b, (((1,), (1,)), ((), ())), preferred_element_type=F32)


def _dot_tn(a, b):
    return lax.dot_general(a, b, (((0,), (0,)), ((), ())), preferred_element_type=F32)


def _a_in_kernel(x_ref, g_ref, w_ref, wdt_ref, cw_ref, cb_ref, o_ref, dt_ref, tail_ref,
                 *, tiles_per_seq):
    rows = x_ref.shape[0]
    di = SSM_D_INNER
    n_main = o_ref.shape[-1]

    @pl.when(pl.program_id(0) % tiles_per_seq == 0)
    def _():
        tail_ref[...] = jnp.zeros_like(tail_ref)

    h = _rms(x_ref[...], g_ref[...]).astype(BF16)
    sub = lax.broadcasted_iota(jnp.int32, (1, SUBLANES, N_CHUNK), 1)
    z_chunks = list(range(0, di, N_CHUNK))
    conv_chunks = list(range(di, n_main, N_CHUNK))
    order = []
    while z_chunks or conv_chunks:
        order += conv_chunks[:2] + z_chunks[:1]
        conv_chunks, z_chunks = conv_chunks[2:], z_chunks[1:]
    nxt = _dot(h, w_ref[:, order[0]:order[0] + N_CHUNK])
    for idx, n0 in enumerate(order):
        a = nxt
        if idx + 1 < len(order):
            nxt = _dot(h, w_ref[:, order[idx + 1]:order[idx + 1] + N_CHUNK])
        if n0 < di:
            o_ref[:, n0:n0 + N_CHUNK] = _silu_of_half(a).astype(BF16)
            continue
        c0 = n0 - di
        tail = tail_ref[:, c0:c0 + N_CHUNK][None]
        tail_ref[:, c0:c0 + N_CHUNK] = a[rows - SUBLANES:rows, :]
        a3 = a.reshape(rows // SUBLANES, SUBLANES, N_CHUNK)
        for g0 in range(0, rows // SUBLANES, CONV_GROUPS):
            rot = a3[g0:g0 + CONV_GROUPS]
            prev = tail if g0 == 0 else a3[g0 - 1:g0]
            acc = cb_ref[:, c0:c0 + N_CHUNK] + cw_ref[SSM_CONV - 1:SSM_CONV, c0:c0 + N_CHUNK] * rot
            for j in range(1, SSM_CONV):
                rot = pltpu.roll(rot, 1, axis=1)
                prev = pltpu.roll(prev, 1, axis=1)
                above = jnp.concatenate([prev, rot[:-1]], axis=0)
                k = SSM_CONV - 1 - j
                acc = acc + cw_ref[k:k + 1, c0:c0 + N_CHUNK] * jnp.where(sub >= j, rot, above)
            r0 = g0 * SUBLANES
            o_ref[r0:r0 + CONV_GROUPS * SUBLANES, n0:n0 + N_CHUNK] = (
                _silu_of_half(acc).reshape(CONV_GROUPS * SUBLANES, N_CHUNK).astype(BF16))
    dt_ref[...] = _dot(h, wdt_ref[...])


def _a_in_proj(x2, g, w_main, w_dt, conv_w, conv_b, seq):
    t, d = x2.shape
    n_main = w_main.shape[1]
    return pl.pallas_call(
        functools.partial(_a_in_kernel, tiles_per_seq=seq // ROW_TILE),
        grid=(t // ROW_TILE,),
        in_specs=[
            pl.BlockSpec((ROW_TILE, d), lambda i: (i, 0)),
            pl.BlockSpec((1, d), lambda i: (0, 0)),
            pl.BlockSpec((d, n_main), lambda i: (0, 0)),
            pl.BlockSpec((d, LANES), lambda i: (0, 0)),
            pl.BlockSpec((SSM_CONV, SSM_CONV_DIM), lambda i: (0, 0)),
            pl.BlockSpec((1, SSM_CONV_DIM), lambda i: (0, 0)),
        ],
        out_specs=[
            pl.BlockSpec((ROW_TILE, n_main), lambda i: (i, 0)),
            pl.BlockSpec((ROW_TILE, LANES), lambda i: (i, 0)),
        ],
        out_shape=[
            jax.ShapeDtypeStruct((t, n_main), BF16),
            jax.ShapeDtypeStruct((t, LANES), F32),
        ],
        scratch_shapes=[pltpu.VMEM((SUBLANES, SSM_CONV_DIM), F32)],
        compiler_params=_cparams(("arbitrary",)),
        name="mamba_in_proj",
    )(x2, g, w_main, w_dt, conv_w, conv_b)


def _ssd_kernel(zx_ref, dtc_ref, dtr_ref, hp_row_ref, hp_col_ref, gng_ref, o_ref, state_ref):
    q = zx_ref.shape[0]
    g_n, r_n, p_n, n_n = SSM_GROUPS, SSM_HPG, SSM_HEAD_DIM, SSM_STATE
    di = SSM_D_INNER
    hn = SSM_HEADS
    gw = r_n * p_n

    @pl.when(pl.program_id(1) == 0)
    def _():
        state_ref[...] = jnp.zeros_like(state_ref)

    dtb_row, alog_row, dskip_row = hp_row_ref[0:1, :], hp_row_ref[1:2, :], hp_row_ref[2:3, :]
    dtb_col, alog_col = hp_col_ref[:, 0:1], hp_col_ref[:, 1:2]
    dt_c = _softplus(dtc_ref[:, 0:hn] + dtb_row)
    adt_c = dt_c * (-jnp.exp(alog_row))
    dt_r = _softplus(dtr_ref[...] + dtb_col)
    adt_r = dt_r * (-jnp.exp(alog_col))
    ri = lax.broadcasted_iota(jnp.int32, (q, q), 0)
    ci = lax.broadcasted_iota(jnp.int32, (q, q), 1)
    causal = ri >= ci
    tril = jnp.where(causal, 1.0, 0.0).astype(BF16)
    triu = jnp.where(ri <= ci, 1.0, 0.0).astype(BF16)
    acs_c = sum(_dot(tril, part) for part in _split3(adt_c))
    acs_r = sum(_dot(part, triu) for part in _split3(adt_r))
    tot_row = acs_c[q - 1:q, :]
    eacs_c = jnp.exp(acs_c)
    toend_c = jnp.exp(tot_row - acs_c) * dt_c
    etot_row = jnp.exp(tot_row)
    acs2_c = acs_c * LOG2E
    acs2_r = acs_r * LOG2E

    lane_head = lax.broadcasted_iota(jnp.int32, (hn, gw), 1) // p_n
    row_head = lax.broadcasted_iota(jnp.int32, (hn, gw), 0)
    dt16 = dt_c.astype(BF16)
    toend16 = toend_c.astype(BF16)
    eacs_hi = _trunc_bf16(eacs_c)
    eacs_lo = (eacs_c - eacs_hi).astype(BF16)
    eacs_hi = eacs_hi.astype(BF16)
    lane_head_row = lane_head[0:1, :]
    head_mask16 = [jnp.where(lane_head_row == r, 1.0, 0.0).astype(BF16) for r in range(r_n)]

    def per_head_row(row, g):
        out = jnp.broadcast_to(row[:, g * r_n:g * r_n + 1], (1, gw))
        for r in range(1, r_n):
            hval = jnp.broadcast_to(row[:, g * r_n + r:g * r_n + r + 1], (1, gw))
            out = jnp.where(lane_head_row == r, hval, out)
        return out

    for g in range(g_n):
        spread = jnp.where(row_head == lane_head + g * r_n, 1.0, 0.0).astype(BF16)
        xs16 = zx_ref[:, di + g * gw:di + (g + 1) * gw]
        bm16 = zx_ref[:, 2 * di + g * n_n:2 * di + (g + 1) * n_n]
        cm16 = zx_ref[:, 2 * di + SSM_GN + g * n_n:2 * di + SSM_GN + (g + 1) * n_n]
        xs = xs16.astype(F32)
        cb = _dot_nt(cm16, bm16)
        xdt16 = (xs * _dot(dt16, spread)).astype(BF16)
        st = state_ref[g]
        y = _dot(cm16, st.astype(BF16)) * (_dot(eacs_hi, spread) + _dot(eacs_lo, spread))
        for r in range(r_n):
            h = g * r_n + r
            seg2 = jnp.broadcast_to(acs2_c[:, h:h + 1], (q, q)) - acs2_r[h:h + 1, :]
            lmat = (cb * jnp.where(causal, jnp.exp2(seg2), 0.0)).astype(BF16)
            y = y + _dot(lmat, xdt16 * head_mask16[r])
        y = y + per_head_row(dskip_row, g) * xs
        xend16 = (xs * _dot(toend16, spread)).astype(BF16)
        state_ref[g] = st * per_head_row(etot_row, g) + _dot_tn(bm16, xend16)
        u = y * zx_ref[:, g * gw:(g + 1) * gw].astype(F32)
        u = u * lax.rsqrt(jnp.mean(u * u, axis=-1, keepdims=True) + EPS)
        o_ref[:, g * gw:(g + 1) * gw] = (u * gng_ref[:, g * gw:(g + 1) * gw]).astype(BF16)


def _ssd(zx, dt_col, dt_row, hp_row, hp_col, gate_g, batch, seq):
    q = SSD_Q
    nc = seq // q
    n_main = zx.shape[1]
    hn = SSM_HEADS
    return pl.pallas_call(
        _ssd_kernel,
        grid=(batch, nc),
        in_specs=[
            pl.BlockSpec((q, n_main), lambda b, c: (b * nc + c, 0)),
            pl.BlockSpec((q, LANES), lambda b, c: (b * nc + c, 0)),
            pl.BlockSpec((None, hn, q), lambda b, c: (b, 0, c)),
            pl.BlockSpec((SUBLANES, hn), lambda b, c: (0, 0)),
            pl.BlockSpec((hn, SUBLANES), lambda b, c: (0, 0)),
            pl.BlockSpec((1, SSM_D_INNER), lambda b, c: (0, 0)),
        ],
        out_specs=pl.BlockSpec((q, SSM_D_INNER), lambda b, c: (b * nc + c, 0)),
        out_shape=jax.ShapeDtypeStruct((batch * seq, SSM_D_INNER), BF16),
        scratch_shapes=[
            pltpu.VMEM((SSM_GROUPS, SSM_STATE, SSM_HPG * SSM_HEAD_DIM), F32),
        ],
        compiler_params=_cparams(("parallel", "arbitrary")),
        name="mamba_ssd",
    )(zx, dt_col, dt_row, hp_row, hp_col, gate_g)


def _out_proj_kernel(y_ref, w_ref, x_ref, o_ref):
    o_ref[...] = x_ref[...] + _dot(y_ref[...], w_ref[...])


def _out_proj(y, w, x2):
    t, k = y.shape
    d = w.shape[1]
    return pl.pallas_call(
        _out_proj_kernel,
        grid=(t // ROW_TILE,),
        in_specs=[
            pl.BlockSpec((ROW_TILE, k), lambda i: (i, 0)),
            pl.BlockSpec((k, d), lambda i: (0, 0)),
            pl.BlockSpec((ROW_TILE, d), lambda i: (i, 0)),
        ],
        out_specs=pl.BlockSpec((ROW_TILE, d), lambda i: (i, 0)),
        out_shape=jax.ShapeDtypeStruct((t, d), F32),
        compiler_params=_cparams(("parallel",)),
        name="mamba_out_proj",
    )(y, w, x2)


def _gated_out_proj_kernel(a_ref, gate_ref, w_ref, x_ref, fg_ref, o_ref, *, final_norm):
    gate = gate_ref[...].astype(F32)
    y = (a_ref[...] * _silu(gate)).astype(BF16)
    out = x_ref[...] + _dot(y, w_ref[...])
    if final_norm:
        out = _rms(out, fg_ref[...])
    o_ref[...] = out


def _gated_out_proj(attn, gate, w, x2, final_g, final_norm):
    t, k = attn.shape
    d = w.shape[1]
    return pl.pallas_call(
        functools.partial(_gated_out_proj_kernel, final_norm=final_norm),
        grid=(t // ROW_TILE,),
        in_specs=[
            pl.BlockSpec((ROW_TILE, k), lambda i: (i, 0)),
            pl.BlockSpec((ROW_TILE, k), lambda i: (i, 0)),
            pl.BlockSpec((k, d), lambda i: (0, 0)),
            pl.BlockSpec((ROW_TILE, d), lambda i: (i, 0)),
            pl.BlockSpec((1, d), lambda i: (0, 0)),
        ],
        out_specs=pl.BlockSpec((ROW_TILE, d), lambda i: (i, 0)),
        out_shape=jax.ShapeDtypeStruct((t, d), F32),
        compiler_params=_cparams(("parallel",)),
        name="attn_out_proj",
    )(attn, gate, w, x2, final_g)


def _kv_kernel(x_ref, g_ref, wk_ref, wv_ref, aug_ref, k_ref, vt_ref):
    h = _rms(x_ref[...], g_ref[...]).astype(BF16)
    hn = DIFF_HEADS
    for n0 in range(0, wk_ref.shape[1], N_CHUNK):
        kc = _dot(h, wk_ref[:, n0:n0 + N_CHUNK])
        for j in range(N_CHUNK // QK_PAD):
            idx = n0 // QK_PAD + j
            k_ref[idx // 2, idx % 2] = (kc[:, j * QK_PAD:(j + 1) * QK_PAD]
                                        + aug_ref[idx // 2]).astype(BF16)
    for n0 in range(0, wv_ref.shape[1], N_CHUNK):
        vc = _dot(h, wv_ref[:, n0:n0 + N_CHUNK])
        for j in range(N_CHUNK // DIFF_V_DIM):
            head = n0 // DIFF_V_DIM + j
            vt_ref[head] = vc[:, j * DIFF_V_DIM:(j + 1) * DIFF_V_DIM].T.astype(BF16)
    del hn


def _kv_proj(x2, g, wk, wv, k_aug, batch, seq):
    d = x2.shape[1]
    nt = seq // ROW_TILE
    hn = DIFF_HEADS
    return pl.pallas_call(
        _kv_kernel,
        grid=(batch, nt),
        in_specs=[
            pl.BlockSpec((ROW_TILE, d), lambda b, i: (b * nt + i, 0)),
            pl.BlockSpec((1, d), lambda b, i: (0, 0)),
            pl.BlockSpec(wk.shape, lambda b, i: (0, 0)),
            pl.BlockSpec(wv.shape, lambda b, i: (0, 0)),
            pl.BlockSpec((hn, ROW_TILE, QK_PAD), lambda b, i: (0, i, 0)),
        ],
        out_specs=[
            pl.BlockSpec((None, hn, 2, ROW_TILE, QK_PAD), lambda b, i: (b, 0, 0, i, 0)),
            pl.BlockSpec((None, hn, DIFF_V_DIM, ROW_TILE), lambda b, i: (b, 0, 0, i)),
        ],
        out_shape=[
            jax.ShapeDtypeStruct((batch, hn, 2, seq, QK_PAD), BF16),
            jax.ShapeDtypeStruct((batch, hn, DIFF_V_DIM, seq), BF16),
        ],
        compiler_params=_cparams(("parallel", "parallel")),
        name="kv_proj",
    )(x2, g, wk, wv, k_aug)


def _b_in_kernel(x_ref, g_ref, wq_ref, wg_ref, aug_ref, q_ref, gate_ref):
    h = _rms(x_ref[...], g_ref[...]).astype(BF16)
    for n0 in range(0, wq_ref.shape[1], N_CHUNK):
        qc = _dot(h, wq_ref[:, n0:n0 + N_CHUNK]) * QK_SCALE
        for j in range(N_CHUNK // QK_PAD):
            idx = n0 // QK_PAD + j
            q_ref[idx // 2, idx % 2] = (qc[:, j * QK_PAD:(j + 1) * QK_PAD] + aug_ref[...]).astype(BF16)
    for n0 in range(0, wg_ref.shape[1], N_CHUNK):
        gate_ref[:, n0:n0 + N_CHUNK] = _dot(h, wg_ref[:, n0:n0 + N_CHUNK]).astype(BF16)


def _b_in_proj(x2, g, wq, wg, q_aug, batch, seq):
    d = x2.shape[1]
    nt = seq // ROW_TILE
    hn = DIFF_HEADS
    return pl.pallas_call(
        _b_in_kernel,
        grid=(batch, nt),
        in_specs=[
            pl.BlockSpec((ROW_TILE, d), lambda b, i: (b * nt + i, 0)),
            pl.BlockSpec((1, d), lambda b, i: (0, 0)),
            pl.BlockSpec(wq.shape, lambda b, i: (0, 0)),
            pl.BlockSpec(wg.shape, lambda b, i: (0, 0)),
            pl.BlockSpec((ROW_TILE, QK_PAD), lambda b, i: (i, 0)),
        ],
        out_specs=[
            pl.BlockSpec((None, hn, 2, ROW_TILE, QK_PAD), lambda b, i: (b, 0, 0, i, 0)),
            pl.BlockSpec((ROW_TILE, DIFF_WIDTH), lambda b, i: (b * nt + i, 0)),
        ],
        out_shape=[
            jax.ShapeDtypeStruct((batch, hn, 2, seq, QK_PAD), BF16),
            jax.ShapeDtypeStruct((batch * seq, DIFF_WIDTH), BF16),
        ],
        compiler_params=_cparams(("parallel", "parallel")),
        name="attn_in_proj",
    )(x2, g, wq, wg, q_aug)


def _attn_kernel(lam_ref, q_ref, k_ref, vt_ref, dbias_ref, subg_ref, o_ref,
                 s_a, s_b, p_a, p_b, m_scr, *, lambda_init):
    tq, tk = ATT_TQ, ATT_TK
    seq = q_ref.shape[2]
    streams = [(hh, i) for hh in range(ATT_HEADS) for i in range(2)]
    n_qt = seq // tq
    lf = lam_ref[...]
    lam = (jnp.exp(jnp.sum(lf[0:1, :] * lf[1:2, :], axis=-1, keepdims=True))
           - jnp.exp(jnp.sum(lf[2:3, :] * lf[3:4, :], axis=-1, keepdims=True)) + lambda_init)
    neg = jnp.float32(-1e30)

    def fold(x, op):
        return op(x.reshape(x.shape[0] // SUBLANES, SUBLANES, tq), axis=0)

    def load_q(qi):
        q0 = pl.multiple_of(qi * tq, tq)
        return [q_ref[hh, i, pl.ds(q0, tq), :] for hh, i in streams]

    def scores_into(qs, t, s_buf):
        k0 = pl.multiple_of(t * tk, tk)
        mparts = []
        for n, (hh, i) in enumerate(streams):
            s = _dot_nt(k_ref[hh, i, pl.ds(k0, tk), :], qs[n])
            s_buf[n] = s
            mparts.append(fold(s, jnp.max))
        return tuple(mparts)

    def q_tile(qi, carry):
        q0 = pl.multiple_of(qi * tq, tq)
        n_full = qi // (tk // tq)
        which = qi % (tk // tq)
        qs = load_q(qi)

        def stage_a(t, s_buf):
            return scores_into(qs, t, s_buf)

        def stage_b(s_buf, p_buf, mparts, ms, ls, diag):
            new_m, new_l, alphas = [], [], []
            for n, (hh, _) in enumerate(streams):
                def scores(r0):
                    s = s_buf[n, r0:r0 + ATT_ROWS, :]
                    return s + dbias_ref[hh, which, r0:r0 + ATT_ROWS, :] if diag else s

                if diag:
                    mpart = fold(scores(0), jnp.max)
                    for r0 in range(ATT_ROWS, tk, ATT_ROWS):
                        mpart = jnp.maximum(mpart, fold(scores(r0), jnp.max))
                else:
                    mpart = mparts[n]
                m_new = jnp.maximum(ms[n], jnp.max(mpart, axis=0, keepdims=True))
                alpha = jnp.exp2(ms[n] - m_new)
                lsum = alpha * ls[n]
                for r0 in range(0, tk, ATT_ROWS):
                    p = jnp.exp2(scores(r0) - m_new)
                    lsum = lsum + fold(p, jnp.sum)
                    p_buf[n, r0:r0 + ATT_ROWS, :] = p.astype(BF16)
                new_l.append(lsum)
                new_m.append(m_new)
                alphas.append(alpha)
            return tuple(new_m), tuple(new_l), tuple(alphas)

        def stage_c(t, p_buf, alphas, accs):
            k0 = pl.multiple_of(jnp.maximum(t, 0) * tk, tk)
            vts = [vt_ref[hh, :, pl.ds(k0, tk)] for hh in range(ATT_HEADS)]
            return tuple(alphas[n] * accs[n] + _dot(vts[hh], p_buf[n])
                         for n, (hh, _) in enumerate(streams))

        def step(j, bufs, st):
            s_cur, s_nxt, p_cur, p_prv = bufs
            mparts, ms, ls, alphas, accs = st
            mparts_next = stage_a(j + 1, s_nxt)
            ms, ls, alphas_new = stage_b(s_cur, p_cur, mparts, ms, ls, False)
            accs = stage_c(j - 1, p_prv, alphas, accs)
            return mparts_next, ms, ls, alphas_new, accs

        def finish(bufs, st):
            s_cur, _, p_cur, p_prv = bufs
            _, ms, ls, alphas, accs = st
            _, ls, alphas_new = stage_b(s_cur, p_cur, None, ms, ls, True)
            nxt = scores_into(load_q(jnp.minimum(qi + 1, n_qt - 1)), 0, s_a)
            for n in range(len(streams)):
                m_scr[n] = nxt[n]
            accs = stage_c(n_full - 1, p_prv, alphas, accs)
            accs = stage_c(n_full, p_cur, alphas_new, accs)
            for hh in range(ATT_HEADS):
                l1 = jnp.sum(ls[2 * hh], axis=0, keepdims=True)
                l2 = jnp.sum(ls[2 * hh + 1], axis=0, keepdims=True)
                out_t = accs[2 * hh] / l1 - lam * (accs[2 * hh + 1] / l2)
                out = _rms(out_t.T, subg_ref[...]) * (1.0 - lambda_init)
                o_ref[pl.ds(q0, tq), hh * DIFF_V_DIM:(hh + 1) * DIFF_V_DIM] = out

        even = (s_a, s_b, p_a, p_b)
        odd = (s_b, s_a, p_b, p_a)
        p_b[...] = jnp.zeros_like(p_b)
        init = (tuple(m_scr[n] for n in range(len(streams))),
                tuple(jnp.full((1, tq), neg, F32) for _ in streams),
                tuple(jnp.zeros((SUBLANES, tq), F32) for _ in streams),
                tuple(jnp.ones((1, tq), F32) for _ in streams),
                tuple(jnp.zeros((DIFF_V_DIM, tq), F32) for _ in streams))
        def steps(j0, n, st):
            for u in range(n):
                st = step(j0 + u, odd if u % 2 else even, st)
            return st

        st = lax.fori_loop(0, n_full // ATT_UNROLL,
                           lambda jj, c: steps(ATT_UNROLL * jj, ATT_UNROLL, c), init)
        base = (n_full // ATT_UNROLL) * ATT_UNROLL
        for rem in range(ATT_UNROLL):
            @pl.when(n_full % ATT_UNROLL == rem)
            def _(rem=rem):
                finish(odd if rem % 2 else even, steps(base, rem, st))

        return carry

    first = scores_into(load_q(0), 0, s_a)
    for n in range(len(streams)):
        m_scr[n] = first[n]
    lax.fori_loop(0, n_qt, q_tile, 0)


def _attention(lam_qk, q, k, vt, dbias, sub_g, lambda_init):
    batch, hn, _, seq, _ = q.shape
    return pl.pallas_call(
        functools.partial(_attn_kernel, lambda_init=lambda_init),
        grid=(batch, hn // ATT_HEADS),
        in_specs=[
            pl.BlockSpec(lam_qk.shape, lambda b, h: (0, 0)),
            pl.BlockSpec((None, ATT_HEADS, 2, seq, QK_PAD), lambda b, h: (b, h, 0, 0, 0)),
            pl.BlockSpec((None, ATT_HEADS, 2, seq, QK_PAD), lambda b, h: (b, h, 0, 0, 0)),
            pl.BlockSpec((None, ATT_HEADS, DIFF_V_DIM, seq), lambda b, h: (b, h, 0, 0)),
            pl.BlockSpec((ATT_HEADS, ATT_TK // ATT_TQ, ATT_TK, ATT_TQ), lambda b, h: (h, 0, 0, 0)),
            pl.BlockSpec((1, DIFF_V_DIM), lambda b, h: (0, 0)),
        ],
        out_specs=pl.BlockSpec((None, seq, ATT_HEADS * DIFF_V_DIM), lambda b, h: (b, 0, h)),
        out_shape=jax.ShapeDtypeStruct((batch, seq, DIFF_WIDTH), F32),
        scratch_shapes=[
            pltpu.VMEM((2 * ATT_HEADS, ATT_TK, ATT_TQ), F32),
            pltpu.VMEM((2 * ATT_HEADS, ATT_TK, ATT_TQ), F32),
            pltpu.VMEM((2 * ATT_HEADS, ATT_TK, ATT_TQ), BF16),
            pltpu.VMEM((2 * ATT_HEADS, ATT_TK, ATT_TQ), BF16),
            pltpu.VMEM((2 * ATT_HEADS, SUBLANES, ATT_TQ), F32),
        ],
        compiler_params=_cparams(("parallel", "parallel")),
        name="diff_attention",
    )(lam_qk, q, k, vt, dbias, sub_g)


def _np_split3(a):
    a = a.astype(np.float32)
    a1 = a.astype(ml_dtypes.bfloat16).astype(np.float32)
    r1 = a - a1
    a2 = r1.astype(ml_dtypes.bfloat16).astype(np.float32)
    a3 = (r1 - a2).astype(ml_dtypes.bfloat16).astype(np.float32)
    return a1, a2, a3


def _alibi_tables(seq):
    hn = DIFF_HEADS
    slopes = np.float32(2.0) ** (-np.float32(ALIBI_MAX_EXP) * np.arange(1, hn + 1, dtype=np.float32)
                                 / np.float32(hn))
    slopes = (slopes.astype(np.float64) * LOG2E).astype(np.float32)
    pos = np.arange(seq, dtype=np.float32)
    kp = _np_split3(slopes[:, None] * pos[None, :])
    sl = _np_split3(np.broadcast_to(slopes[:, None], (hn, seq)))
    pos_hi = np.floor(pos / 256.0) * 256.0
    pos_lo = pos - pos_hi
    ones = np.ones((seq,), np.float32)
    k_cols = list(kp)
    q_cols = [ones, ones, ones]
    for t in sl:
        k_cols += [-t, -t]
        q_cols += [pos_hi, pos_lo]
    lane_pad = (DIFF_HEAD_DIM, QK_PAD - DIFF_HEAD_DIM - len(k_cols))
    k_aug = jnp.pad(jnp.asarray(np.stack(k_cols, axis=-1)), ((0, 0), (0, 0), lane_pad))
    q_aug = jnp.pad(jnp.asarray(np.stack(q_cols, axis=-1)), ((0, 0), lane_pad))
    kk = np.arange(ATT_TK)[None, :, None]
    qq = np.arange(ATT_TQ)[None, None, :] + ATT_TQ * np.arange(ATT_TK // ATT_TQ)[:, None, None]
    allowed = (kk // CHUNK) <= (qq // CHUNK)
    fix = np.where(kk > qq, -2.0 * (kk - qq), 0.0).astype(np.float32)
    dbias = np.where(allowed[None], slopes[:, None, None, None] * fix[None], -np.inf).astype(np.float32)
    return k_aug, q_aug, jnp.asarray(dbias)


def _pad_heads(w, n_blocks):
    d = w.shape[0]
    w = w.reshape(d, n_blocks, DIFF_HEAD_DIM)
    w = jnp.concatenate([w, jnp.zeros_like(w)], axis=-1)
    return w.reshape(d, n_blocks * QK_PAD).astype(BF16)


def kernel(x, a_norm_g, a_w_in, a_conv_w, a_conv_b, a_dt_bias, a_a_log, a_d_skip, a_gate_norm_g,
           a_w_out, kv_norm_g, w_kv, b_norm_g, b_w_in, b_lambda, b_sub_g, b_w_out, final_norm_g):
    batch, seq, d = x.shape
    t = batch * seq
    hn = SSM_HEADS
    x2 = x.reshape(t, d)

    for i in range(N_A_LAYERS):
        w_in = a_w_in[i]
        n_main = SSM_D_INNER + SSM_CONV_DIM
        w_main = jnp.concatenate([0.5 * w_in[:, :SSM_D_INNER], w_in[:, SSM_D_INNER:n_main]], axis=1).astype(BF16)
        w_dt = jnp.pad(w_in[:, n_main:], ((0, 0), (0, LANES - hn))).astype(BF16)
        zx, dt_raw = _a_in_proj(x2, a_norm_g[i][None], w_main, w_dt, 0.5 * a_conv_w[i],
                                0.5 * a_conv_b[i][None], seq)
        dt_row = dt_raw[:, :hn].reshape(batch, seq, hn).transpose(0, 2, 1)
        hp_row = jnp.zeros((SUBLANES, hn), F32).at[0].set(a_dt_bias[i]).at[1].set(a_a_log[i]).at[2].set(a_d_skip[i])
        hp_col = hp_row.T
        y = _ssd(zx, dt_raw, dt_row, hp_row, hp_col, a_gate_norm_g[i][None], batch, seq)
        x2 = _out_proj(y, a_w_out[i].astype(BF16), x2)

    k_aug, q_aug, dbias = _alibi_tables(seq)
    n_k = DIFF_HEADS * 2 * DIFF_HEAD_DIM
    wk = _pad_heads(w_kv[:, :n_k], DIFF_HEADS * 2)
    wv = w_kv[:, n_k:].astype(BF16)
    k, vt = _kv_proj(x2, kv_norm_g[None], wk, wv, k_aug, batch, seq)

    for j in range(N_B_LAYERS):
        layer = N_A_LAYERS + j
        lambda_init = 0.8 - 0.6 * math.exp(-0.3 * layer)
        wq = _pad_heads(b_w_in[j][:, :DIFF_WIDTH], DIFF_HEADS * 2)
        wg = b_w_in[j][:, DIFF_WIDTH:].astype(BF16)
        q, gate = _b_in_proj(x2, b_norm_g[j][None], wq, wg, q_aug, batch, seq)
        attn = _attention(b_lambda[j], q, k, vt, dbias, b_sub_g[j][None], lambda_init)
        x2 = _gated_out_proj(attn.reshape(t, DIFF_WIDTH), gate, b_w_out[j].astype(BF16), x2,
                             final_norm_g[None], final_norm=(j == N_B_LAYERS - 1))
    return x2.reshape(batch, seq, d)
```

```python
import functools
import math

import jax
import jax.numpy as jnp
import ml_dtypes
import numpy as np
from jax import lax
from jax.experimental import pallas as pl
from jax.experimental.pallas import tpu as pltpu

F32 = jnp.float32
BF16 = jnp.bfloat16

D_MODEL = 1024
DEPTH = 4
CHUNK = 64
N_B_LAYERS = DEPTH // 2
N_A_LAYERS = DEPTH - N_B_LAYERS
SSM_D_INNER = 2 * D_MODEL
SSM_HEAD_DIM = 64
SSM_HEADS = SSM_D_INNER // SSM_HEAD_DIM
SSM_GROUPS = 8
SSM_HPG = SSM_HEADS // SSM_GROUPS
SSM_STATE = 128
SSM_CONV = 4
SSM_GN = SSM_GROUPS * SSM_STATE
SSM_CONV_DIM = SSM_D_INNER + 2 * SSM_GN
DIFF_HEAD_DIM = 64
DIFF_V_DIM = 2 * DIFF_HEAD_DIM
DIFF_HEADS = D_MODEL // DIFF_V_DIM
DIFF_WIDTH = DIFF_HEADS * DIFF_V_DIM
ALIBI_MAX_EXP = 8.0
EPS = 1e-5
LOG2E = math.log2(math.e)
QK_SCALE = LOG2E / math.sqrt(DIFF_HEAD_DIM)

LANES = 128
SUBLANES = 8
VMEM_LIMIT = 56 * 1024 * 1024

ROW_TILE = 512
SSD_Q = 256
ATT_TQ = 256
ATT_TK = 512
ATT_UNROLL = 4
ATT_HEADS = 2
ATT_ROWS = 64
N_CHUNK = 256
CONV_GROUPS = 8
QK_PAD = 2 * DIFF_HEAD_DIM


def _cparams(sem):
    return pltpu.CompilerParams(dimension_semantics=sem, vmem_limit_bytes=VMEM_LIMIT)


def _rms(x, g):
    return x * lax.rsqrt(jnp.mean(x * x, axis=-1, keepdims=True) + EPS) * g


def _silu_of_half(h):
    return h * jnp.tanh(h) + h


def _silu(x):
    return _silu_of_half(0.5 * x)


def _softplus(x):
    return jnp.maximum(x, 0.0) + jnp.log1p(jnp.exp(-jnp.abs(x)))


def _trunc_bf16(a):
    bits = lax.bitcast_convert_type(a, jnp.int32) & jnp.int32(-65536)
    return lax.bitcast_convert_type(bits, F32)


def _split3(a):
    a1 = _trunc_bf16(a)
    r1 = a - a1
    a2 = _trunc_bf16(r1)
    return a1.astype(BF16), a2.astype(BF16), (r1 - a2).astype(BF16)


def _dot(a, b):
    return jnp.dot(a, b, preferred_element_type=F32)


def _dot_nt(a, b):
    return lax.dot_general(a, b, (((1,), (1,)), ((), ())), preferred_element_type=F32)


def _dot_tn(a, b):
    return lax.dot_general(a, b, (((0,), (0,)), ((), ())), preferred_element_type=F32)


def _a_in_kernel(x_ref, g_ref, w_ref, wdt_ref, cw_ref, cb_ref, o_ref, dt_ref, tail_ref,
                 *, tiles_per_seq):
    rows = x_ref.shape[0]
    di = SSM_D_INNER
    n_main = o_ref.shape[-1]

    @pl.when(pl.program_id(0) % tiles_per_seq == 0)
    def _():
        tail_ref[...] = jnp.zeros_like(tail_ref)

    h = _rms(x_ref[...], g_ref[...]).astype(BF16)
    sub = lax.broadcasted_iota(jnp.int32, (1, SUBLANES, N_CHUNK), 1)
    z_chunks = list(range(0, di, N_CHUNK))
    conv_chunks = list(range(di, n_main, N_CHUNK))
    order = []
    while z_chunks or conv_chunks:
        order += conv_chunks[:2] + z_chunks[:1]
        conv_chunks, z_chunks = conv_chunks[2:], z_chunks[1:]
    nxt = _dot(h, w_ref[:, order[0]:order[0] + N_CHUNK])
    for idx, n0 in enumerate(order):
        a = nxt
        if idx + 1 < len(order):
            nxt = _dot(h, w_ref[:, order[idx + 1]:order[idx + 1] + N_CHUNK])
        if n0 < di:
            o_ref[:, n0:n0 + N_CHUNK] = _silu_of_half(a).astype(BF16)
            continue
        c0 = n0 - di
        tail = tail_ref[:, c0:c0 + N_CHUNK][None]
        tail_ref[:, c0:c0 + N_CHUNK] = a[rows - SUBLANES:rows, :]
        a3 = a.reshape(rows // SUBLANES, SUBLANES, N_CHUNK)
        for g0 in range(0, rows // SUBLANES, CONV_GROUPS):
            rot = a3[g0:g0 + CONV_GROUPS]
            prev = tail if g0 == 0 else a3[g0 - 1:g0]
            acc = cb_ref[:, c0:c0 + N_CHUNK] + cw_ref[SSM_CONV - 1:SSM_CONV, c0:c0 + N_CHUNK] * rot
            for j in range(1, SSM_CONV):
                rot = pltpu.roll(rot, 1, axis=1)
                prev = pltpu.roll(prev, 1, axis=1)
                above = jnp.concatenate([prev, rot[:-1]], axis=0)
                k = SSM_CONV - 1 - j
                acc = acc + cw_ref[k:k + 1, c0:c0 + N_CHUNK] * jnp.where(sub >= j, rot, above)
            r0 = g0 * SUBLANES
            o_ref[r0:r0 + CONV_GROUPS * SUBLANES, n0:n0 + N_CHUNK] = (
                _silu_of_half(acc).reshape(CONV_GROUPS * SUBLANES, N_CHUNK).astype(BF16))
    dt_ref[...] = _dot(h, wdt_ref[...])


def _a_in_proj(x2, g, w_main, w_dt, conv_w, conv_b, seq):
    t, d = x2.shape
    n_main = w_main.shape[1]
    return pl.pallas_call(
        functools.partial(_a_in_kernel, tiles_per_seq=seq // ROW_TILE),
        grid=(t // ROW_TILE,),
        in_specs=[
            pl.BlockSpec((ROW_TILE, d), lambda i: (i, 0)),
            pl.BlockSpec((1, d), lambda i: (0, 0)),
            pl.BlockSpec((d, n_main), lambda i: (0, 0)),
            pl.BlockSpec((d, LANES), lambda i: (0, 0)),
            pl.BlockSpec((SSM_CONV, SSM_CONV_DIM), lambda i: (0, 0)),
            pl.BlockSpec((1, SSM_CONV_DIM), lambda i: (0, 0)),
        ],
        out_specs=[
            pl.BlockSpec((ROW_TILE, n_main), lambda i: (i, 0)),
            pl.BlockSpec((ROW_TILE, LANES), lambda i: (i, 0)),
        ],
        out_shape=[
            jax.ShapeDtypeStruct((t, n_main), BF16),
            jax.ShapeDtypeStruct((t, LANES), F32),
        ],
        scratch_shapes=[pltpu.VMEM((SUBLANES, SSM_CONV_DIM), F32)],
        compiler_params=_cparams(("arbitrary",)),
        name="mamba_in_proj",
    )(x2, g, w_main, w_dt, conv_w, conv_b)


def _ssd_kernel(zx_ref, dtc_ref, dtr_ref, hp_row_ref, hp_col_ref, gng_ref, o_ref, state_ref):
    q = zx_ref.shape[0]
    g_n, r_n, p_n, n_n = SSM_GROUPS, SSM_HPG, SSM_HEAD_DIM, SSM_STATE
    di = SSM_D_INNER
    hn = SSM_HEADS
    gw = r_n * p_n

    @pl.when(pl.program_id(1) == 0)
    def _():
        state_ref[...] = jnp.zeros_like(state_ref)

    dtb_row, alog_row, dskip_row = hp_row_ref[0:1, :], hp_row_ref[1:2, :], hp_row_ref[2:3, :]
    dtb_col, alog_col = hp_col_ref[:, 0:1], hp_col_ref[:, 1:2]
    dt_c = _softplus(dtc_ref[:, 0:hn] + dtb_row)
    adt_c = dt_c * (-jnp.exp(alog_row))
    dt_r = _softplus(dtr_ref[...] + dtb_col)
    adt_r = dt_r * (-jnp.exp(alog_col))
    ri = lax.broadcasted_iota(jnp.int32, (q, q), 0)
    ci = lax.broadcasted_iota(jnp.int32, (q, q), 1)
    causal = ri >= ci
    tril = jnp.where(causal, 1.0, 0.0).astype(BF16)
    triu = jnp.where(ri <= ci, 1.0, 0.0).astype(BF16)
    acs_c = sum(_dot(tril, part) for part in _split3(adt_c))
    acs_r = sum(_dot(part, triu) for part in _split3(adt_r))
    tot_row = acs_c[q - 1:q, :]
    eacs_c = jnp.exp(acs_c)
    toend_c = jnp.exp(tot_row - acs_c) * dt_c
    etot_row = jnp.exp(tot_row)
    acs2_c = acs_c * LOG2E
    acs2_r = acs_r * LOG2E

    lane_head = lax.broadcasted_iota(jnp.int32, (hn, gw), 1) // p_n
    row_head = lax.broadcasted_iota(jnp.int32, (hn, gw), 0)
    dt16 = dt_c.astype(BF16)
    toend16 = toend_c.astype(BF16)
    eacs_hi = _trunc_bf16(eacs_c)
    eacs_lo = (eacs_c - eacs_hi).astype(BF16)
    eacs_hi = eacs_hi.astype(BF16)
    lane_head_row = lane_head[0:1, :]
    head_mask16 = [jnp.where(lane_head_row == r, 1.0, 0.0).astype(BF16) for r in range(r_n)]

    def per_head_row(row, g):
        out = jnp.broadcast_to(row[:, g * r_n:g * r_n + 1], (1, gw))
        for r in range(1, r_n):
            hval = jnp.broadcast_to(row[:, g * r_n + r:g * r_n + r + 1], (1, gw))
            out = jnp.where(lane_head_row == r, hval, out)
        return out

    for g in range(g_n):
        spread = jnp.where(row_head == lane_head + g * r_n, 1.0, 0.0).astype(BF16)
        xs16 = zx_ref[:, di + g * gw:di + (g + 1) * gw]
        bm16 = zx_ref[:, 2 * di + g * n_n:2 * di + (g + 1) * n_n]
        cm16 = zx_ref[:, 2 * di + SSM_GN + g * n_n:2 * di + SSM_GN + (g + 1) * n_n]
        xs = xs16.astype(F32)
        cb = _dot_nt(cm16, bm16)
        xdt16 = (xs * _dot(dt16, spread)).astype(BF16)
        st = state_ref[g]
        y = _dot(cm16, st.astype(BF16)) * (_dot(eacs_hi, spread) + _dot(eacs_lo, spread))
        for r in range(r_n):
            h = g * r_n + r
            seg2 = jnp.broadcast_to(acs2_c[:, h:h + 1], (q, q)) - acs2_r[h:h + 1, :]
            lmat = (cb * jnp.where(causal, jnp.exp2(seg2), 0.0)).astype(BF16)
            y = y + _dot(lmat, xdt16 * head_mask16[r])
        y = y + per_head_row(dskip_row, g) * xs
        xend16 = (xs * _dot(toend16, spread)).astype(BF16)
        state_ref[g] = st * per_head_row(etot_row, g) + _dot_tn(bm16, xend16)
        u = y * zx_ref[:, g * gw:(g + 1) * gw].astype(F32)
        u = u * lax.rsqrt(jnp.mean(u * u, axis=-1, keepdims=True) + EPS)
        o_ref[:, g * gw:(g + 1) * gw] = (u * gng_ref[:, g * gw:(g + 1) * gw]).astype(BF16)


def _ssd(zx, dt_col, dt_row, hp_row, hp_col, gate_g, batch, seq):
    q = SSD_Q
    nc = seq // q
    n_main = zx.shape[1]
    hn = SSM_HEADS
    return pl.pallas_call(
        _ssd_kernel,
        grid=(batch, nc),
        in_specs=[
            pl.BlockSpec((q, n_main), lambda b, c: (b * nc + c, 0)),
            pl.BlockSpec((q, LANES), lambda b, c: (b * nc + c, 0)),
            pl.BlockSpec((None, hn, q), lambda b, c: (b, 0, c)),
            pl.BlockSpec((SUBLANES, hn), lambda b, c: (0, 0)),
            pl.BlockSpec((hn, SUBLANES), lambda b, c: (0, 0)),
            pl.BlockSpec((1, SSM_D_INNER), lambda b, c: (0, 0)),
        ],
        out_specs=pl.BlockSpec((q, SSM_D_INNER), lambda b, c: (b * nc + c, 0)),
        out_shape=jax.ShapeDtypeStruct((batch * seq, SSM_D_INNER), BF16),
        scratch_shapes=[
            pltpu.VMEM((SSM_GROUPS, SSM_STATE, SSM_HPG * SSM_HEAD_DIM), F32),
        ],
        compiler_params=_cparams(("parallel", "arbitrary")),
        name="mamba_ssd",
    )(zx, dt_col, dt_row, hp_row, hp_col, gate_g)


def _out_proj_kernel(y_ref, w_ref, x_ref, o_ref):
    o_ref[...] = x_ref[...] + _dot(y_ref[...], w_ref[...])


def _out_proj(y, w, x2):
    t, k = y.shape
    d = w.shape[1]
    return pl.pallas_call(
        _out_proj_kernel,
        grid=(t // ROW_TILE,),
        in_specs=[
            pl.BlockSpec((ROW_TILE, k), lambda i: (i, 0)),
            pl.BlockSpec((k, d), lambda i: (0, 0)),
            pl.BlockSpec((ROW_TILE, d), lambda i: (i, 0)),
        ],
        out_specs=pl.BlockSpec((ROW_TILE, d), lambda i: (i, 0)),
        out_shape=jax.ShapeDtypeStruct((t, d), F32),
        compiler_params=_cparams(("parallel",)),
        name="mamba_out_proj",
    )(y, w, x2)


def _gated_out_proj_kernel(a_ref, gate_ref, w_ref, x_ref, fg_ref, o_ref, *, final_norm):
    gate = gate_ref[...].astype(F32)
    y = (a_ref[...] * _silu(gate)).astype(BF16)
    out = x_ref[...] + _dot(y, w_ref[...])
    if final_norm:
        out = _rms(out, fg_ref[...])
    o_ref[...] = out


def _gated_out_proj(attn, gate, w, x2, final_g, final_norm):
    t, k = attn.shape
    d = w.shape[1]
    return pl.pallas_call(
        functools.partial(_gated_out_proj_kernel, final_norm=final_norm),
        grid=(t // ROW_TILE,),
        in_specs=[
            pl.BlockSpec((ROW_TILE, k), lambda i: (i, 0)),
            pl.BlockSpec((ROW_TILE, k), lambda i: (i, 0)),
            pl.BlockSpec((k, d), lambda i: (0, 0)),
            pl.BlockSpec((ROW_TILE, d), lambda i: (i, 0)),
            pl.BlockSpec((1, d), lambda i: (0, 0)),
        ],
        out_specs=pl.BlockSpec((ROW_TILE, d), lambda i: (i, 0)),
        out_shape=jax.ShapeDtypeStruct((t, d), F32),
        compiler_params=_cparams(("parallel",)),
        name="attn_out_proj",
    )(attn, gate, w, x2, final_g)


def _kv_kernel(x_ref, g_ref, wk_ref, wv_ref, aug_ref, k_ref, vt_ref):
    h = _rms(x_ref[...], g_ref[...]).astype(BF16)
    hn = DIFF_HEADS
    for n0 in range(0, wk_ref.shape[1], N_CHUNK):
        kc = _dot(h, wk_ref[:, n0:n0 + N_CHUNK])
        for j in range(N_CHUNK // QK_PAD):
            idx = n0 // QK_PAD + j
            k_ref[idx // 2, idx % 2] = (kc[:, j * QK_PAD:(j + 1) * QK_PAD]
                                        + aug_ref[idx // 2]).astype(BF16)
    for n0 in range(0, wv_ref.shape[1], N_CHUNK):
        vc = _dot(h, wv_ref[:, n0:n0 + N_CHUNK])
        for j in range(N_CHUNK // DIFF_V_DIM):
            head = n0 // DIFF_V_DIM + j
            vt_ref[head] = vc[:, j * DIFF_V_DIM:(j + 1) * DIFF_V_DIM].T.astype(BF16)
    del hn


def _kv_proj(x2, g, wk, wv, k_aug, batch, seq):
    d = x2.shape[1]
    nt = seq // ROW_TILE
    hn = DIFF_HEADS
    return pl.pallas_call(
        _kv_kernel,
        grid=(batch, nt),
        in_specs=[
            pl.BlockSpec((ROW_TILE, d), lambda b, i: (b * nt + i, 0)),
            pl.BlockSpec((1, d), lambda b, i: (0, 0)),
            pl.BlockSpec(wk.shape, lambda b, i: (0, 0)),
            pl.BlockSpec(wv.shape, lambda b, i: (0, 0)),
            pl.BlockSpec((hn, ROW_TILE, QK_PAD), lambda b, i: (0, i, 0)),
        ],
        out_specs=[
            pl.BlockSpec((None, hn, 2, ROW_TILE, QK_PAD), lambda b, i: (b, 0, 0, i, 0)),
            pl.BlockSpec((None, hn, DIFF_V_DIM, ROW_TILE), lambda b, i: (b, 0, 0, i)),
        ],
        out_shape=[
            jax.ShapeDtypeStruct((batch, hn, 2, seq, QK_PAD), BF16),
            jax.ShapeDtypeStruct((batch, hn, DIFF_V_DIM, seq), BF16),
        ],
        compiler_params=_cparams(("parallel", "parallel")),
        name="kv_proj",
    )(x2, g, wk, wv, k_aug)


def _b_in_kernel(x_ref, g_ref, wq_ref, wg_ref, aug_ref, q_ref, gate_ref):
    h = _rms(x_ref[...], g_ref[...]).astype(BF16)
    for n0 in range(0, wq_ref.shape[1], N_CHUNK):
        qc = _dot(h, wq_ref[:, n0:n0 + N_CHUNK]) * QK_SCALE
        for j in range(N_CHUNK // QK_PAD):
            idx = n0 // QK_PAD + j
            q_ref[idx // 2, idx % 2] = (qc[:, j * QK_PAD:(j + 1) * QK_PAD] + aug_ref[...]).astype(BF16)
    for n0 in range(0, wg_ref.shape[1], N_CHUNK):
        gate_ref[:, n0:n0 + N_CHUNK] = _dot(h, wg_ref[:, n0:n0 + N_CHUNK]).astype(BF16)


def _b_in_proj(x2, g, wq, wg, q_aug, batch, seq):
    d = x2.shape[1]
    nt = seq // ROW_TILE
    hn = DIFF_HEADS
    return pl.pallas_call(
        _b_in_kernel,
        grid=(batch, nt),
        in_specs=[
            pl.BlockSpec((ROW_TILE, d), lambda b, i: (b * nt + i, 0)),
            pl.BlockSpec((1, d), lambda b, i: (0, 0)),
            pl.BlockSpec(wq.shape, lambda b, i: (0, 0)),
            pl.BlockSpec(wg.shape, lambda b, i: (0, 0)),
            pl.BlockSpec((ROW_TILE, QK_PAD), lambda b, i: (i, 0)),
        ],
        out_specs=[
            pl.BlockSpec((None, hn, 2, ROW_TILE, QK_PAD), lambda b, i: (b, 0, 0, i, 0)),
            pl.BlockSpec((ROW_TILE, DIFF_WIDTH), lambda b, i: (b * nt + i, 0)),
        ],
        out_shape=[
            jax.ShapeDtypeStruct((batch, hn, 2, seq, QK_PAD), BF16),
            jax.ShapeDtypeStruct((batch * seq, DIFF_WIDTH), BF16),
        ],
        compiler_params=_cparams(("parallel", "parallel")),
        name="attn_in_proj",
    )(x2, g, wq, wg, q_aug)


def _attn_kernel(lam_ref, q_ref, k_ref, vt_ref, dbias_ref, subg_ref, o_ref,
                 s_a, s_b, p_a, p_b, p_last, m_scr, acc_scr, l_scr, al_scr, *, lambda_init):
    tq, tk = ATT_TQ, ATT_TK
    seq = q_ref.shape[2]
    assert (tk // tq) * 2 * ATT_UNROLL >= seq // tq
    streams = [(hh, i) for hh in range(ATT_HEADS) for i in range(2)]
    n_qt = seq // tq
    lf = lam_ref[...]
    lam = (jnp.exp(jnp.sum(lf[0:1, :] * lf[1:2, :], axis=-1, keepdims=True))
           - jnp.exp(jnp.sum(lf[2:3, :] * lf[3:4, :], axis=-1, keepdims=True)) + lambda_init)
    neg = jnp.float32(-1e30)

    def fold(x, op):
        return op(x.reshape(x.shape[0] // SUBLANES, SUBLANES, tq), axis=0)

    def load_q(qi):
        q0 = pl.multiple_of(qi * tq, tq)
        return [q_ref[hh, i, pl.ds(q0, tq), :] for hh, i in streams]

    def key_rows(t):
        return pl.ds(t * tk if isinstance(t, int) else pl.multiple_of(t * tk, tk), tk)

    def scores_into(qs, t, s_buf):
        mparts = []
        for n, (hh, i) in enumerate(streams):
            s = _dot_nt(k_ref[hh, i, key_rows(t), :], qs[n])
            s_buf[n] = s
            mparts.append(fold(s, jnp.max))
        return tuple(mparts)

    def stage_c(t, p_buf, alphas, accs):
        vts = [vt_ref[hh, :, key_rows(t)] for hh in range(ATT_HEADS)]
        return tuple(alphas[n] * accs[n] + _dot(vts[hh], p_buf[n])
                     for n, (hh, _) in enumerate(streams))

    def flush_previous(qi):
        q_old = jnp.maximum(qi - 1, 0)
        o0 = pl.multiple_of(q_old * tq, tq)
        slot = q_old % 2
        accs = stage_c(q_old // (tk // tq), p_last.at[slot],
                       tuple(al_scr[slot, n, 0:1, :] for n in range(len(streams))),
                       tuple(acc_scr[slot, n] for n in range(len(streams))))
        for hh in range(ATT_HEADS):
            l1 = jnp.sum(l_scr[slot, 2 * hh], axis=0, keepdims=True)
            l2 = jnp.sum(l_scr[slot, 2 * hh + 1], axis=0, keepdims=True)
            out_t = accs[2 * hh] / l1 - lam * (accs[2 * hh + 1] / l2)
            out = _rms(out_t.T, subg_ref[...]) * (1.0 - lambda_init)
            o_ref[pl.ds(o0, tq), hh * DIFF_V_DIM:(hh + 1) * DIFF_V_DIM] = out
        return tuple(jnp.zeros((DIFF_V_DIM, tq), F32) for _ in streams)

    def q_tile(qi, carry):
        n_full = qi // (tk // tq)
        which = qi % (tk // tq)
        qs = load_q(qi)

        def stage_a(t, s_buf):
            return scores_into(qs, t, s_buf)

        def stage_b(s_buf, p_buf, mparts, ms, ls, diag):
            new_m, new_l, alphas = [], [], []
            for n, (hh, _) in enumerate(streams):
                def scores(r0):
                    s = s_buf[n, r0:r0 + ATT_ROWS, :]
                    return s + dbias_ref[hh, which, r0:r0 + ATT_ROWS, :] if diag else s

                if diag:
                    mpart = fold(scores(0), jnp.max)
                    for r0 in range(ATT_ROWS, tk, ATT_ROWS):
                        mpart = jnp.maximum(mpart, fold(scores(r0), jnp.max))
                else:
                    mpart = mparts[n]
                m_new = jnp.maximum(ms[n], jnp.max(mpart, axis=0, keepdims=True))
                alpha = jnp.exp2(ms[n] - m_new)
                lsum = alpha * ls[n]
                for r0 in range(0, tk, ATT_ROWS):
                    p = jnp.exp2(scores(r0) - m_new)
                    lsum = lsum + fold(p, jnp.sum)
                    p_buf[n, r0:r0 + ATT_ROWS, :] = p.astype(BF16)
                new_l.append(lsum)
                new_m.append(m_new)
                alphas.append(alpha)
            return tuple(new_m), tuple(new_l), tuple(alphas)

        def step(j, bufs, st):
            s_cur, s_nxt, p_cur, p_prv = bufs
            mparts, ms, ls, alphas, accs = st
            mparts_next = stage_a(j + 1, s_nxt)
            ms, ls, alphas_new = stage_b(s_cur, p_cur, mparts, ms, ls, False)
            accs = flush_previous(qi) if j == 0 else stage_c(j - 1, p_prv, alphas, accs)
            return mparts_next, ms, ls, alphas_new, accs

        def steps(j0, j1, st):
            for j in range(j0, j1):
                st = step(j, odd if j % 2 else even, st)
            return st

        def finish(v, st):
            s_cur, _, _, p_prv = odd if v % 2 else even
            _, ms, ls, alphas, accs = st
            accs = flush_previous(qi) if v == 0 else stage_c(v - 1, p_prv, alphas, accs)
            slot = qi % 2
            _, ls, alphas_new = stage_b(s_cur, p_last.at[slot], None, ms, ls, True)
            nxt = scores_into(load_q(jnp.minimum(qi + 1, n_qt - 1)), 0, s_a)
            for n in range(len(streams)):
                m_scr[n] = nxt[n]
                acc_scr[slot, n] = accs[n]
                l_scr[slot, n] = ls[n]
                al_scr[slot, n] = jnp.broadcast_to(alphas_new[n], (SUBLANES, tq))

        even = (s_a, s_b, p_a, p_b)
        odd = (s_b, s_a, p_b, p_a)
        init = (tuple(m_scr[n] for n in range(len(streams))),
                tuple(jnp.full((1, tq), neg, F32) for _ in streams),
                tuple(jnp.zeros((SUBLANES, tq), F32) for _ in streams),
                tuple(jnp.ones((1, tq), F32) for _ in streams),
                tuple(jnp.zeros((DIFF_V_DIM, tq), F32) for _ in streams))
        st = lax.fori_loop(0, n_full // ATT_UNROLL, lambda _, c: steps(0, ATT_UNROLL, c), init)
        for v in range(n_qt // (tk // tq)):
            @pl.when(n_full == v)
            def _(v=v):
                finish(v, steps(v - v % ATT_UNROLL if v >= ATT_UNROLL else 0, v, st))

        return carry

    first = scores_into(load_q(0), 0, s_a)
    for n in range(len(streams)):
        m_scr[n] = first[n]
        acc_scr[0, n] = jnp.zeros((DIFF_V_DIM, tq), F32)
        l_scr[0, n] = jnp.ones((SUBLANES, tq), F32)
        al_scr[0, n] = jnp.ones((SUBLANES, tq), F32)
    p_last[0] = jnp.zeros(p_last.shape[1:], BF16)
    lax.fori_loop(0, n_qt, q_tile, 0)
    flush_previous(n_qt)


def _attention(lam_qk, q, k, vt, dbias, sub_g, lambda_init):
    batch, hn, _, seq, _ = q.shape
    return pl.pallas_call(
        functools.partial(_attn_kernel, lambda_init=lambda_init),
        grid=(batch, hn // ATT_HEADS),
        in_specs=[
            pl.BlockSpec(lam_qk.shape, lambda b, h: (0, 0)),
            pl.BlockSpec((None, ATT_HEADS, 2, seq, QK_PAD), lambda b, h: (b, h, 0, 0, 0)),
            pl.BlockSpec((None, ATT_HEADS, 2, seq, QK_PAD), lambda b, h: (b, h, 0, 0, 0)),
            pl.BlockSpec((None, ATT_HEADS, DIFF_V_DIM, seq), lambda b, h: (b, h, 0, 0)),
            pl.BlockSpec((ATT_HEADS, ATT_TK // ATT_TQ, ATT_TK, ATT_TQ), lambda b, h: (h, 0, 0, 0)),
            pl.BlockSpec((1, DIFF_V_DIM), lambda b, h: (0, 0)),
        ],
        out_specs=pl.BlockSpec((None, seq, ATT_HEADS * DIFF_V_DIM), lambda b, h: (b, 0, h)),
        out_shape=jax.ShapeDtypeStruct((batch, seq, DIFF_WIDTH), F32),
        scratch_shapes=[
            pltpu.VMEM((2 * ATT_HEADS, ATT_TK, ATT_TQ), F32),
            pltpu.VMEM((2 * ATT_HEADS, ATT_TK, ATT_TQ), F32),
            pltpu.VMEM((2 * ATT_HEADS, ATT_TK, ATT_TQ), BF16),
            pltpu.VMEM((2 * ATT_HEADS, ATT_TK, ATT_TQ), BF16),
            pltpu.VMEM((2, 2 * ATT_HEADS, ATT_TK, ATT_TQ), BF16),
            pltpu.VMEM((2 * ATT_HEADS, SUBLANES, ATT_TQ), F32),
            pltpu.VMEM((2, 2 * ATT_HEADS, DIFF_V_DIM, ATT_TQ), F32),
            pltpu.VMEM((2, 2 * ATT_HEADS, SUBLANES, ATT_TQ), F32),
            pltpu.VMEM((2, 2 * ATT_HEADS, SUBLANES, ATT_TQ), F32),
        ],
        compiler_params=_cparams(("parallel", "parallel")),
        name="diff_attention",
    )(lam_qk, q, k, vt, dbias, sub_g)


def _np_split3(a):
    a = a.astype(np.float32)
    a1 = a.astype(ml_dtypes.bfloat16).astype(np.float32)
    r1 = a - a1
    a2 = r1.astype(ml_dtypes.bfloat16).astype(np.float32)
    a3 = (r1 - a2).astype(ml_dtypes.bfloat16).astype(np.float32)
    return a1, a2, a3


def _alibi_tables(seq):
    hn = DIFF_HEADS
    slopes = np.float32(2.0) ** (-np.float32(ALIBI_MAX_EXP) * np.arange(1, hn + 1, dtype=np.float32)
                                 / np.float32(hn))
    slopes = (slopes.astype(np.float64) * LOG2E).astype(np.float32)
    pos = np.arange(seq, dtype=np.float32)
    kp = _np_split3(slopes[:, None] * pos[None, :])
    sl = _np_split3(np.broadcast_to(slopes[:, None], (hn, seq)))
    pos_hi = np.floor(pos / 256.0) * 256.0
    pos_lo = pos - pos_hi
    ones = np.ones((seq,), np.float32)
    k_cols = list(kp)
    q_cols = [ones, ones, ones]
    for t in sl:
        k_cols += [-t, -t]
        q_cols += [pos_hi, pos_lo]
    lane_pad = (DIFF_HEAD_DIM, QK_PAD - DIFF_HEAD_DIM - len(k_cols))
    k_aug = jnp.pad(jnp.asarray(np.stack(k_cols, axis=-1)), ((0, 0), (0, 0), lane_pad))
    q_aug = jnp.pad(jnp.asarray(np.stack(q_cols, axis=-1)), ((0, 0), lane_pad))
    kk = np.arange(ATT_TK)[None, :, None]
    qq = np.arange(ATT_TQ)[None, None, :] + ATT_TQ * np.arange(ATT_TK // ATT_TQ)[:, None, None]
    allowed = (kk // CHUNK) <= (qq // CHUNK)
    fix = np.where(kk > qq, -2.0 * (kk - qq), 0.0).astype(np.float32)
    dbias = np.where(allowed[None], slopes[:, None, None, None] * fix[None], -np.inf).astype(np.float32)
    return k_aug, q_aug, jnp.asarray(dbias)


def _pad_heads(w, n_blocks):
    d = w.shape[0]
    w = w.reshape(d, n_blocks, DIFF_HEAD_DIM)
    w = jnp.concatenate([w, jnp.zeros_like(w)], axis=-1)
    return w.reshape(d, n_blocks * QK_PAD).astype(BF16)


def kernel(x, a_norm_g, a_w_in, a_conv_w, a_conv_b, a_dt_bias, a_a_log, a_d_skip, a_gate_norm_g,
           a_w_out, kv_norm_g, w_kv, b_norm_g, b_w_in, b_lambda, b_sub_g, b_w_out, final_norm_g):
    batch, seq, d = x.shape
    t = batch * seq
    hn = SSM_HEADS
    x2 = x.reshape(t, d)

    for i in range(N_A_LAYERS):
        w_in = a_w_in[i]
        n_main = SSM_D_INNER + SSM_CONV_DIM
        w_main = jnp.concatenate([0.5 * w_in[:, :SSM_D_INNER], w_in[:, SSM_D_INNER:n_main]], axis=1).astype(BF16)
        w_dt = jnp.pad(w_in[:, n_main:], ((0, 0), (0, LANES - hn))).astype(BF16)
        zx, dt_raw = _a_in_proj(x2, a_norm_g[i][None], w_main, w_dt, 0.5 * a_conv_w[i],
                                0.5 * a_conv_b[i][None], seq)
        dt_row = dt_raw[:, :hn].reshape(batch, seq, hn).transpose(0, 2, 1)
        hp_row = jnp.zeros((SUBLANES, hn), F32).at[0].set(a_dt_bias[i]).at[1].set(a_a_log[i]).at[2].set(a_d_skip[i])
        hp_col = hp_row.T
        y = _ssd(zx, dt_raw, dt_row, hp_row, hp_col, a_gate_norm_g[i][None], batch, seq)
        x2 = _out_proj(y, a_w_out[i].astype(BF16), x2)

    k_aug, q_aug, dbias = _alibi_tables(seq)
    n_k = DIFF_HEADS * 2 * DIFF_HEAD_DIM
    wk = _pad_heads(w_kv[:, :n_k], DIFF_HEADS * 2)
    wv = w_kv[:, n_k:].astype(BF16)
    k, vt = _kv_proj(x2, kv_norm_g[None], wk, wv, k_aug, batch, seq)

    for j in range(N_B_LAYERS):
        layer = N_A_LAYERS + j
        lambda_init = 0.8 - 0.6 * math.exp(-0.3 * layer)
        wq = _pad_heads(b_w_in[j][:, :DIFF_WIDTH], DIFF_HEADS * 2)
        wg = b_w_in[j][:, DIFF_WIDTH:].astype(BF16)
        q, gate = _b_in_proj(x2, b_norm_g[j][None], wq, wg, q_aug, batch, seq)
        attn = _attention(b_lambda[j], q, k, vt, dbias, b_sub_g[j][None], lambda_init)
        x2 = _gated_out_proj(attn.reshape(t, DIFF_WIDTH), gate, b_w_out[j].astype(BF16), x2,
                             final_norm_g[None], final_norm=(j == N_B_LAYERS - 1))
    return x2.reshape(batch, seq, d)
```

```python
import functools
import math

import jax
import jax.numpy as jnp
import ml_dtypes
import numpy as np
from jax import lax
from jax.experimental import pallas as pl
from jax.experimental.pallas import tpu as pltpu

F32 = jnp.float32
BF16 = jnp.bfloat16

D_MODEL = 1024
DEPTH = 4
CHUNK = 64
N_B_LAYERS = DEPTH // 2
N_A_LAYERS = DEPTH - N_B_LAYERS
SSM_D_INNER = 2 * D_MODEL
SSM_HEAD_DIM = 64
SSM_HEADS = SSM_D_INNER // SSM_HEAD_DIM
SSM_GROUPS = 8
SSM_HPG = SSM_HEADS // SSM_GROUPS
SSM_STATE = 128
SSM_CONV = 4
SSM_GN = SSM_GROUPS * SSM_STATE
SSM_CONV_DIM = SSM_D_INNER + 2 * SSM_GN
DIFF_HEAD_DIM = 64
DIFF_V_DIM = 2 * DIFF_HEAD_DIM
DIFF_HEADS = D_MODEL // DIFF_V_DIM
DIFF_WIDTH = DIFF_HEADS * DIFF_V_DIM
ALIBI_MAX_EXP = 8.0
EPS = 1e-5
LOG2E = math.log2(math.e)
QK_SCALE = LOG2E / math.sqrt(DIFF_HEAD_DIM)

LANES = 128
SUBLANES = 8
VMEM_LIMIT = 56 * 1024 * 1024

ROW_TILE = 512
SSD_Q = 256
ATT_TQ = 256
ATT_TK = 512
ATT_UNROLL = 4
ATT_HEADS = 2
ATT_ROWS = 64
N_CHUNK = 256
CONV_GROUPS = 8
QK_PAD = 2 * DIFF_HEAD_DIM


def _cparams(sem):
    return pltpu.CompilerParams(dimension_semantics=sem, vmem_limit_bytes=VMEM_LIMIT)


def _rms(x, g):
    return x * lax.rsqrt(jnp.mean(x * x, axis=-1, keepdims=True) + EPS) * g


def _silu_of_half(h):
    return h * jnp.tanh(h) + h


def _silu(x):
    return _silu_of_half(0.5 * x)


def _softplus(x):
    return jnp.maximum(x, 0.0) + jnp.log1p(jnp.exp(-jnp.abs(x)))


def _trunc_bf16(a):
    bits = lax.bitcast_convert_type(a, jnp.int32) & jnp.int32(-65536)
    return lax.bitcast_convert_type(bits, F32)


def _split3(a):
    a1 = _trunc_bf16(a)
    r1 = a - a1
    a2 = _trunc_bf16(r1)
    return a1.astype(BF16), a2.astype(BF16), (r1 - a2).astype(BF16)


def _dot(a, b):
    return jnp.dot(a, b, preferred_element_type=F32)


def _dot_nt(a, b):
    return lax.dot_general(a, b, (((1,), (1,)), ((), ())), preferred_element_type=F32)


def _dot_tn(a, b):
    return lax.dot_general(a, b, (((0,), (0,)), ((), ())), preferred_element_type=F32)


def _a_in_kernel(x_ref, g_ref, w_ref, wdt_ref, cw_ref, cb_ref, o_ref, dt_ref, tail_ref,
                 *, tiles_per_seq):
    rows = x_ref.shape[0]
    di = SSM_D_INNER
    n_main = o_ref.shape[-1]

    @pl.when(pl.program_id(0) % tiles_per_seq == 0)
    def _():
        tail_ref[...] = jnp.zeros_like(tail_ref)

    h = _rms(x_ref[...], g_ref[...]).astype(BF16)
    sub = lax.broadcasted_iota(jnp.int32, (1, SUBLANES, N_CHUNK), 1)
    z_chunks = list(range(0, di, N_CHUNK))
    conv_chunks = list(range(di, n_main, N_CHUNK))
    order = []
    while z_chunks or conv_chunks:
        order += conv_chunks[:2] + z_chunks[:1]
        conv_chunks, z_chunks = conv_chunks[2:], z_chunks[1:]
    nxt = _dot(h, w_ref[:, order[0]:order[0] + N_CHUNK])
    for idx, n0 in enumerate(order):
        a = nxt
        if idx + 1 < len(order):
            nxt = _dot(h, w_ref[:, order[idx + 1]:order[idx + 1] + N_CHUNK])
        if n0 < di:
            o_ref[:, n0:n0 + N_CHUNK] = _silu_of_half(a).astype(BF16)
            continue
        c0 = n0 - di
        tail = tail_ref[:, c0:c0 + N_CHUNK][None]
        tail_ref[:, c0:c0 + N_CHUNK] = a[rows - SUBLANES:rows, :]
        a3 = a.reshape(rows // SUBLANES, SUBLANES, N_CHUNK)
        for g0 in range(0, rows // SUBLANES, CONV_GROUPS):
            rot = a3[g0:g0 + CONV_GROUPS]
            prev = tail if g0 == 0 else a3[g0 - 1:g0]
            acc = cb_ref[:, c0:c0 + N_CHUNK] + cw_ref[SSM_CONV - 1:SSM_CONV, c0:c0 + N_CHUNK] * rot
            for j in range(1, SSM_CONV):
                rot = pltpu.roll(rot, 1, axis=1)
                prev = pltpu.roll(prev, 1, axis=1)
                above = jnp.concatenate([prev, rot[:-1]], axis=0)
                k = SSM_CONV - 1 - j
                acc = acc + cw_ref[k:k + 1, c0:c0 + N_CHUNK] * jnp.where(sub >= j, rot, above)
            r0 = g0 * SUBLANES
            o_ref[r0:r0 + CONV_GROUPS * SUBLANES, n0:n0 + N_CHUNK] = (
                _silu_of_half(acc).reshape(CONV_GROUPS * SUBLANES, N_CHUNK).astype(BF16))
    dt_ref[...] = _dot(h, wdt_ref[...])


def _a_in_proj(x2, g, w_main, w_dt, conv_w, conv_b, seq):
    t, d = x2.shape
    n_main = w_main.shape[1]
    return pl.pallas_call(
        functools.partial(_a_in_kernel, tiles_per_seq=seq // ROW_TILE),
        grid=(t // ROW_TILE,),
        in_specs=[
            pl.BlockSpec((ROW_TILE, d), lambda i: (i, 0)),
            pl.BlockSpec((1, d), lambda i: (0, 0)),
            pl.BlockSpec((d, n_main), lambda i: (0, 0)),
            pl.BlockSpec((d, LANES), lambda i: (0, 0)),
            pl.BlockSpec((SSM_CONV, SSM_CONV_DIM), lambda i: (0, 0)),
            pl.BlockSpec((1, SSM_CONV_DIM), lambda i: (0, 0)),
        ],
        out_specs=[
            pl.BlockSpec((ROW_TILE, n_main), lambda i: (i, 0)),
            pl.BlockSpec((ROW_TILE, LANES), lambda i: (i, 0)),
        ],
        out_shape=[
            jax.ShapeDtypeStruct((t, n_main), BF16),
            jax.ShapeDtypeStruct((t, LANES), F32),
        ],
        scratch_shapes=[pltpu.VMEM((SUBLANES, SSM_CONV_DIM), F32)],
        compiler_params=_cparams(("arbitrary",)),
        name="mamba_in_proj",
    )(x2, g, w_main, w_dt, conv_w, conv_b)


def _ssd_kernel(zx_ref, dtc_ref, dtr_ref, hp_row_ref, hp_col_ref, gng_ref, o_ref, state_ref):
    q = zx_ref.shape[0]
    g_n, r_n, p_n, n_n = SSM_GROUPS, SSM_HPG, SSM_HEAD_DIM, SSM_STATE
    di = SSM_D_INNER
    hn = SSM_HEADS
    gw = r_n * p_n

    @pl.when(pl.program_id(1) == 0)
    def _():
        state_ref[...] = jnp.zeros_like(state_ref)

    dtb_row, alog_row, dskip_row = hp_row_ref[0:1, :], hp_row_ref[1:2, :], hp_row_ref[2:3, :]
    dtb_col, alog_col = hp_col_ref[:, 0:1], hp_col_ref[:, 1:2]
    dt_c = _softplus(dtc_ref[:, 0:hn] + dtb_row)
    adt_c = dt_c * (-jnp.exp(alog_row))
    dt_r = _softplus(dtr_ref[...] + dtb_col)
    adt_r = dt_r * (-jnp.exp(alog_col))
    ri = lax.broadcasted_iota(jnp.int32, (q, q), 0)
    ci = lax.broadcasted_iota(jnp.int32, (q, q), 1)
    causal = ri >= ci
    tril = jnp.where(causal, 1.0, 0.0).astype(BF16)
    triu = jnp.where(ri <= ci, 1.0, 0.0).astype(BF16)
    acs_c = sum(_dot(tril, part) for part in _split3(adt_c))
    acs_r = sum(_dot(part, triu) for part in _split3(adt_r))
    tot_row = acs_c[q - 1:q, :]
    eacs_c = jnp.exp(acs_c)
    toend_c = jnp.exp(tot_row - acs_c) * dt_c
    etot_row = jnp.exp(tot_row)
    acs2_c = acs_c * LOG2E
    acs2_r = acs_r * LOG2E

    lane_head = lax.broadcasted_iota(jnp.int32, (hn, gw), 1) // p_n
    row_head = lax.broadcasted_iota(jnp.int32, (hn, gw), 0)
    dt16 = dt_c.astype(BF16)
    toend16 = toend_c.astype(BF16)
    eacs_hi = _trunc_bf16(eacs_c)
    eacs_lo = (eacs_c - eacs_hi).astype(BF16)
    eacs_hi = eacs_hi.astype(BF16)
    lane_head_row = lane_head[0:1, :]
    head_mask16 = [jnp.where(lane_head_row == r, 1.0, 0.0).astype(BF16) for r in range(r_n)]

    def per_head_row(row, g):
        out = jnp.broadcast_to(row[:, g * r_n:g * r_n + 1], (1, gw))
        for r in range(1, r_n):
            hval = jnp.broadcast_to(row[:, g * r_n + r:g * r_n + r + 1], (1, gw))
            out = jnp.where(lane_head_row == r, hval, out)
        return out

    for g in range(g_n):
        spread = jnp.where(row_head == lane_head + g * r_n, 1.0, 0.0).astype(BF16)
        xs16 = zx_ref[:, di + g * gw:di + (g + 1) * gw]
        bm16 = zx_ref[:, 2 * di + g * n_n:2 * di + (g + 1) * n_n]
        cm16 = zx_ref[:, 2 * di + SSM_GN + g * n_n:2 * di + SSM_GN + (g + 1) * n_n]
        xs = xs16.astype(F32)
        cb = _dot_nt(cm16, bm16)
        xdt16 = (xs * _dot(dt16, spread)).astype(BF16)
        st = state_ref[g]
        y = _dot(cm16, st.astype(BF16)) * (_dot(eacs_hi, spread) + _dot(eacs_lo, spread))
        for r in range(r_n):
            h = g * r_n + r
            seg2 = jnp.broadcast_to(acs2_c[:, h:h + 1], (q, q)) - acs2_r[h:h + 1, :]
            lmat = (cb * jnp.where(causal, jnp.exp2(seg2), 0.0)).astype(BF16)
            y = y + _dot(lmat, xdt16 * head_mask16[r])
        y = y + per_head_row(dskip_row, g) * xs
        xend16 = (xs * _dot(toend16, spread)).astype(BF16)
        state_ref[g] = st * per_head_row(etot_row, g) + _dot_tn(bm16, xend16)
        u = y * zx_ref[:, g * gw:(g + 1) * gw].astype(F32)
        u = u * lax.rsqrt(jnp.mean(u * u, axis=-1, keepdims=True) + EPS)
        o_ref[:, g * gw:(g + 1) * gw] = (u * gng_ref[:, g * gw:(g + 1) * gw]).astype(BF16)


def _ssd(zx, dt_col, dt_row, hp_row, hp_col, gate_g, batch, seq):
    q = SSD_Q
    nc = seq // q
    n_main = zx.shape[1]
    hn = SSM_HEADS
    return pl.pallas_call(
        _ssd_kernel,
        grid=(batch, nc),
        in_specs=[
            pl.BlockSpec((q, n_main), lambda b, c: (b * nc + c, 0)),
            pl.BlockSpec((q, LANES), lambda b, c: (b * nc + c, 0)),
            pl.BlockSpec((None, hn, q), lambda b, c: (b, 0, c)),
            pl.BlockSpec((SUBLANES, hn), lambda b, c: (0, 0)),
            pl.BlockSpec((hn, SUBLANES), lambda b, c: (0, 0)),
            pl.BlockSpec((1, SSM_D_INNER), lambda b, c: (0, 0)),
        ],
        out_specs=pl.BlockSpec((q, SSM_D_INNER), lambda b, c: (b * nc + c, 0)),
        out_shape=jax.ShapeDtypeStruct((batch * seq, SSM_D_INNER), BF16),
        scratch_shapes=[
            pltpu.VMEM((SSM_GROUPS, SSM_STATE, SSM_HPG * SSM_HEAD_DIM), F32),
        ],
        compiler_params=_cparams(("parallel", "arbitrary")),
        name="mamba_ssd",
    )(zx, dt_col, dt_row, hp_row, hp_col, gate_g)


def _out_proj_kernel(y_ref, w_ref, x_ref, o_ref):
    o_ref[...] = x_ref[...] + _dot(y_ref[...], w_ref[...])


def _out_proj(y, w, x2):
    t, k = y.shape
    d = w.shape[1]
    return pl.pallas_call(
        _out_proj_kernel,
        grid=(t // ROW_TILE,),
        in_specs=[
            pl.BlockSpec((ROW_TILE, k), lambda i: (i, 0)),
            pl.BlockSpec((k, d), lambda i: (0, 0)),
            pl.BlockSpec((ROW_TILE, d), lambda i: (i, 0)),
        ],
        out_specs=pl.BlockSpec((ROW_TILE, d), lambda i: (i, 0)),
        out_shape=jax.ShapeDtypeStruct((t, d), F32),
        compiler_params=_cparams(("parallel",)),
        name="mamba_out_proj",
    )(y, w, x2)


def _gated_out_proj_kernel(a_ref, gate_ref, w_ref, x_ref, fg_ref, o_ref, *, final_norm):
    gate = gate_ref[...].astype(F32)
    y = (a_ref[...] * _silu(gate)).astype(BF16)
    out = x_ref[...] + _dot(y, w_ref[...])
    if final_norm:
        out = _rms(out, fg_ref[...])
    o_ref[...] = out


def _gated_out_proj(attn, gate, w, x2, final_g, final_norm):
    t, k = attn.shape
    d = w.shape[1]
    return pl.pallas_call(
        functools.partial(_gated_out_proj_kernel, final_norm=final_norm),
        grid=(t // ROW_TILE,),
        in_specs=[
            pl.BlockSpec((ROW_TILE, k), lambda i: (i, 0)),
            pl.BlockSpec((ROW_TILE, k), lambda i: (i, 0)),
            pl.BlockSpec((k, d), lambda i: (0, 0)),
            pl.BlockSpec((ROW_TILE, d), lambda i: (i, 0)),
            pl.BlockSpec((1, d), lambda i: (0, 0)),
        ],
        out_specs=pl.BlockSpec((ROW_TILE, d), lambda i: (i, 0)),
        out_shape=jax.ShapeDtypeStruct((t, d), F32),
        compiler_params=_cparams(("parallel",)),
        name="attn_out_proj",
    )(attn, gate, w, x2, final_g)


def _kv_kernel(x_ref, g_ref, wk_ref, wv_ref, aug_ref, k_ref, vt_ref):
    h = _rms(x_ref[...], g_ref[...]).astype(BF16)
    hn = DIFF_HEADS
    for n0 in range(0, wk_ref.shape[1], N_CHUNK):
        kc = _dot(h, wk_ref[:, n0:n0 + N_CHUNK])
        for j in range(N_CHUNK // QK_PAD):
            idx = n0 // QK_PAD + j
            k_ref[idx // 2, idx % 2] = (kc[:, j * QK_PAD:(j + 1) * QK_PAD]
                                        + aug_ref[idx // 2]).astype(BF16)
    for n0 in range(0, wv_ref.shape[1], N_CHUNK):
        vc = _dot(h, wv_ref[:, n0:n0 + N_CHUNK])
        for j in range(N_CHUNK // DIFF_V_DIM):
            head = n0 // DIFF_V_DIM + j
            vt_ref[head] = vc[:, j * DIFF_V_DIM:(j + 1) * DIFF_V_DIM].T.astype(BF16)
    del hn


def _kv_proj(x2, g, wk, wv, k_aug, batch, seq):
    d = x2.shape[1]
    nt = seq // ROW_TILE
    hn = DIFF_HEADS
    return pl.pallas_call(
        _kv_kernel,
        grid=(batch, nt),
        in_specs=[
            pl.BlockSpec((ROW_TILE, d), lambda b, i: (b * nt + i, 0)),
            pl.BlockSpec((1, d), lambda b, i: (0, 0)),
            pl.BlockSpec(wk.shape, lambda b, i: (0, 0)),
            pl.BlockSpec(wv.shape, lambda b, i: (0, 0)),
            pl.BlockSpec((hn, ROW_TILE, QK_PAD), lambda b, i: (0, i, 0)),
        ],
        out_specs=[
            pl.BlockSpec((None, hn, 2, ROW_TILE, QK_PAD), lambda b, i: (b, 0, 0, i, 0)),
            pl.BlockSpec((None, hn, DIFF_V_DIM, ROW_TILE), lambda b, i: (b, 0, 0, i)),
        ],
        out_shape=[
            jax.ShapeDtypeStruct((batch, hn, 2, seq, QK_PAD), BF16),
            jax.ShapeDtypeStruct((batch, hn, DIFF_V_DIM, seq), BF16),
        ],
        compiler_params=_cparams(("parallel", "parallel")),
        name="kv_proj",
    )(x2, g, wk, wv, k_aug)


def _b_in_kernel(x_ref, g_ref, wq_ref, wg_ref, aug_ref, q_ref, gate_ref):
    h = _rms(x_ref[...], g_ref[...]).astype(BF16)
    for n0 in range(0, wq_ref.shape[1], N_CHUNK):
        qc = _dot(h, wq_ref[:, n0:n0 + N_CHUNK]) * QK_SCALE
        for j in range(N_CHUNK // QK_PAD):
            idx = n0 // QK_PAD + j
            q_ref[idx // 2, idx % 2] = (qc[:, j * QK_PAD:(j + 1) * QK_PAD] + aug_ref[...]).astype(BF16)
    for n0 in range(0, wg_ref.shape[1], N_CHUNK):
        gate_ref[:, n0:n0 + N_CHUNK] = _dot(h, wg_ref[:, n0:n0 + N_CHUNK]).astype(BF16)


def _b_in_proj(x2, g, wq, wg, q_aug, batch, seq):
    d = x2.shape[1]
    nt = seq // ROW_TILE
    hn = DIFF_HEADS
    return pl.pallas_call(
        _b_in_kernel,
        grid=(batch, nt),
        in_specs=[
            pl.BlockSpec((ROW_TILE, d), lambda b, i: (b * nt + i, 0)),
            pl.BlockSpec((1, d), lambda b, i: (0, 0)),
            pl.BlockSpec(wq.shape, lambda b, i: (0, 0)),
            pl.BlockSpec(wg.shape, lambda b, i: (0, 0)),
            pl.BlockSpec((ROW_TILE, QK_PAD), lambda b, i: (i, 0)),
        ],
        out_specs=[
            pl.BlockSpec((None, hn, 2, ROW_TILE, QK_PAD), lambda b, i: (b, 0, 0, i, 0)),
            pl.BlockSpec((ROW_TILE, DIFF_WIDTH), lambda b, i: (b * nt + i, 0)),
        ],
        out_shape=[
            jax.ShapeDtypeStruct((batch, hn, 2, seq, QK_PAD), BF16),
            jax.ShapeDtypeStruct((batch * seq, DIFF_WIDTH), BF16),
        ],
        compiler_params=_cparams(("parallel", "parallel")),
        name="attn_in_proj",
    )(x2, g, wq, wg, q_aug)


def _attn_kernel(lam_ref, q_ref, k_ref, vt_ref, dbias_ref, subg_ref, o_ref,
                 s_a, s_b, p_a, p_b, p_last, m_scr, acc_scr, l_scr, al_scr, *, lambda_init):
    tq, tk = ATT_TQ, ATT_TK
    seq = q_ref.shape[2]
    places = tk // tq
    assert places * 2 * ATT_UNROLL >= seq // tq
    streams = [(hh, i) for hh in range(ATT_HEADS) for i in range(2)]
    n_qt = seq // tq
    lf = lam_ref[...]
    lam = (jnp.exp(jnp.sum(lf[0:1, :] * lf[1:2, :], axis=-1, keepdims=True))
           - jnp.exp(jnp.sum(lf[2:3, :] * lf[3:4, :], axis=-1, keepdims=True)) + lambda_init)
    neg = jnp.float32(-1e30)

    def fold(x, op):
        return op(x.reshape(x.shape[0] // SUBLANES, SUBLANES, tq), axis=0)

    def load_q(qi):
        q0 = pl.multiple_of(qi * tq, tq)
        return [q_ref[hh, i, pl.ds(q0, tq), :] for hh, i in streams]

    def key_rows(t, rows=tk):
        return pl.ds(t * tk if isinstance(t, int) else pl.multiple_of(t * tk, tk), rows)

    def scores_into(qs, t, s_buf):
        mparts = []
        for n, (hh, i) in enumerate(streams):
            s = _dot_nt(k_ref[hh, i, key_rows(t), :], qs[n])
            s_buf[n] = s
            mparts.append(fold(s, jnp.max))
        return tuple(mparts)

    def stage_c(t, p_buf, alphas, accs, rows=tk):
        vts = [vt_ref[hh, :, key_rows(t, rows)] for hh in range(ATT_HEADS)]
        return tuple(alphas[n] * accs[n] + _dot(vts[hh], p_buf[n, 0:rows, :])
                     for n, (hh, _) in enumerate(streams))

    def flush_previous(qi, old_place):
        q_old = jnp.maximum(qi - 1, 0)
        o0 = pl.multiple_of(q_old * tq, tq)
        slot = q_old % 2
        accs = stage_c(q_old // (tk // tq), p_last.at[slot],
                       tuple(al_scr[slot, n, 0:1, :] for n in range(len(streams))),
                       tuple(acc_scr[slot, n] for n in range(len(streams))),
                       rows=(old_place + 1) * tq)
        for hh in range(ATT_HEADS):
            l1 = jnp.sum(l_scr[slot, 2 * hh], axis=0, keepdims=True)
            l2 = jnp.sum(l_scr[slot, 2 * hh + 1], axis=0, keepdims=True)
            out_t = accs[2 * hh] / l1 - lam * (accs[2 * hh + 1] / l2)
            out = _rms(out_t.T, subg_ref[...]) * (1.0 - lambda_init)
            o_ref[pl.ds(o0, tq), hh * DIFF_V_DIM:(hh + 1) * DIFF_V_DIM] = out
        return tuple(jnp.zeros((DIFF_V_DIM, tq), F32) for _ in streams)

    def q_tile(qi, carry):
        n_full = qi // places
        which = qi % places
        qs = load_q(qi)

        def stage_a(t, s_buf):
            return scores_into(qs, t, s_buf)

        def stage_b(s_buf, p_buf, mparts, ms, ls, place=None):
            diag = place is not None
            rows = (place + 1) * tq if diag else tk
            new_m, new_l, alphas = [], [], []
            for n, (hh, _) in enumerate(streams):
                def scores(r0):
                    s = s_buf[n, r0:r0 + ATT_ROWS, :]
                    return s + dbias_ref[hh, place, r0:r0 + ATT_ROWS, :] if diag else s

                if diag:
                    mpart = fold(scores(0), jnp.max)
                    for r0 in range(ATT_ROWS, rows, ATT_ROWS):
                        mpart = jnp.maximum(mpart, fold(scores(r0), jnp.max))
                else:
                    mpart = mparts[n]
                m_new = jnp.maximum(ms[n], jnp.max(mpart, axis=0, keepdims=True))
                alpha = jnp.exp2(ms[n] - m_new)
                lsum = alpha * ls[n]
                for r0 in range(0, rows, ATT_ROWS):
                    p = jnp.exp2(scores(r0) - m_new)
                    lsum = lsum + fold(p, jnp.sum)
                    p_buf[n, r0:r0 + ATT_ROWS, :] = p.astype(BF16)
                new_l.append(lsum)
                new_m.append(m_new)
                alphas.append(alpha)
            return tuple(new_m), tuple(new_l), tuple(alphas)

        def step(j, bufs, st, place):
            s_cur, s_nxt, p_cur, p_prv = bufs
            mparts, ms, ls, alphas, accs = st
            mparts_next = stage_a(j + 1, s_nxt)
            ms, ls, alphas_new = stage_b(s_cur, p_cur, mparts, ms, ls)
            accs = (flush_previous(qi, (place - 1) % places) if j == 0
                    else stage_c(j - 1, p_prv, alphas, accs))
            return mparts_next, ms, ls, alphas_new, accs

        def steps(j0, j1, st, place):
            for j in range(j0, j1):
                st = step(j, odd if j % 2 else even, st, place)
            return st

        def finish(v, st, place):
            s_cur, _, _, p_prv = odd if v % 2 else even
            _, ms, ls, alphas, accs = st
            accs = (flush_previous(qi, (place - 1) % places) if v == 0
                    else stage_c(v - 1, p_prv, alphas, accs))
            slot = qi % 2
            _, ls, alphas_new = stage_b(s_cur, p_last.at[slot], None, ms, ls, place)
            nxt = scores_into(load_q(jnp.minimum(qi + 1, n_qt - 1)), 0, s_a)
            for n in range(len(streams)):
                m_scr[n] = nxt[n]
                acc_scr[slot, n] = accs[n]
                l_scr[slot, n] = ls[n]
                al_scr[slot, n] = jnp.broadcast_to(alphas_new[n], (SUBLANES, tq))

        even = (s_a, s_b, p_a, p_b)
        odd = (s_b, s_a, p_b, p_a)
        init = (tuple(m_scr[n] for n in range(len(streams))),
                tuple(jnp.full((1, tq), neg, F32) for _ in streams),
                tuple(jnp.zeros((SUBLANES, tq), F32) for _ in streams),
                tuple(jnp.ones((1, tq), F32) for _ in streams),
                tuple(jnp.zeros((DIFF_V_DIM, tq), F32) for _ in streams))
        for place in range(places):
            @pl.when(which == place)
            def _(place=place):
                st = lax.fori_loop(0, n_full // ATT_UNROLL,
                                   lambda _, c: steps(0, ATT_UNROLL, c, place), init)
                for v in range(n_qt // places):
                    @pl.when(n_full == v)
                    def _(v=v):
                        lo = v - v % ATT_UNROLL if v >= ATT_UNROLL else 0
                        finish(v, steps(lo, v, st, place), place)

        return carry

    first = scores_into(load_q(0), 0, s_a)
    for n in range(len(streams)):
        m_scr[n] = first[n]
        acc_scr[0, n] = jnp.zeros((DIFF_V_DIM, tq), F32)
        l_scr[0, n] = jnp.ones((SUBLANES, tq), F32)
        al_scr[0, n] = jnp.ones((SUBLANES, tq), F32)
    p_last[0] = jnp.zeros(p_last.shape[1:], BF16)
    lax.fori_loop(0, n_qt, q_tile, 0)
    flush_previous(n_qt, (n_qt - 1) % places)


def _attention(lam_qk, q, k, vt, dbias, sub_g, lambda_init):
    batch, hn, _, seq, _ = q.shape
    return pl.pallas_call(
        functools.partial(_attn_kernel, lambda_init=lambda_init),
        grid=(batch, hn // ATT_HEADS),
        in_specs=[
            pl.BlockSpec(lam_qk.shape, lambda b, h: (0, 0)),
            pl.BlockSpec((None, ATT_HEADS, 2, seq, QK_PAD), lambda b, h: (b, h, 0, 0, 0)),
            pl.BlockSpec((None, ATT_HEADS, 2, seq, QK_PAD), lambda b, h: (b, h, 0, 0, 0)),
            pl.BlockSpec((None, ATT_HEADS, DIFF_V_DIM, seq), lambda b, h: (b, h, 0, 0)),
            pl.BlockSpec((ATT_HEADS, ATT_TK // ATT_TQ, ATT_TK, ATT_TQ), lambda b, h: (h, 0, 0, 0)),
            pl.BlockSpec((1, DIFF_V_DIM), lambda b, h: (0, 0)),
        ],
        out_specs=pl.BlockSpec((None, seq, ATT_HEADS * DIFF_V_DIM), lambda b, h: (b, 0, h)),
        out_shape=jax.ShapeDtypeStruct((batch, seq, DIFF_WIDTH), F32),
        scratch_shapes=[
            pltpu.VMEM((2 * ATT_HEADS, ATT_TK, ATT_TQ), F32),
            pltpu.VMEM((2 * ATT_HEADS, ATT_TK, ATT_TQ), F32),
            pltpu.VMEM((2 * ATT_HEADS, ATT_TK, ATT_TQ), BF16),
            pltpu.VMEM((2 * ATT_HEADS, ATT_TK, ATT_TQ), BF16),
            pltpu.VMEM((2, 2 * ATT_HEADS, ATT_TK, ATT_TQ), BF16),
            pltpu.VMEM((2 * ATT_HEADS, SUBLANES, ATT_TQ), F32),
            pltpu.VMEM((2, 2 * ATT_HEADS, DIFF_V_DIM, ATT_TQ), F32),
            pltpu.VMEM((2, 2 * ATT_HEADS, SUBLANES, ATT_TQ), F32),
            pltpu.VMEM((2, 2 * ATT_HEADS, SUBLANES, ATT_TQ), F32),
        ],
        compiler_params=_cparams(("parallel", "parallel")),
        name="diff_attention",
    )(lam_qk, q, k, vt, dbias, sub_g)


def _np_split3(a):
    a = a.astype(np.float32)
    a1 = a.astype(ml_dtypes.bfloat16).astype(np.float32)
    r1 = a - a1
    a2 = r1.astype(ml_dtypes.bfloat16).astype(np.float32)
    a3 = (r1 - a2).astype(ml_dtypes.bfloat16).astype(np.float32)
    return a1, a2, a3


def _alibi_tables(seq):
    hn = DIFF_HEADS
    slopes = np.float32(2.0) ** (-np.float32(ALIBI_MAX_EXP) * np.arange(1, hn + 1, dtype=np.float32)
                                 / np.float32(hn))
    slopes = (slopes.astype(np.float64) * LOG2E).astype(np.float32)
    pos = np.arange(seq, dtype=np.float32)
    kp = _np_split3(slopes[:, None] * pos[None, :])
    sl = _np_split3(np.broadcast_to(slopes[:, None], (hn, seq)))
    pos_hi = np.floor(pos / 256.0) * 256.0
    pos_lo = pos - pos_hi
    ones = np.ones((seq,), np.float32)
    k_cols = list(kp)
    q_cols = [ones, ones, ones]
    for t in sl:
        k_cols += [-t, -t]
        q_cols += [pos_hi, pos_lo]
    lane_pad = (DIFF_HEAD_DIM, QK_PAD - DIFF_HEAD_DIM - len(k_cols))
    k_aug = jnp.pad(jnp.asarray(np.stack(k_cols, axis=-1)), ((0, 0), (0, 0), lane_pad))
    q_aug = jnp.pad(jnp.asarray(np.stack(q_cols, axis=-1)), ((0, 0), lane_pad))
    kk = np.arange(ATT_TK)[None, :, None]
    qq = np.arange(ATT_TQ)[None, None, :] + ATT_TQ * np.arange(ATT_TK // ATT_TQ)[:, None, None]
    allowed = (kk // CHUNK) <= (qq // CHUNK)
    fix = np.where(kk > qq, -2.0 * (kk - qq), 0.0).astype(np.float32)
    dbias = np.where(allowed[None], slopes[:, None, None, None] * fix[None], -np.inf).astype(np.float32)
    return k_aug, q_aug, jnp.asarray(dbias)


def _pad_heads(w, n_blocks):
    d = w.shape[0]
    w = w.reshape(d, n_blocks, DIFF_HEAD_DIM)
    w = jnp.concatenate([w, jnp.zeros_like(w)], axis=-1)
    return w.reshape(d, n_blocks * QK_PAD).astype(BF16)


def kernel(x, a_norm_g, a_w_in, a_conv_w, a_conv_b, a_dt_bias, a_a_log, a_d_skip, a_gate_norm_g,
           a_w_out, kv_norm_g, w_kv, b_norm_g, b_w_in, b_lambda, b_sub_g, b_w_out, final_norm_g):
    batch, seq, d = x.shape
    t = batch * seq
    hn = SSM_HEADS
    x2 = x.reshape(t, d)

    for i in range(N_A_LAYERS):
        w_in = a_w_in[i]
        n_main = SSM_D_INNER + SSM_CONV_DIM
        w_main = jnp.concatenate([0.5 * w_in[:, :SSM_D_INNER], w_in[:, SSM_D_INNER:n_main]], axis=1).astype(BF16)
        w_dt = jnp.pad(w_in[:, n_main:], ((0, 0), (0, LANES - hn))).astype(BF16)
        zx, dt_raw = _a_in_proj(x2, a_norm_g[i][None], w_main, w_dt, 0.5 * a_conv_w[i],
                                0.5 * a_conv_b[i][None], seq)
        dt_row = dt_raw[:, :hn].reshape(batch, seq, hn).transpose(0, 2, 1)
        hp_row = jnp.zeros((SUBLANES, hn), F32).at[0].set(a_dt_bias[i]).at[1].set(a_a_log[i]).at[2].set(a_d_skip[i])
        hp_col = hp_row.T
        y = _ssd(zx, dt_raw, dt_row, hp_row, hp_col, a_gate_norm_g[i][None], batch, seq)
        x2 = _out_proj(y, a_w_out[i].astype(BF16), x2)

    k_aug, q_aug, dbias = _alibi_tables(seq)
    n_k = DIFF_HEADS * 2 * DIFF_HEAD_DIM
    wk = _pad_heads(w_kv[:, :n_k], DIFF_HEADS * 2)
    wv = w_kv[:, n_k:].astype(BF16)
    k, vt = _kv_proj(x2, kv_norm_g[None], wk, wv, k_aug, batch, seq)

    for j in range(N_B_LAYERS):
        layer = N_A_LAYERS + j
        lambda_init = 0.8 - 0.6 * math.exp(-0.3 * layer)
        wq = _pad_heads(b_w_in[j][:, :DIFF_WIDTH], DIFF_HEADS * 2)
        wg = b_w_in[j][:, DIFF_WIDTH:].astype(BF16)
        q, gate = _b_in_proj(x2, b_norm_g[j][None], wq, wg, q_aug, batch, seq)
        attn = _attention(b_lambda[j], q, k, vt, dbias, b_sub_g[j][None], lambda_init)
        x2 = _gated_out_proj(attn.reshape(t, DIFF_WIDTH), gate, b_w_out[j].astype(BF16), x2,
                             final_norm_g[None], final_norm=(j == N_B_LAYERS - 1))
    return x2.reshape(batch, seq, d)
```

```python
import functools
import math

import jax
import jax.numpy as jnp
import ml_dtypes
import numpy as np
from jax import lax
from jax.experimental import pallas as pl
from jax.experimental.pallas import tpu as pltpu

F32 = jnp.float32
BF16 = jnp.bfloat16

D_MODEL = 1024
DEPTH = 4
CHUNK = 64
N_B_LAYERS = DEPTH // 2
N_A_LAYERS = DEPTH - N_B_LAYERS
SSM_D_INNER = 2 * D_MODEL
SSM_HEAD_DIM = 64
SSM_HEADS = SSM_D_INNER // SSM_HEAD_DIM
SSM_GROUPS = 8
SSM_HPG = SSM_HEADS // SSM_GROUPS
SSM_STATE = 128
SSM_CONV = 4
SSM_GN = SSM_GROUPS * SSM_STATE
SSM_CONV_DIM = SSM_D_INNER + 2 * SSM_GN
DIFF_HEAD_DIM = 64
DIFF_V_DIM = 2 * DIFF_HEAD_DIM
DIFF_HEADS = D_MODEL // DIFF_V_DIM
DIFF_WIDTH = DIFF_HEADS * DIFF_V_DIM
ALIBI_MAX_EXP = 8.0
EPS = 1e-5
LOG2E = math.log2(math.e)
QK_SCALE = LOG2E / math.sqrt(DIFF_HEAD_DIM)

LANES = 128
SUBLANES = 8
VMEM_LIMIT = 56 * 1024 * 1024

ROW_TILE = 512
SSD_Q = 256
ATT_TQ = 256
ATT_TK = 512
ATT_UNROLL = 4
ATT_HEADS = 2
ATT_ROWS = 64
N_CHUNK = 256
CONV_GROUPS = 8
QK_PAD = 2 * DIFF_HEAD_DIM


def _cparams(sem):
    return pltpu.CompilerParams(dimension_semantics=sem, vmem_limit_bytes=VMEM_LIMIT)


def _rms(x, g):
    return x * lax.rsqrt(jnp.mean(x * x, axis=-1, keepdims=True) + EPS) * g


def _silu_of_half(h):
    return h * jnp.tanh(h) + h


def _silu(x):
    return _silu_of_half(0.5 * x)


def _softplus(x):
    return jnp.maximum(x, 0.0) + jnp.log1p(jnp.exp(-jnp.abs(x)))


def _trunc_bf16(a):
    bits = lax.bitcast_convert_type(a, jnp.int32) & jnp.int32(-65536)
    return lax.bitcast_convert_type(bits, F32)


def _split3(a):
    a1 = _trunc_bf16(a)
    r1 = a - a1
    a2 = _trunc_bf16(r1)
    return a1.astype(BF16), a2.astype(BF16), (r1 - a2).astype(BF16)


def _dot(a, b):
    return jnp.dot(a, b, preferred_element_type=F32)


def _dot_nt(a, b):
    return lax.dot_general(a, b, (((1,), (1,)), ((), ())), preferred_element_type=F32)


def _dot_tn(a, b):
    return lax.dot_general(a, b, (((0,), (0,)), ((), ())), preferred_element_type=F32)


def _a_in_kernel(x_ref, g_ref, w_ref, wdt_ref, cw_ref, cb_ref, o_ref, dt_ref, tail_ref,
                 *, tiles_per_seq):
    rows = x_ref.shape[0]
    di = SSM_D_INNER
    n_main = o_ref.shape[-1]

    @pl.when(pl.program_id(0) % tiles_per_seq == 0)
    def _():
        tail_ref[...] = jnp.zeros_like(tail_ref)

    h = _rms(x_ref[...], g_ref[...]).astype(BF16)
    sub = lax.broadcasted_iota(jnp.int32, (1, SUBLANES, N_CHUNK), 1)
    z_chunks = list(range(0, di, N_CHUNK))
    conv_chunks = list(range(di, n_main, N_CHUNK))
    order = []
    while z_chunks or conv_chunks:
        order += conv_chunks[:2] + z_chunks[:1]
        conv_chunks, z_chunks = conv_chunks[2:], z_chunks[1:]
    nxt = _dot(h, w_ref[:, order[0]:order[0] + N_CHUNK])
    for idx, n0 in enumerate(order):
        a = nxt
        if idx + 1 < len(order):
            nxt = _dot(h, w_ref[:, order[idx + 1]:order[idx + 1] + N_CHUNK])
        if n0 < di:
            o_ref[:, n0:n0 + N_CHUNK] = _silu_of_half(a).astype(BF16)
            continue
        c0 = n0 - di
        tail = tail_ref[:, c0:c0 + N_CHUNK][None]
        tail_ref[:, c0:c0 + N_CHUNK] = a[rows - SUBLANES:rows, :]
        a3 = a.reshape(rows // SUBLANES, SUBLANES, N_CHUNK)
        for g0 in range(0, rows // SUBLANES, CONV_GROUPS):
            rot = a3[g0:g0 + CONV_GROUPS]
            prev = tail if g0 == 0 else a3[g0 - 1:g0]
            acc = cb_ref[:, c0:c0 + N_CHUNK] + cw_ref[SSM_CONV - 1:SSM_CONV, c0:c0 + N_CHUNK] * rot
            for j in range(1, SSM_CONV):
                rot = pltpu.roll(rot, 1, axis=1)
                prev = pltpu.roll(prev, 1, axis=1)
                above = jnp.concatenate([prev, rot[:-1]], axis=0)
                k = SSM_CONV - 1 - j
                acc = acc + cw_ref[k:k + 1, c0:c0 + N_CHUNK] * jnp.where(sub >= j, rot, above)
            r0 = g0 * SUBLANES
            o_ref[r0:r0 + CONV_GROUPS * SUBLANES, n0:n0 + N_CHUNK] = (
                _silu_of_half(acc).reshape(CONV_GROUPS * SUBLANES, N_CHUNK).astype(BF16))
    dt_ref[...] = _dot(h, wdt_ref[...])


def _a_in_proj(x2, g, w_main, w_dt, conv_w, conv_b, seq):
    t, d = x2.shape
    n_main = w_main.shape[1]
    return pl.pallas_call(
        functools.partial(_a_in_kernel, tiles_per_seq=seq // ROW_TILE),
        grid=(t // ROW_TILE,),
        in_specs=[
            pl.BlockSpec((ROW_TILE, d), lambda i: (i, 0)),
            pl.BlockSpec((1, d), lambda i: (0, 0)),
            pl.BlockSpec((d, n_main), lambda i: (0, 0)),
            pl.BlockSpec((d, LANES), lambda i: (0, 0)),
            pl.BlockSpec((SSM_CONV, SSM_CONV_DIM), lambda i: (0, 0)),
            pl.BlockSpec((1, SSM_CONV_DIM), lambda i: (0, 0)),
        ],
        out_specs=[
            pl.BlockSpec((ROW_TILE, n_main), lambda i: (i, 0)),
            pl.BlockSpec((ROW_TILE, LANES), lambda i: (i, 0)),
        ],
        out_shape=[
            jax.ShapeDtypeStruct((t, n_main), BF16),
            jax.ShapeDtypeStruct((t, LANES), F32),
        ],
        scratch_shapes=[pltpu.VMEM((SUBLANES, SSM_CONV_DIM), F32)],
        compiler_params=_cparams(("arbitrary",)),
        name="mamba_in_proj",
    )(x2, g, w_main, w_dt, conv_w, conv_b)


def _ssd_kernel(zx_ref, dtc_ref, dtr_ref, hp_row_ref, hp_col_ref, gng_ref, o_ref, state_ref):
    q = zx_ref.shape[0]
    g_n, r_n, p_n, n_n = SSM_GROUPS, SSM_HPG, SSM_HEAD_DIM, SSM_STATE
    di = SSM_D_INNER
    hn = SSM_HEADS
    gw = r_n * p_n

    @pl.when(pl.program_id(1) == 0)
    def _():
        state_ref[...] = jnp.zeros_like(state_ref)

    dtb_row, alog_row, dskip_row = hp_row_ref[0:1, :], hp_row_ref[1:2, :], hp_row_ref[2:3, :]
    dtb_col, alog_col = hp_col_ref[:, 0:1], hp_col_ref[:, 1:2]
    dt_c = _softplus(dtc_ref[:, 0:hn] + dtb_row)
    adt_c = dt_c * (-jnp.exp(alog_row))
    dt_r = _softplus(dtr_ref[...] + dtb_col)
    adt_r = dt_r * (-jnp.exp(alog_col))
    ri = lax.broadcasted_iota(jnp.int32, (q, q), 0)
    ci = lax.broadcasted_iota(jnp.int32, (q, q), 1)
    causal = ri >= ci
    tril = jnp.where(causal, 1.0, 0.0).astype(BF16)
    triu = jnp.where(ri <= ci, 1.0, 0.0).astype(BF16)
    acs_c = sum(_dot(tril, part) for part in _split3(adt_c))
    acs_r = sum(_dot(part, triu) for part in _split3(adt_r))
    tot_row = acs_c[q - 1:q, :]
    eacs_c = jnp.exp(acs_c)
    toend_c = jnp.exp(tot_row - acs_c) * dt_c
    etot_row = jnp.exp(tot_row)
    acs2_c = acs_c * LOG2E
    acs2_r = acs_r * LOG2E

    lane_head = lax.broadcasted_iota(jnp.int32, (hn, gw), 1) // p_n
    row_head = lax.broadcasted_iota(jnp.int32, (hn, gw), 0)
    dt16 = dt_c.astype(BF16)
    toend16 = toend_c.astype(BF16)
    eacs_hi = _trunc_bf16(eacs_c)
    eacs_lo = (eacs_c - eacs_hi).astype(BF16)
    eacs_hi = eacs_hi.astype(BF16)
    lane_head_row = lane_head[0:1, :]
    head_mask16 = [jnp.where(lane_head_row == r, 1.0, 0.0).astype(BF16) for r in range(r_n)]

    def per_head_row(row, g):
        out = jnp.broadcast_to(row[:, g * r_n:g * r_n + 1], (1, gw))
        for r in range(1, r_n):
            hval = jnp.broadcast_to(row[:, g * r_n + r:g * r_n + r + 1], (1, gw))
            out = jnp.where(lane_head_row == r, hval, out)
        return out

    for g in range(g_n):
        spread = jnp.where(row_head == lane_head + g * r_n, 1.0, 0.0).astype(BF16)
        xs16 = zx_ref[:, di + g * gw:di + (g + 1) * gw]
        bm16 = zx_ref[:, 2 * di + g * n_n:2 * di + (g + 1) * n_n]
        cm16 = zx_ref[:, 2 * di + SSM_GN + g * n_n:2 * di + SSM_GN + (g + 1) * n_n]
        xs = xs16.astype(F32)
        cb = _dot_nt(cm16, bm16)
        xdt16 = (xs * _dot(dt16, spread)).astype(BF16)
        st = state_ref[g]
        y = _dot(cm16, st.astype(BF16)) * (_dot(eacs_hi, spread) + _dot(eacs_lo, spread))
        for r in range(r_n):
            h = g * r_n + r
            seg2 = jnp.broadcast_to(acs2_c[:, h:h + 1], (q, q)) - acs2_r[h:h + 1, :]
            lmat = (cb * jnp.where(causal, jnp.exp2(seg2), 0.0)).astype(BF16)
            y = y + _dot(lmat, xdt16 * head_mask16[r])
        y = y + per_head_row(dskip_row, g) * xs
        xend16 = (xs * _dot(toend16, spread)).astype(BF16)
        state_ref[g] = st * per_head_row(etot_row, g) + _dot_tn(bm16, xend16)
        u = y * zx_ref[:, g * gw:(g + 1) * gw].astype(F32)
        u = u * lax.rsqrt(jnp.mean(u * u, axis=-1, keepdims=True) + EPS)
        o_ref[:, g * gw:(g + 1) * gw] = (u * gng_ref[:, g * gw:(g + 1) * gw]).astype(BF16)


def _ssd(zx, dt_col, dt_row, hp_row, hp_col, gate_g, batch, seq):
    q = SSD_Q
    nc = seq // q
    n_main = zx.shape[1]
    hn = SSM_HEADS
    return pl.pallas_call(
        _ssd_kernel,
        grid=(batch, nc),
        in_specs=[
            pl.BlockSpec((q, n_main), lambda b, c: (b * nc + c, 0)),
            pl.BlockSpec((q, LANES), lambda b, c: (b * nc + c, 0)),
            pl.BlockSpec((None, hn, q), lambda b, c: (b, 0, c)),
            pl.BlockSpec((SUBLANES, hn), lambda b, c: (0, 0)),
            pl.BlockSpec((hn, SUBLANES), lambda b, c: (0, 0)),
            pl.BlockSpec((1, SSM_D_INNER), lambda b, c: (0, 0)),
        ],
        out_specs=pl.BlockSpec((q, SSM_D_INNER), lambda b, c: (b * nc + c, 0)),
        out_shape=jax.ShapeDtypeStruct((batch * seq, SSM_D_INNER), BF16),
        scratch_shapes=[
            pltpu.VMEM((SSM_GROUPS, SSM_STATE, SSM_HPG * SSM_HEAD_DIM), F32),
        ],
        compiler_params=_cparams(("parallel", "arbitrary")),
        name="mamba_ssd",
    )(zx, dt_col, dt_row, hp_row, hp_col, gate_g)


def _out_proj_kernel(y_ref, w_ref, x_ref, o_ref):
    o_ref[...] = x_ref[...] + _dot(y_ref[...], w_ref[...])


def _out_proj(y, w, x2):
    t, k = y.shape
    d = w.shape[1]
    return pl.pallas_call(
        _out_proj_kernel,
        grid=(t // ROW_TILE,),
        in_specs=[
            pl.BlockSpec((ROW_TILE, k), lambda i: (i, 0)),
            pl.BlockSpec((k, d), lambda i: (0, 0)),
            pl.BlockSpec((ROW_TILE, d), lambda i: (i, 0)),
        ],
        out_specs=pl.BlockSpec((ROW_TILE, d), lambda i: (i, 0)),
        out_shape=jax.ShapeDtypeStruct((t, d), F32),
        compiler_params=_cparams(("parallel",)),
        name="mamba_out_proj",
    )(y, w, x2)


def _gated_out_proj_kernel(a_ref, gate_ref, w_ref, x_ref, fg_ref, o_ref, *, final_norm):
    gate = gate_ref[...].astype(F32)
    y = (a_ref[...].astype(F32) * _silu(gate)).astype(BF16)
    out = x_ref[...] + _dot(y, w_ref[...])
    if final_norm:
        out = _rms(out, fg_ref[...])
    o_ref[...] = out


def _gated_out_proj(attn, gate, w, x2, final_g, final_norm):
    t, k = attn.shape
    d = w.shape[1]
    return pl.pallas_call(
        functools.partial(_gated_out_proj_kernel, final_norm=final_norm),
        grid=(t // ROW_TILE,),
        in_specs=[
            pl.BlockSpec((ROW_TILE, k), lambda i: (i, 0)),
            pl.BlockSpec((ROW_TILE, k), lambda i: (i, 0)),
            pl.BlockSpec((k, d), lambda i: (0, 0)),
            pl.BlockSpec((ROW_TILE, d), lambda i: (i, 0)),
            pl.BlockSpec((1, d), lambda i: (0, 0)),
        ],
        out_specs=pl.BlockSpec((ROW_TILE, d), lambda i: (i, 0)),
        out_shape=jax.ShapeDtypeStruct((t, d), F32),
        compiler_params=_cparams(("parallel",)),
        name="attn_out_proj",
    )(attn, gate, w, x2, final_g)


def _kv_kernel(x_ref, g_ref, wk_ref, wv_ref, aug_ref, k_ref, vt_ref):
    h = _rms(x_ref[...], g_ref[...]).astype(BF16)
    hn = DIFF_HEADS
    for n0 in range(0, wk_ref.shape[1], N_CHUNK):
        kc = _dot(h, wk_ref[:, n0:n0 + N_CHUNK])
        for j in range(N_CHUNK // QK_PAD):
            idx = n0 // QK_PAD + j
            k_ref[idx // 2, idx % 2] = (kc[:, j * QK_PAD:(j + 1) * QK_PAD]
                                        + aug_ref[idx // 2]).astype(BF16)
    for n0 in range(0, wv_ref.shape[1], N_CHUNK):
        vc = _dot(h, wv_ref[:, n0:n0 + N_CHUNK])
        for j in range(N_CHUNK // DIFF_V_DIM):
            head = n0 // DIFF_V_DIM + j
            vt_ref[head] = vc[:, j * DIFF_V_DIM:(j + 1) * DIFF_V_DIM].T.astype(BF16)
    del hn


def _kv_proj(x2, g, wk, wv, k_aug, batch, seq):
    d = x2.shape[1]
    nt = seq // ROW_TILE
    hn = DIFF_HEADS
    return pl.pallas_call(
        _kv_kernel,
        grid=(batch, nt),
        in_specs=[
            pl.BlockSpec((ROW_TILE, d), lambda b, i: (b * nt + i, 0)),
            pl.BlockSpec((1, d), lambda b, i: (0, 0)),
            pl.BlockSpec(wk.shape, lambda b, i: (0, 0)),
            pl.BlockSpec(wv.shape, lambda b, i: (0, 0)),
            pl.BlockSpec((hn, ROW_TILE, QK_PAD), lambda b, i: (0, i, 0)),
        ],
        out_specs=[
            pl.BlockSpec((None, hn, 2, ROW_TILE, QK_PAD), lambda b, i: (b, 0, 0, i, 0)),
            pl.BlockSpec((None, hn, DIFF_V_DIM, ROW_TILE), lambda b, i: (b, 0, 0, i)),
        ],
        out_shape=[
            jax.ShapeDtypeStruct((batch, hn, 2, seq, QK_PAD), BF16),
            jax.ShapeDtypeStruct((batch, hn, DIFF_V_DIM, seq), BF16),
        ],
        compiler_params=_cparams(("parallel", "parallel")),
        name="kv_proj",
    )(x2, g, wk, wv, k_aug)


def _b_in_kernel(x_ref, g_ref, wq_ref, wg_ref, aug_ref, q_ref, gate_ref):
    h = _rms(x_ref[...], g_ref[...]).astype(BF16)
    for n0 in range(0, wq_ref.shape[1], N_CHUNK):
        qc = _dot(h, wq_ref[:, n0:n0 + N_CHUNK]) * QK_SCALE
        for j in range(N_CHUNK // QK_PAD):
            idx = n0 // QK_PAD + j
            q_ref[idx // 2, idx % 2] = (qc[:, j * QK_PAD:(j + 1) * QK_PAD] + aug_ref[...]).astype(BF16)
    for n0 in range(0, wg_ref.shape[1], N_CHUNK):
        gate_ref[:, n0:n0 + N_CHUNK] = _dot(h, wg_ref[:, n0:n0 + N_CHUNK]).astype(BF16)


def _b_in_proj(x2, g, wq, wg, q_aug, batch, seq):
    d = x2.shape[1]
    nt = seq // ROW_TILE
    hn = DIFF_HEADS
    return pl.pallas_call(
        _b_in_kernel,
        grid=(batch, nt),
        in_specs=[
            pl.BlockSpec((ROW_TILE, d), lambda b, i: (b * nt + i, 0)),
            pl.BlockSpec((1, d), lambda b, i: (0, 0)),
            pl.BlockSpec(wq.shape, lambda b, i: (0, 0)),
            pl.BlockSpec(wg.shape, lambda b, i: (0, 0)),
            pl.BlockSpec((ROW_TILE, QK_PAD), lambda b, i: (i, 0)),
        ],
        out_specs=[
            pl.BlockSpec((None, hn, 2, ROW_TILE, QK_PAD), lambda b, i: (b, 0, 0, i, 0)),
            pl.BlockSpec((ROW_TILE, DIFF_WIDTH), lambda b, i: (b * nt + i, 0)),
        ],
        out_shape=[
            jax.ShapeDtypeStruct((batch, hn, 2, seq, QK_PAD), BF16),
            jax.ShapeDtypeStruct((batch * seq, DIFF_WIDTH), BF16),
        ],
        compiler_params=_cparams(("parallel", "parallel")),
        name="attn_in_proj",
    )(x2, g, wq, wg, q_aug)


def _attn_kernel(lam_ref, q_ref, k_ref, vt_ref, dbias_ref, subg_ref, o_ref,
                 s_a, s_b, p_a, p_b, p_last, m_scr, acc_scr, l_scr, al_scr, *, lambda_init):
    tq, tk = ATT_TQ, ATT_TK
    seq = q_ref.shape[2]
    places = tk // tq
    assert places * 2 * ATT_UNROLL >= seq // tq
    streams = [(hh, i) for hh in range(ATT_HEADS) for i in range(2)]
    n_qt = seq // tq
    lf = lam_ref[...]
    lam = (jnp.exp(jnp.sum(lf[0:1, :] * lf[1:2, :], axis=-1, keepdims=True))
           - jnp.exp(jnp.sum(lf[2:3, :] * lf[3:4, :], axis=-1, keepdims=True)) + lambda_init)
    neg = jnp.float32(-1e30)

    def fold(x, op):
        return op(x.reshape(x.shape[0] // SUBLANES, SUBLANES, tq), axis=0)

    def load_q(qi):
        q0 = pl.multiple_of(qi * tq, tq)
        return [q_ref[hh, i, pl.ds(q0, tq), :] for hh, i in streams]

    def key_rows(t, rows=tk):
        return pl.ds(t * tk if isinstance(t, int) else pl.multiple_of(t * tk, tk), rows)

    def scores_into(qs, t, s_buf, rows=tk, want_max=True):
        mparts = []
        for n, (hh, i) in enumerate(streams):
            s = _dot_nt(k_ref[hh, i, key_rows(t, rows), :], qs[n])
            s_buf[n, 0:rows, :] = s
            if want_max:
                mparts.append(fold(s, jnp.max))
        return tuple(mparts)

    def stage_c(t, p_buf, alphas, accs, rows=tk):
        vts = [vt_ref[hh, :, key_rows(t, rows)] for hh in range(ATT_HEADS)]
        return tuple(alphas[n] * accs[n] + _dot(vts[hh], p_buf[n, 0:rows, :])
                     for n, (hh, _) in enumerate(streams))

    def flush_previous(qi, old_place):
        q_old = jnp.maximum(qi - 1, 0)
        o0 = pl.multiple_of(q_old * tq, tq)
        slot = q_old % 2
        accs = stage_c(q_old // (tk // tq), p_last.at[slot],
                       tuple(al_scr[slot, n, 0:1, :] for n in range(len(streams))),
                       tuple(acc_scr[slot, n] for n in range(len(streams))),
                       rows=(old_place + 1) * tq)
        for hh in range(ATT_HEADS):
            l1 = jnp.sum(l_scr[slot, 2 * hh], axis=0, keepdims=True)
            l2 = jnp.sum(l_scr[slot, 2 * hh + 1], axis=0, keepdims=True)
            out_t = accs[2 * hh] / l1 - lam * (accs[2 * hh + 1] / l2)
            out = _rms(out_t.T, subg_ref[...]) * (1.0 - lambda_init)
            o_ref[pl.ds(o0, tq), hh * DIFF_V_DIM:(hh + 1) * DIFF_V_DIM] = out.astype(BF16)
        return tuple(jnp.zeros((DIFF_V_DIM, tq), F32) for _ in streams)

    def q_tile(qi, carry):
        n_full = qi // places
        which = qi % places
        qs = load_q(qi)

        def stage_a(t, s_buf, rows=tk, want_max=True):
            return scores_into(qs, t, s_buf, rows, want_max)

        def stage_b(s_buf, p_buf, mparts, ms, ls, place=None):
            diag = place is not None
            rows = (place + 1) * tq if diag else tk
            new_m, new_l, alphas = [], [], []
            for n, (hh, _) in enumerate(streams):
                def scores(r0):
                    s = s_buf[n, r0:r0 + ATT_ROWS, :]
                    return s + dbias_ref[hh, place, r0:r0 + ATT_ROWS, :] if diag else s

                if diag:
                    mpart = fold(scores(0), jnp.max)
                    for r0 in range(ATT_ROWS, rows, ATT_ROWS):
                        mpart = jnp.maximum(mpart, fold(scores(r0), jnp.max))
                else:
                    mpart = mparts[n]
                m_new = jnp.maximum(ms[n], jnp.max(mpart, axis=0, keepdims=True))
                alpha = jnp.exp2(ms[n] - m_new)
                lsum = alpha * ls[n]
                for r0 in range(0, rows, ATT_ROWS):
                    p = jnp.exp2(scores(r0) - m_new)
                    lsum = lsum + fold(p, jnp.sum)
                    p_buf[n, r0:r0 + ATT_ROWS, :] = p.astype(BF16)
                new_l.append(lsum)
                new_m.append(m_new)
                alphas.append(alpha)
            return tuple(new_m), tuple(new_l), tuple(alphas)

        def step(j, bufs, st, place, last):
            s_cur, s_nxt, p_cur, p_prv = bufs
            mparts, ms, ls, alphas, accs = st
            if j + 1 == last:
                mparts_next = stage_a(j + 1, s_nxt, (place + 1) * tq, False)
            else:
                mparts_next = stage_a(j + 1, s_nxt)
            ms, ls, alphas_new = stage_b(s_cur, p_cur, mparts, ms, ls)
            accs = (flush_previous(qi, (place - 1) % places) if j == 0
                    else stage_c(j - 1, p_prv, alphas, accs))
            return mparts_next, ms, ls, alphas_new, accs

        def steps(j0, j1, st, place, last=None):
            for j in range(j0, j1):
                st = step(j, odd if j % 2 else even, st, place, last)
            return st

        def finish(v, st, place):
            s_cur, _, _, p_prv = odd if v % 2 else even
            _, ms, ls, alphas, accs = st
            accs = (flush_previous(qi, (place - 1) % places) if v == 0
                    else stage_c(v - 1, p_prv, alphas, accs))
            slot = qi % 2
            _, ls, alphas_new = stage_b(s_cur, p_last.at[slot], None, ms, ls, place)
            nxt = scores_into(load_q(jnp.minimum(qi + 1, n_qt - 1)), 0, s_a)
            for n in range(len(streams)):
                m_scr[n] = nxt[n]
                acc_scr[slot, n] = accs[n]
                l_scr[slot, n] = ls[n]
                al_scr[slot, n] = jnp.broadcast_to(alphas_new[n], (SUBLANES, tq))

        even = (s_a, s_b, p_a, p_b)
        odd = (s_b, s_a, p_b, p_a)
        init = (tuple(m_scr[n] for n in range(len(streams))),
                tuple(jnp.full((1, tq), neg, F32) for _ in streams),
                tuple(jnp.zeros((SUBLANES, tq), F32) for _ in streams),
                tuple(jnp.ones((1, tq), F32) for _ in streams),
                tuple(jnp.zeros((DIFF_V_DIM, tq), F32) for _ in streams))
        for place in range(places):
            @pl.when(which == place)
            def _(place=place):
                st = init
                if ATT_UNROLL < n_qt // places:
                    st = lax.fori_loop(0, n_full // ATT_UNROLL,
                                       lambda _, c: steps(0, ATT_UNROLL, c, place), init)
                for v in range(n_qt // places):
                    @pl.when(n_full == v)
                    def _(v=v):
                        lo = v - v % ATT_UNROLL if v >= ATT_UNROLL else 0
                        finish(v, steps(lo, v, st, place, last=v), place)

        return carry

    first = scores_into(load_q(0), 0, s_a)
    for n in range(len(streams)):
        m_scr[n] = first[n]
        acc_scr[0, n] = jnp.zeros((DIFF_V_DIM, tq), F32)
        l_scr[0, n] = jnp.ones((SUBLANES, tq), F32)
        al_scr[0, n] = jnp.ones((SUBLANES, tq), F32)
    p_last[0] = jnp.zeros(p_last.shape[1:], BF16)
    lax.fori_loop(0, n_qt, q_tile, 0)
    flush_previous(n_qt, (n_qt - 1) % places)


def _attention(lam_qk, q, k, vt, dbias, sub_g, lambda_init):
    batch, hn, _, seq, _ = q.shape
    return pl.pallas_call(
        functools.partial(_attn_kernel, lambda_init=lambda_init),
        grid=(batch, hn // ATT_HEADS),
        in_specs=[
            pl.BlockSpec(lam_qk.shape, lambda b, h: (0, 0)),
            pl.BlockSpec((None, ATT_HEADS, 2, seq, QK_PAD), lambda b, h: (b, h, 0, 0, 0)),
            pl.BlockSpec((None, ATT_HEADS, 2, seq, QK_PAD), lambda b, h: (b, h, 0, 0, 0)),
            pl.BlockSpec((None, ATT_HEADS, DIFF_V_DIM, seq), lambda b, h: (b, h, 0, 0)),
            pl.BlockSpec((ATT_HEADS, ATT_TK // ATT_TQ, ATT_TK, ATT_TQ), lambda b, h: (h, 0, 0, 0)),
            pl.BlockSpec((1, DIFF_V_DIM), lambda b, h: (0, 0)),
        ],
        out_specs=pl.BlockSpec((None, seq, ATT_HEADS * DIFF_V_DIM), lambda b, h: (b, 0, h)),
        out_shape=jax.ShapeDtypeStruct((batch, seq, DIFF_WIDTH), BF16),
        scratch_shapes=[
            pltpu.VMEM((2 * ATT_HEADS, ATT_TK, ATT_TQ), F32),
            pltpu.VMEM((2 * ATT_HEADS, ATT_TK, ATT_TQ), F32),
            pltpu.VMEM((2 * ATT_HEADS, ATT_TK, ATT_TQ), BF16),
            pltpu.VMEM((2 * ATT_HEADS, ATT_TK, ATT_TQ), BF16),
            pltpu.VMEM((2, 2 * ATT_HEADS, ATT_TK, ATT_TQ), BF16),
            pltpu.VMEM((2 * ATT_HEADS, SUBLANES, ATT_TQ), F32),
            pltpu.VMEM((2, 2 * ATT_HEADS, DIFF_V_DIM, ATT_TQ), F32),
            pltpu.VMEM((2, 2 * ATT_HEADS, SUBLANES, ATT_TQ), F32),
            pltpu.VMEM((2, 2 * ATT_HEADS, SUBLANES, ATT_TQ), F32),
        ],
        compiler_params=_cparams(("parallel", "parallel")),
        name="diff_attention",
    )(lam_qk, q, k, vt, dbias, sub_g)


def _np_split3(a):
    a = a.astype(np.float32)
    a1 = a.astype(ml_dtypes.bfloat16).astype(np.float32)
    r1 = a - a1
    a2 = r1.astype(ml_dtypes.bfloat16).astype(np.float32)
    a3 = (r1 - a2).astype(ml_dtypes.bfloat16).astype(np.float32)
    return a1, a2, a3


def _alibi_tables(seq):
    hn = DIFF_HEADS
    slopes = np.float32(2.0) ** (-np.float32(ALIBI_MAX_EXP) * np.arange(1, hn + 1, dtype=np.float32)
                                 / np.float32(hn))
    slopes = (slopes.astype(np.float64) * LOG2E).astype(np.float32)
    pos = np.arange(seq, dtype=np.float32)
    kp = _np_split3(slopes[:, None] * pos[None, :])
    sl = _np_split3(np.broadcast_to(slopes[:, None], (hn, seq)))
    pos_hi = np.floor(pos / 256.0) * 256.0
    pos_lo = pos - pos_hi
    ones = np.ones((seq,), np.float32)
    k_cols = list(kp)
    q_cols = [ones, ones, ones]
    for t in sl:
        k_cols += [-t, -t]
        q_cols += [pos_hi, pos_lo]
    lane_pad = (DIFF_HEAD_DIM, QK_PAD - DIFF_HEAD_DIM - len(k_cols))
    k_aug = jnp.pad(jnp.asarray(np.stack(k_cols, axis=-1)), ((0, 0), (0, 0), lane_pad))
    q_aug = jnp.pad(jnp.asarray(np.stack(q_cols, axis=-1)), ((0, 0), lane_pad))
    kk = np.arange(ATT_TK)[None, :, None]
    qq = np.arange(ATT_TQ)[None, None, :] + ATT_TQ * np.arange(ATT_TK // ATT_TQ)[:, None, None]
    allowed = (kk // CHUNK) <= (qq // CHUNK)
    fix = np.where(kk > qq, -2.0 * (kk - qq), 0.0).astype(np.float32)
    dbias = np.where(allowed[None], slopes[:, None, None, None] * fix[None], -np.inf).astype(np.float32)
    return k_aug, q_aug, jnp.asarray(dbias)


def _pad_heads(w, n_blocks):
    d = w.shape[0]
    w = w.reshape(d, n_blocks, DIFF_HEAD_DIM)
    w = jnp.concatenate([w, jnp.zeros_like(w)], axis=-1)
    return w.reshape(d, n_blocks * QK_PAD).astype(BF16)


def kernel(x, a_norm_g, a_w_in, a_conv_w, a_conv_b, a_dt_bias, a_a_log, a_d_skip, a_gate_norm_g,
           a_w_out, kv_norm_g, w_kv, b_norm_g, b_w_in, b_lambda, b_sub_g, b_w_out, final_norm_g):
    batch, seq, d = x.shape
    t = batch * seq
    hn = SSM_HEADS
    x2 = x.reshape(t, d)

    for i in range(N_A_LAYERS):
        w_in = a_w_in[i]
        n_main = SSM_D_INNER + SSM_CONV_DIM
        w_main = jnp.concatenate([0.5 * w_in[:, :SSM_D_INNER], w_in[:, SSM_D_INNER:n_main]], axis=1).astype(BF16)
        w_dt = jnp.pad(w_in[:, n_main:], ((0, 0), (0, LANES - hn))).astype(BF16)
        zx, dt_raw = _a_in_proj(x2, a_norm_g[i][None], w_main, w_dt, 0.5 * a_conv_w[i],
                                0.5 * a_conv_b[i][None], seq)
        dt_row = dt_raw[:, :hn].reshape(batch, seq, hn).transpose(0, 2, 1)
        hp_row = jnp.zeros((SUBLANES, hn), F32).at[0].set(a_dt_bias[i]).at[1].set(a_a_log[i]).at[2].set(a_d_skip[i])
        hp_col = hp_row.T
        y = _ssd(zx, dt_raw, dt_row, hp_row, hp_col, a_gate_norm_g[i][None], batch, seq)
        x2 = _out_proj(y, a_w_out[i].astype(BF16), x2)

    k_aug, q_aug, dbias = _alibi_tables(seq)
    n_k = DIFF_HEADS * 2 * DIFF_HEAD_DIM
    wk = _pad_heads(w_kv[:, :n_k], DIFF_HEADS * 2)
    wv = w_kv[:, n_k:].astype(BF16)
    k, vt = _kv_proj(x2, kv_norm_g[None], wk, wv, k_aug, batch, seq)

    for j in range(N_B_LAYERS):
        layer = N_A_LAYERS + j
        lambda_init = 0.8 - 0.6 * math.exp(-0.3 * layer)
        wq = _pad_heads(b_w_in[j][:, :DIFF_WIDTH], DIFF_HEADS * 2)
        wg = b_w_in[j][:, DIFF_WIDTH:].astype(BF16)
        q, gate = _b_in_proj(x2, b_norm_g[j][None], wq, wg, q_aug, batch, seq)
        attn = _attention(b_lambda[j], q, k, vt, dbias, b_sub_g[j][None], lambda_init)
        x2 = _gated_out_proj(attn.reshape(t, DIFF_WIDTH), gate, b_w_out[j].astype(BF16), x2,
                             final_norm_g[None], final_norm=(j == N_B_LAYERS - 1))
    return x2.reshape(batch, seq, d)
```

```python
import functools
import math

import jax
import jax.numpy as jnp
import ml_dtypes
import numpy as np
from jax import lax
from jax.experimental import pallas as pl
from jax.experimental.pallas import tpu as pltpu

F32 = jnp.float32
BF16 = jnp.bfloat16

D_MODEL = 1024
DEPTH = 4
CHUNK = 64
N_B_LAYERS = DEPTH // 2
N_A_LAYERS = DEPTH - N_B_LAYERS
SSM_D_INNER = 2 * D_MODEL
SSM_HEAD_DIM = 64
SSM_HEADS = SSM_D_INNER // SSM_HEAD_DIM
SSM_GROUPS = 8
SSM_HPG = SSM_HEADS // SSM_GROUPS
SSM_STATE = 128
SSM_CONV = 4
SSM_GN = SSM_GROUPS * SSM_STATE
SSM_CONV_DIM = SSM_D_INNER + 2 * SSM_GN
DIFF_HEAD_DIM = 64
DIFF_V_DIM = 2 * DIFF_HEAD_DIM
DIFF_HEADS = D_MODEL // DIFF_V_DIM
DIFF_WIDTH = DIFF_HEADS * DIFF_V_DIM
ALIBI_MAX_EXP = 8.0
EPS = 1e-5
LOG2E = math.log2(math.e)
QK_SCALE = LOG2E / math.sqrt(DIFF_HEAD_DIM)

LANES = 128
SUBLANES = 8
VMEM_LIMIT = 56 * 1024 * 1024

ROW_TILE = 512
SSD_Q = 256
ATT_TQ = 256
ATT_TK = 512
ATT_UNROLL = 4
ATT_HEADS = 2
ATT_ROWS = 64
N_CHUNK = 256
CONV_GROUPS = 8
QK_PAD = 2 * DIFF_HEAD_DIM


def _cparams(sem):
    return pltpu.CompilerParams(dimension_semantics=sem, vmem_limit_bytes=VMEM_LIMIT)


def _rms(x, g):
    return x * lax.rsqrt(jnp.mean(x * x, axis=-1, keepdims=True) + EPS) * g


def _silu_of_half(h):
    return h * jnp.tanh(h) + h


def _silu(x):
    return _silu_of_half(0.5 * x)


def _softplus(x):
    return jnp.maximum(x, 0.0) + jnp.log1p(jnp.exp(-jnp.abs(x)))


def _trunc_bf16(a):
    bits = lax.bitcast_convert_type(a, jnp.int32) & jnp.int32(-65536)
    return lax.bitcast_convert_type(bits, F32)


def _split3(a):
    a1 = _trunc_bf16(a)
    r1 = a - a1
    a2 = _trunc_bf16(r1)
    return a1.astype(BF16), a2.astype(BF16), (r1 - a2).astype(BF16)


def _dot(a, b):
    return jnp.dot(a, b, preferred_element_type=F32)


def _dot_nt(a, b):
    return lax.dot_general(a, b, (((1,), (1,)), ((), ())), preferred_element_type=F32)


def _dot_tn(a, b):
    return lax.dot_general(a, b, (((0,), (0,)), ((), ())), preferred_element_type=F32)


def _a_in_kernel(x_ref, g_ref, w_ref, wdt_ref, cw_ref, cb_ref, o_ref, dt_ref, tail_ref,
                 *, tiles_per_seq):
    rows = x_ref.shape[0]
    di = SSM_D_INNER
    n_main = o_ref.shape[-1]

    @pl.when(pl.program_id(0) % tiles_per_seq == 0)
    def _():
        tail_ref[...] = jnp.zeros_like(tail_ref)

    h = _rms(x_ref[...], g_ref[...]).astype(BF16)
    sub = lax.broadcasted_iota(jnp.int32, (1, SUBLANES, N_CHUNK), 1)
    z_chunks = list(range(0, di, N_CHUNK))
    conv_chunks = list(range(di, n_main, N_CHUNK))
    order = []
    while z_chunks or conv_chunks:
        order += conv_chunks[:2] + z_chunks[:1]
        conv_chunks, z_chunks = conv_chunks[2:], z_chunks[1:]
    nxt = _dot(h, w_ref[:, order[0]:order[0] + N_CHUNK])
    for idx, n0 in enumerate(order):
        a = nxt
        if idx + 1 < len(order):
            nxt = _dot(h, w_ref[:, order[idx + 1]:order[idx + 1] + N_CHUNK])
        if n0 < di:
            o_ref[:, n0:n0 + N_CHUNK] = _silu_of_half(a).astype(BF16)
            continue
        c0 = n0 - di
        tail = tail_ref[:, c0:c0 + N_CHUNK][None]
        tail_ref[:, c0:c0 + N_CHUNK] = a[rows - SUBLANES:rows, :]
        a3 = a.reshape(rows // SUBLANES, SUBLANES, N_CHUNK)
        for g0 in range(0, rows // SUBLANES, CONV_GROUPS):
            rot = a3[g0:g0 + CONV_GROUPS]
            prev = tail if g0 == 0 else a3[g0 - 1:g0]
            acc = cb_ref[:, c0:c0 + N_CHUNK] + cw_ref[SSM_CONV - 1:SSM_CONV, c0:c0 + N_CHUNK] * rot
            for j in range(1, SSM_CONV):
                rot = pltpu.roll(rot, 1, axis=1)
                prev = pltpu.roll(prev, 1, axis=1)
                above = jnp.concatenate([prev, rot[:-1]], axis=0)
                k = SSM_CONV - 1 - j
                acc = acc + cw_ref[k:k + 1, c0:c0 + N_CHUNK] * jnp.where(sub >= j, rot, above)
            r0 = g0 * SUBLANES
            o_ref[r0:r0 + CONV_GROUPS * SUBLANES, n0:n0 + N_CHUNK] = (
                _silu_of_half(acc).reshape(CONV_GROUPS * SUBLANES, N_CHUNK).astype(BF16))
    dt_ref[...] = _dot(h, wdt_ref[...])


def _a_in_proj(x2, g, w_main, w_dt, conv_w, conv_b, seq):
    t, d = x2.shape
    n_main = w_main.shape[1]
    return pl.pallas_call(
        functools.partial(_a_in_kernel, tiles_per_seq=seq // ROW_TILE),
        grid=(t // ROW_TILE,),
        in_specs=[
            pl.BlockSpec((ROW_TILE, d), lambda i: (i, 0)),
            pl.BlockSpec((1, d), lambda i: (0, 0)),
            pl.BlockSpec((d, n_main), lambda i: (0, 0)),
            pl.BlockSpec((d, LANES), lambda i: (0, 0)),
            pl.BlockSpec((SSM_CONV, SSM_CONV_DIM), lambda i: (0, 0)),
            pl.BlockSpec((1, SSM_CONV_DIM), lambda i: (0, 0)),
        ],
        out_specs=[
            pl.BlockSpec((ROW_TILE, n_main), lambda i: (i, 0)),
            pl.BlockSpec((ROW_TILE, LANES), lambda i: (i, 0)),
        ],
        out_shape=[
            jax.ShapeDtypeStruct((t, n_main), BF16),
            jax.ShapeDtypeStruct((t, LANES), F32),
        ],
        scratch_shapes=[pltpu.VMEM((SUBLANES, SSM_CONV_DIM), F32)],
        compiler_params=_cparams(("arbitrary",)),
        name="mamba_in_proj",
    )(x2, g, w_main, w_dt, conv_w, conv_b)


def _ssd_kernel(zx_ref, dtc_ref, dtr_ref, hp_row_ref, hp_col_ref, gng_ref, o_ref, state_ref):
    q = zx_ref.shape[0]
    g_n, r_n, p_n, n_n = SSM_GROUPS, SSM_HPG, SSM_HEAD_DIM, SSM_STATE
    di = SSM_D_INNER
    hn = SSM_HEADS
    gw = r_n * p_n

    @pl.when(pl.program_id(1) == 0)
    def _():
        state_ref[...] = jnp.zeros_like(state_ref)

    dtb_row, alog_row, dskip_row = hp_row_ref[0:1, :], hp_row_ref[1:2, :], hp_row_ref[2:3, :]
    dtb_col, alog_col = hp_col_ref[:, 0:1], hp_col_ref[:, 1:2]
    dt_c = _softplus(dtc_ref[:, 0:hn] + dtb_row)
    adt_c = dt_c * (-jnp.exp(alog_row))
    dt_r = _softplus(dtr_ref[...] + dtb_col)
    adt_r = dt_r * (-jnp.exp(alog_col))
    ri = lax.broadcasted_iota(jnp.int32, (q, q), 0)
    ci = lax.broadcasted_iota(jnp.int32, (q, q), 1)
    causal = ri >= ci
    tril = jnp.where(causal, 1.0, 0.0).astype(BF16)
    triu = jnp.where(ri <= ci, 1.0, 0.0).astype(BF16)
    acs_c = sum(_dot(tril, part) for part in _split3(adt_c))
    acs_r = sum(_dot(part, triu) for part in _split3(adt_r))
    tot_row = acs_c[q - 1:q, :]
    eacs_c = jnp.exp(acs_c)
    toend_c = jnp.exp(tot_row - acs_c) * dt_c
    etot_row = jnp.exp(tot_row)
    acs2_c = acs_c * LOG2E
    src2_r = (acs_r - jnp.log(dt_r)) * LOG2E

    lane_head = lax.broadcasted_iota(jnp.int32, (hn, gw), 1) // p_n
    row_head = lax.broadcasted_iota(jnp.int32, (hn, gw), 0)
    toend16 = toend_c.astype(BF16)
    eacs16 = eacs_c.astype(BF16)
    lane_head_row = lane_head[0:1, :]
    head_mask16 = [jnp.where(lane_head_row == r, 1.0, 0.0).astype(BF16) for r in range(r_n)]

    def per_head_row(row, g):
        out = jnp.broadcast_to(row[:, g * r_n:g * r_n + 1], (1, gw))
        for r in range(1, r_n):
            hval = jnp.broadcast_to(row[:, g * r_n + r:g * r_n + r + 1], (1, gw))
            out = jnp.where(lane_head_row == r, hval, out)
        return out

    for g in range(g_n):
        spread = jnp.where(row_head == lane_head + g * r_n, 1.0, 0.0).astype(BF16)
        xs16 = zx_ref[:, di + g * gw:di + (g + 1) * gw]
        bm16 = zx_ref[:, 2 * di + g * n_n:2 * di + (g + 1) * n_n]
        cm16 = zx_ref[:, 2 * di + SSM_GN + g * n_n:2 * di + SSM_GN + (g + 1) * n_n]
        xs = xs16.astype(F32)
        cb = _dot_nt(cm16, bm16)
        st = state_ref[g]
        y = _dot(cm16, st.astype(BF16)) * _dot(eacs16, spread)
        for r in range(r_n):
            h = g * r_n + r
            seg2 = jnp.broadcast_to(acs2_c[:, h:h + 1], (q, q)) - src2_r[h:h + 1, :]
            lmat = (cb * jnp.where(causal, jnp.exp2(seg2), 0.0)).astype(BF16)
            y = y + _dot(lmat, xs16 * head_mask16[r])
        y = y + per_head_row(dskip_row, g) * xs
        xend16 = (xs * _dot(toend16, spread)).astype(BF16)
        state_ref[g] = st * per_head_row(etot_row, g) + _dot_tn(bm16, xend16)
        u = y * zx_ref[:, g * gw:(g + 1) * gw].astype(F32)
        u = u * lax.rsqrt(jnp.mean(u * u, axis=-1, keepdims=True) + EPS)
        o_ref[:, g * gw:(g + 1) * gw] = (u * gng_ref[:, g * gw:(g + 1) * gw]).astype(BF16)


def _ssd(zx, dt_col, dt_row, hp_row, hp_col, gate_g, batch, seq):
    q = SSD_Q
    nc = seq // q
    n_main = zx.shape[1]
    hn = SSM_HEADS
    return pl.pallas_call(
        _ssd_kernel,
        grid=(batch, nc),
        in_specs=[
            pl.BlockSpec((q, n_main), lambda b, c: (b * nc + c, 0)),
            pl.BlockSpec((q, LANES), lambda b, c: (b * nc + c, 0)),
            pl.BlockSpec((None, hn, q), lambda b, c: (b, 0, c)),
            pl.BlockSpec((SUBLANES, hn), lambda b, c: (0, 0)),
            pl.BlockSpec((hn, SUBLANES), lambda b, c: (0, 0)),
            pl.BlockSpec((1, SSM_D_INNER), lambda b, c: (0, 0)),
        ],
        out_specs=pl.BlockSpec((q, SSM_D_INNER), lambda b, c: (b * nc + c, 0)),
        out_shape=jax.ShapeDtypeStruct((batch * seq, SSM_D_INNER), BF16),
        scratch_shapes=[
            pltpu.VMEM((SSM_GROUPS, SSM_STATE, SSM_HPG * SSM_HEAD_DIM), F32),
        ],
        compiler_params=_cparams(("parallel", "arbitrary")),
        name="mamba_ssd",
    )(zx, dt_col, dt_row, hp_row, hp_col, gate_g)


def _out_proj_kernel(y_ref, w_ref, x_ref, o_ref):
    o_ref[...] = x_ref[...] + _dot(y_ref[...], w_ref[...])


def _out_proj(y, w, x2):
    t, k = y.shape
    d = w.shape[1]
    return pl.pallas_call(
        _out_proj_kernel,
        grid=(t // ROW_TILE,),
        in_specs=[
            pl.BlockSpec((ROW_TILE, k), lambda i: (i, 0)),
            pl.BlockSpec((k, d), lambda i: (0, 0)),
            pl.BlockSpec((ROW_TILE, d), lambda i: (i, 0)),
        ],
        out_specs=pl.BlockSpec((ROW_TILE, d), lambda i: (i, 0)),
        out_shape=jax.ShapeDtypeStruct((t, d), F32),
        compiler_params=_cparams(("parallel",)),
        name="mamba_out_proj",
    )(y, w, x2)


def _gated_out_proj_kernel(a_ref, gate_ref, w_ref, x_ref, fg_ref, o_ref, *, final_norm):
    gate = gate_ref[...].astype(F32)
    y = (a_ref[...].astype(F32) * _silu(gate)).astype(BF16)
    out = x_ref[...] + _dot(y, w_ref[...])
    if final_norm:
        out = _rms(out, fg_ref[...])
    o_ref[...] = out


def _gated_out_proj(attn, gate, w, x2, final_g, final_norm):
    t, k = attn.shape
    d = w.shape[1]
    return pl.pallas_call(
        functools.partial(_gated_out_proj_kernel, final_norm=final_norm),
        grid=(t // ROW_TILE,),
        in_specs=[
            pl.BlockSpec((ROW_TILE, k), lambda i: (i, 0)),
            pl.BlockSpec((ROW_TILE, k), lambda i: (i, 0)),
            pl.BlockSpec((k, d), lambda i: (0, 0)),
            pl.BlockSpec((ROW_TILE, d), lambda i: (i, 0)),
            pl.BlockSpec((1, d), lambda i: (0, 0)),
        ],
        out_specs=pl.BlockSpec((ROW_TILE, d), lambda i: (i, 0)),
        out_shape=jax.ShapeDtypeStruct((t, d), F32),
        compiler_params=_cparams(("parallel",)),
        name="attn_out_proj",
    )(attn, gate, w, x2, final_g)


def _kv_kernel(x_ref, g_ref, wk_ref, wv_ref, aug_ref, k_ref, vt_ref):
    h = _rms(x_ref[...], g_ref[...]).astype(BF16)
    hn = DIFF_HEADS
    for n0 in range(0, wk_ref.shape[1], N_CHUNK):
        kc = _dot(h, wk_ref[:, n0:n0 + N_CHUNK])
        for j in range(N_CHUNK // QK_PAD):
            idx = n0 // QK_PAD + j
            k_ref[idx // 2, idx % 2] = (kc[:, j * QK_PAD:(j + 1) * QK_PAD]
                                        + aug_ref[idx // 2]).astype(BF16)
    for n0 in range(0, wv_ref.shape[1], N_CHUNK):
        vc = _dot(h, wv_ref[:, n0:n0 + N_CHUNK])
        for j in range(N_CHUNK // DIFF_V_DIM):
            head = n0 // DIFF_V_DIM + j
            vt_ref[head] = vc[:, j * DIFF_V_DIM:(j + 1) * DIFF_V_DIM].T.astype(BF16)
    del hn


def _kv_proj(x2, g, wk, wv, k_aug, batch, seq):
    d = x2.shape[1]
    nt = seq // ROW_TILE
    hn = DIFF_HEADS
    return pl.pallas_call(
        _kv_kernel,
        grid=(batch, nt),
        in_specs=[
            pl.BlockSpec((ROW_TILE, d), lambda b, i: (b * nt + i, 0)),
            pl.BlockSpec((1, d), lambda b, i: (0, 0)),
            pl.BlockSpec(wk.shape, lambda b, i: (0, 0)),
            pl.BlockSpec(wv.shape, lambda b, i: (0, 0)),
            pl.BlockSpec((hn, ROW_TILE, QK_PAD), lambda b, i: (0, i, 0)),
        ],
        out_specs=[
            pl.BlockSpec((None, hn, 2, ROW_TILE, QK_PAD), lambda b, i: (b, 0, 0, i, 0)),
            pl.BlockSpec((None, hn, DIFF_V_DIM, ROW_TILE), lambda b, i: (b, 0, 0, i)),
        ],
        out_shape=[
            jax.ShapeDtypeStruct((batch, hn, 2, seq, QK_PAD), BF16),
            jax.ShapeDtypeStruct((batch, hn, DIFF_V_DIM, seq), BF16),
        ],
        compiler_params=_cparams(("parallel", "parallel")),
        name="kv_proj",
    )(x2, g, wk, wv, k_aug)


def _b_in_kernel(x_ref, g_ref, wq_ref, wg_ref, aug_ref, q_ref, gate_ref):
    h = _rms(x_ref[...], g_ref[...]).astype(BF16)
    for n0 in range(0, wq_ref.shape[1], N_CHUNK):
        qc = _dot(h, wq_ref[:, n0:n0 + N_CHUNK]) * QK_SCALE
        for j in range(N_CHUNK // QK_PAD):
            idx = n0 // QK_PAD + j
            q_ref[idx // 2, idx % 2] = (qc[:, j * QK_PAD:(j + 1) * QK_PAD] + aug_ref[...]).astype(BF16)
    for n0 in range(0, wg_ref.shape[1], N_CHUNK):
        gate_ref[:, n0:n0 + N_CHUNK] = _dot(h, wg_ref[:, n0:n0 + N_CHUNK]).astype(BF16)


def _b_in_proj(x2, g, wq, wg, q_aug, batch, seq):
    d = x2.shape[1]
    nt = seq // ROW_TILE
    hn = DIFF_HEADS
    return pl.pallas_call(
        _b_in_kernel,
        grid=(batch, nt),
        in_specs=[
            pl.BlockSpec((ROW_TILE, d), lambda b, i: (b * nt + i, 0)),
            pl.BlockSpec((1, d), lambda b, i: (0, 0)),
            pl.BlockSpec(wq.shape, lambda b, i: (0, 0)),
            pl.BlockSpec(wg.shape, lambda b, i: (0, 0)),
            pl.BlockSpec((ROW_TILE, QK_PAD), lambda b, i: (i, 0)),
        ],
        out_specs=[
            pl.BlockSpec((None, hn, 2, ROW_TILE, QK_PAD), lambda b, i: (b, 0, 0, i, 0)),
            pl.BlockSpec((ROW_TILE, DIFF_WIDTH), lambda b, i: (b * nt + i, 0)),
        ],
        out_shape=[
            jax.ShapeDtypeStruct((batch, hn, 2, seq, QK_PAD), BF16),
            jax.ShapeDtypeStruct((batch * seq, DIFF_WIDTH), BF16),
        ],
        compiler_params=_cparams(("parallel", "parallel")),
        name="attn_in_proj",
    )(x2, g, wq, wg, q_aug)


def _attn_kernel(lam_ref, q_ref, k_ref, vt_ref, dbias_ref, subg_ref, o_ref,
                 s_a, s_b, p_a, p_b, p_last, m_scr, acc_scr, l_scr, al_scr, *, lambda_init):
    tq, tk = ATT_TQ, ATT_TK
    seq = q_ref.shape[2]
    places = tk // tq
    assert places * 2 * ATT_UNROLL >= seq // tq
    streams = [(hh, i) for hh in range(ATT_HEADS) for i in range(2)]
    n_qt = seq // tq
    lf = lam_ref[...]
    lam = (jnp.exp(jnp.sum(lf[0:1, :] * lf[1:2, :], axis=-1, keepdims=True))
           - jnp.exp(jnp.sum(lf[2:3, :] * lf[3:4, :], axis=-1, keepdims=True)) + lambda_init)
    neg = jnp.float32(-1e30)

    def fold(x, op):
        return op(x.reshape(x.shape[0] // SUBLANES, SUBLANES, tq), axis=0)

    def load_q(qi):
        q0 = pl.multiple_of(qi * tq, tq)
        return [q_ref[hh, i, pl.ds(q0, tq), :] for hh, i in streams]

    def key_rows(t, rows=tk):
        return pl.ds(t * tk if isinstance(t, int) else pl.multiple_of(t * tk, tk), rows)

    def scores_into(qs, t, s_buf, rows=tk, want_max=True):
        mparts = []
        for n, (hh, i) in enumerate(streams):
            s = _dot_nt(k_ref[hh, i, key_rows(t, rows), :], qs[n])
            s_buf[n, 0:rows, :] = s
            if want_max:
                mparts.append(fold(s, jnp.max))
        return tuple(mparts)

    def stage_c(t, p_buf, alphas, accs, rows=tk):
        vts = [vt_ref[hh, :, key_rows(t, rows)] for hh in range(ATT_HEADS)]
        return tuple(alphas[n] * accs[n] + _dot(vts[hh], p_buf[n, 0:rows, :])
                     for n, (hh, _) in enumerate(streams))

    def flush_previous(qi, old_place):
        q_old = jnp.maximum(qi - 1, 0)
        o0 = pl.multiple_of(q_old * tq, tq)
        slot = q_old % 2
        accs = stage_c(q_old // (tk // tq), p_last.at[slot],
                       tuple(al_scr[slot, n, 0:1, :] for n in range(len(streams))),
                       tuple(acc_scr[slot, n] for n in range(len(streams))),
                       rows=(old_place + 1) * tq)
        for hh in range(ATT_HEADS):
            l1 = jnp.sum(l_scr[slot, 2 * hh], axis=0, keepdims=True)
            l2 = jnp.sum(l_scr[slot, 2 * hh + 1], axis=0, keepdims=True)
            out_t = accs[2 * hh] / l1 - lam * (accs[2 * hh + 1] / l2)
            out = _rms(out_t.T, subg_ref[...]) * (1.0 - lambda_init)
            o_ref[pl.ds(o0, tq), hh * DIFF_V_DIM:(hh + 1) * DIFF_V_DIM] = out.astype(BF16)
        return tuple(jnp.zeros((DIFF_V_DIM, tq), F32) for _ in streams)

    def q_tile(qi, carry):
        n_full = qi // places
        which = qi % places
        qs = load_q(qi)

        def stage_a(t, s_buf, rows=tk, want_max=True):
            return scores_into(qs, t, s_buf, rows, want_max)

        def stage_b(s_buf, p_buf, mparts, ms, ls, place=None):
            diag = place is not None
            rows = (place + 1) * tq if diag else tk
            new_m, new_l, alphas = [], [], []
            for n, (hh, _) in enumerate(streams):
                def scores(r0):
                    s = s_buf[n, r0:r0 + ATT_ROWS, :]
                    return s + dbias_ref[hh, place, r0:r0 + ATT_ROWS, :] if diag else s

                if diag:
                    mpart = fold(scores(0), jnp.max)
                    for r0 in range(ATT_ROWS, rows, ATT_ROWS):
                        mpart = jnp.maximum(mpart, fold(scores(r0), jnp.max))
                else:
                    mpart = mparts[n]
                m_new = jnp.maximum(ms[n], jnp.max(mpart, axis=0, keepdims=True))
                alpha = jnp.exp2(ms[n] - m_new)
                lsum = alpha * ls[n]
                for r0 in range(0, rows, ATT_ROWS):
                    p = jnp.exp2(scores(r0) - m_new)
                    lsum = lsum + fold(p, jnp.sum)
                    p_buf[n, r0:r0 + ATT_ROWS, :] = p.astype(BF16)
                new_l.append(lsum)
                new_m.append(m_new)
                alphas.append(alpha)
            return tuple(new_m), tuple(new_l), tuple(alphas)

        def step(j, bufs, st, place, last):
            s_cur, s_nxt, p_cur, p_prv = bufs
            mparts, ms, ls, alphas, accs = st
            if j + 1 == last:
                mparts_next = stage_a(j + 1, s_nxt, (place + 1) * tq, False)
            else:
                mparts_next = stage_a(j + 1, s_nxt)
            ms, ls, alphas_new = stage_b(s_cur, p_cur, mparts, ms, ls)
            accs = (flush_previous(qi, (place - 1) % places) if j == 0
                    else stage_c(j - 1, p_prv, alphas, accs))
            return mparts_next, ms, ls, alphas_new, accs

        def steps(j0, j1, st, place, last=None):
            for j in range(j0, j1):
                st = step(j, odd if j % 2 else even, st, place, last)
            return st

        def finish(v, st, place):
            s_cur, _, _, p_prv = odd if v % 2 else even
            _, ms, ls, alphas, accs = st
            accs = (flush_previous(qi, (place - 1) % places) if v == 0
                    else stage_c(v - 1, p_prv, alphas, accs))
            slot = qi % 2
            _, ls, alphas_new = stage_b(s_cur, p_last.at[slot], None, ms, ls, place)
            nxt = scores_into(load_q(jnp.minimum(qi + 1, n_qt - 1)), 0, s_a)
            for n in range(len(streams)):
                m_scr[n] = nxt[n]
                acc_scr[slot, n] = accs[n]
                l_scr[slot, n] = ls[n]
                al_scr[slot, n] = jnp.broadcast_to(alphas_new[n], (SUBLANES, tq))

        even = (s_a, s_b, p_a, p_b)
        odd = (s_b, s_a, p_b, p_a)
        init = (tuple(m_scr[n] for n in range(len(streams))),
                tuple(jnp.full((1, tq), neg, F32) for _ in streams),
                tuple(jnp.zeros((SUBLANES, tq), F32) for _ in streams),
                tuple(jnp.ones((1, tq), F32) for _ in streams),
                tuple(jnp.zeros((DIFF_V_DIM, tq), F32) for _ in streams))
        for place in range(places):
            @pl.when(which == place)
            def _(place=place):
                st = init
                if ATT_UNROLL < n_qt // places:
                    st = lax.fori_loop(0, n_full // ATT_UNROLL,
                                       lambda _, c: steps(0, ATT_UNROLL, c, place), init)
                for v in range(n_qt // places):
                    @pl.when(n_full == v)
                    def _(v=v):
                        lo = v - v % ATT_UNROLL if v >= ATT_UNROLL else 0
                        finish(v, steps(lo, v, st, place, last=v), place)

        return carry

    first = scores_into(load_q(0), 0, s_a)
    for n in range(len(streams)):
        m_scr[n] = first[n]
        acc_scr[0, n] = jnp.zeros((DIFF_V_DIM, tq), F32)
        l_scr[0, n] = jnp.ones((SUBLANES, tq), F32)
        al_scr[0, n] = jnp.ones((SUBLANES, tq), F32)
    p_last[0] = jnp.zeros(p_last.shape[1:], BF16)
    lax.fori_loop(0, n_qt, q_tile, 0)
    flush_previous(n_qt, (n_qt - 1) % places)


def _attention(lam_qk, q, k, vt, dbias, sub_g, lambda_init):
    batch, hn, _, seq, _ = q.shape
    return pl.pallas_call(
        functools.partial(_attn_kernel, lambda_init=lambda_init),
        grid=(batch, hn // ATT_HEADS),
        in_specs=[
            pl.BlockSpec(lam_qk.shape, lambda b, h: (0, 0)),
            pl.BlockSpec((None, ATT_HEADS, 2, seq, QK_PAD), lambda b, h: (b, h, 0, 0, 0)),
            pl.BlockSpec((None, ATT_HEADS, 2, seq, QK_PAD), lambda b, h: (b, h, 0, 0, 0)),
            pl.BlockSpec((None, ATT_HEADS, DIFF_V_DIM, seq), lambda b, h: (b, h, 0, 0)),
            pl.BlockSpec((ATT_HEADS, ATT_TK // ATT_TQ, ATT_TK, ATT_TQ), lambda b, h: (h, 0, 0, 0)),
            pl.BlockSpec((1, DIFF_V_DIM), lambda b, h: (0, 0)),
        ],
        out_specs=pl.BlockSpec((None, seq, ATT_HEADS * DIFF_V_DIM), lambda b, h: (b, 0, h)),
        out_shape=jax.ShapeDtypeStruct((batch, seq, DIFF_WIDTH), BF16),
        scratch_shapes=[
            pltpu.VMEM((2 * ATT_HEADS, ATT_TK, ATT_TQ), F32),
            pltpu.VMEM((2 * ATT_HEADS, ATT_TK, ATT_TQ), F32),
            pltpu.VMEM((2 * ATT_HEADS, ATT_TK, ATT_TQ), BF16),
            pltpu.VMEM((2 * ATT_HEADS, ATT_TK, ATT_TQ), BF16),
            pltpu.VMEM((2, 2 * ATT_HEADS, ATT_TK, ATT_TQ), BF16),
            pltpu.VMEM((2 * ATT_HEADS, SUBLANES, ATT_TQ), F32),
            pltpu.VMEM((2, 2 * ATT_HEADS, DIFF_V_DIM, ATT_TQ), F32),
            pltpu.VMEM((2, 2 * ATT_HEADS, SUBLANES, ATT_TQ), F32),
            pltpu.VMEM((2, 2 * ATT_HEADS, SUBLANES, ATT_TQ), F32),
        ],
        compiler_params=_cparams(("parallel", "parallel")),
        name="diff_attention",
    )(lam_qk, q, k, vt, dbias, sub_g)


def _np_split3(a):
    a = a.astype(np.float32)
    a1 = a.astype(ml_dtypes.bfloat16).astype(np.float32)
    r1 = a - a1
    a2 = r1.astype(ml_dtypes.bfloat16).astype(np.float32)
    a3 = (r1 - a2).astype(ml_dtypes.bfloat16).astype(np.float32)
    return a1, a2, a3


def _alibi_tables(seq):
    hn = DIFF_HEADS
    slopes = np.float32(2.0) ** (-np.float32(ALIBI_MAX_EXP) * np.arange(1, hn + 1, dtype=np.float32)
                                 / np.float32(hn))
    slopes = (slopes.astype(np.float64) * LOG2E).astype(np.float32)
    pos = np.arange(seq, dtype=np.float32)
    kp = _np_split3(slopes[:, None] * pos[None, :])
    sl = _np_split3(np.broadcast_to(slopes[:, None], (hn, seq)))
    pos_hi = np.floor(pos / 256.0) * 256.0
    pos_lo = pos - pos_hi
    ones = np.ones((seq,), np.float32)
    k_cols = list(kp)
    q_cols = [ones, ones, ones]
    for t in sl:
        k_cols += [-t, -t]
        q_cols += [pos_hi, pos_lo]
    lane_pad = (DIFF_HEAD_DIM, QK_PAD - DIFF_HEAD_DIM - len(k_cols))
    k_aug = jnp.pad(jnp.asarray(np.stack(k_cols, axis=-1)), ((0, 0), (0, 0), lane_pad))
    q_aug = jnp.pad(jnp.asarray(np.stack(q_cols, axis=-1)), ((0, 0), lane_pad))
    kk = np.arange(ATT_TK)[None, :, None]
    qq = np.arange(ATT_TQ)[None, None, :] + ATT_TQ * np.arange(ATT_TK // ATT_TQ)[:, None, None]
    allowed = (kk // CHUNK) <= (qq // CHUNK)
    fix = np.where(kk > qq, -2.0 * (kk - qq), 0.0).astype(np.float32)
    dbias = np.where(allowed[None], slopes[:, None, None, None] * fix[None], -np.inf).astype(np.float32)
    return k_aug, q_aug, jnp.asarray(dbias)


def _pad_heads(w, n_blocks):
    d = w.shape[0]
    w = w.reshape(d, n_blocks, DIFF_HEAD_DIM)
    w = jnp.concatenate([w, jnp.zeros_like(w)], axis=-1)
    return w.reshape(d, n_blocks * QK_PAD).astype(BF16)


def kernel(x, a_norm_g, a_w_in, a_conv_w, a_conv_b, a_dt_bias, a_a_log, a_d_skip, a_gate_norm_g,
           a_w_out, kv_norm_g, w_kv, b_norm_g, b_w_in, b_lambda, b_sub_g, b_w_out, final_norm_g):
    batch, seq, d = x.shape
    t = batch * seq
    hn = SSM_HEADS
    x2 = x.reshape(t, d)

    for i in range(N_A_LAYERS):
        w_in = a_w_in[i]
        n_main = SSM_D_INNER + SSM_CONV_DIM
        w_main = jnp.concatenate([0.5 * w_in[:, :SSM_D_INNER], w_in[:, SSM_D_INNER:n_main]], axis=1).astype(BF16)
        w_dt = jnp.pad(w_in[:, n_main:], ((0, 0), (0, LANES - hn))).astype(BF16)
        zx, dt_raw = _a_in_proj(x2, a_norm_g[i][None], w_main, w_dt, 0.5 * a_conv_w[i],
                                0.5 * a_conv_b[i][None], seq)
        dt_row = dt_raw[:, :hn].reshape(batch, seq, hn).transpose(0, 2, 1)
        hp_row = jnp.zeros((SUBLANES, hn), F32).at[0].set(a_dt_bias[i]).at[1].set(a_a_log[i]).at[2].set(a_d_skip[i])
        hp_col = hp_row.T
        y = _ssd(zx, dt_raw, dt_row, hp_row, hp_col, a_gate_norm_g[i][None], batch, seq)
        x2 = _out_proj(y, a_w_out[i].astype(BF16), x2)

    k_aug, q_aug, dbias = _alibi_tables(seq)
    n_k = DIFF_HEADS * 2 * DIFF_HEAD_DIM
    wk = _pad_heads(w_kv[:, :n_k], DIFF_HEADS * 2)
    wv = w_kv[:, n_k:].astype(BF16)
    k, vt = _kv_proj(x2, kv_norm_g[None], wk, wv, k_aug, batch, seq)

    for j in range(N_B_LAYERS):
        layer = N_A_LAYERS + j
        lambda_init = 0.8 - 0.6 * math.exp(-0.3 * layer)
        wq = _pad_heads(b_w_in[j][:, :DIFF_WIDTH], DIFF_HEADS * 2)
        wg = b_w_in[j][:, DIFF_WIDTH:].astype(BF16)
        q, gate = _b_in_proj(x2, b_norm_g[j][None], wq, wg, q_aug, batch, seq)
        attn = _attention(b_lambda[j], q, k, vt, dbias, b_sub_g[j][None], lambda_init)
        x2 = _gated_out_proj(attn.reshape(t, DIFF_WIDTH), gate, b_w_out[j].astype(BF16), x2,
                             final_norm_g[None], final_norm=(j == N_B_LAYERS - 1))
    return x2.reshape(batch, seq, d)
```

```python
import functools
import math

import jax
import jax.numpy as jnp
import ml_dtypes
import numpy as np
from jax import lax
from jax.experimental import pallas as pl
from jax.experimental.pallas import tpu as pltpu

F32 = jnp.float32
BF16 = jnp.bfloat16

D_MODEL = 1024
DEPTH = 4
CHUNK = 64
N_B_LAYERS = DEPTH // 2
N_A_LAYERS = DEPTH - N_B_LAYERS
SSM_D_INNER = 2 * D_MODEL
SSM_HEAD_DIM = 64
SSM_HEADS = SSM_D_INNER // SSM_HEAD_DIM
SSM_GROUPS = 8
SSM_HPG = SSM_HEADS // SSM_GROUPS
SSM_STATE = 128
SSM_CONV = 4
SSM_GN = SSM_GROUPS * SSM_STATE
SSM_CONV_DIM = SSM_D_INNER + 2 * SSM_GN
DIFF_HEAD_DIM = 64
DIFF_V_DIM = 2 * DIFF_HEAD_DIM
DIFF_HEADS = D_MODEL // DIFF_V_DIM
DIFF_WIDTH = DIFF_HEADS * DIFF_V_DIM
ALIBI_MAX_EXP = 8.0
EPS = 1e-5
LOG2E = math.log2(math.e)
QK_SCALE = LOG2E / math.sqrt(DIFF_HEAD_DIM)

LANES = 128
SUBLANES = 8
VMEM_LIMIT = 56 * 1024 * 1024

ROW_TILE = 512
SSD_Q = 256
ATT_TQ = 256
ATT_TK = 512
ATT_UNROLL = 4
ATT_HEADS = 2
ATT_ROWS = 64
ATT_VROWS = DIFF_V_DIM + 16
N_CHUNK = 256
CONV_GROUPS = 8
QK_PAD = 2 * DIFF_HEAD_DIM


def _cparams(sem):
    return pltpu.CompilerParams(dimension_semantics=sem, vmem_limit_bytes=VMEM_LIMIT)


def _rms(x, g):
    return x * lax.rsqrt(jnp.mean(x * x, axis=-1, keepdims=True) + EPS) * g


def _silu_of_half(h):
    return h * jnp.tanh(h) + h


def _silu(x):
    return _silu_of_half(0.5 * x)


def _softplus(x):
    return jnp.maximum(x, 0.0) + jnp.log1p(jnp.exp(-jnp.abs(x)))


def _trunc_bf16(a):
    bits = lax.bitcast_convert_type(a, jnp.int32) & jnp.int32(-65536)
    return lax.bitcast_convert_type(bits, F32)


def _split3(a):
    a1 = _trunc_bf16(a)
    r1 = a - a1
    a2 = _trunc_bf16(r1)
    return a1.astype(BF16), a2.astype(BF16), (r1 - a2).astype(BF16)


def _dot(a, b):
    return jnp.dot(a, b, preferred_element_type=F32)


def _dot_nt(a, b):
    return lax.dot_general(a, b, (((1,), (1,)), ((), ())), preferred_element_type=F32)


def _dot_tn(a, b):
    return lax.dot_general(a, b, (((0,), (0,)), ((), ())), preferred_element_type=F32)


def _a_in_kernel(x_ref, g_ref, w_ref, wdt_ref, cw_ref, cb_ref, o_ref, dt_ref, tail_ref,
                 *, tiles_per_seq):
    rows = x_ref.shape[0]
    di = SSM_D_INNER
    n_main = o_ref.shape[-1]

    @pl.when(pl.program_id(0) % tiles_per_seq == 0)
    def _():
        tail_ref[...] = jnp.zeros_like(tail_ref)

    h = _rms(x_ref[...], g_ref[...]).astype(BF16)
    sub = lax.broadcasted_iota(jnp.int32, (1, SUBLANES, N_CHUNK), 1)
    z_chunks = list(range(0, di, N_CHUNK))
    conv_chunks = list(range(di, n_main, N_CHUNK))
    order = []
    while z_chunks or conv_chunks:
        order += conv_chunks[:2] + z_chunks[:1]
        conv_chunks, z_chunks = conv_chunks[2:], z_chunks[1:]
    nxt = _dot(h, w_ref[:, order[0]:order[0] + N_CHUNK])
    for idx, n0 in enumerate(order):
        a = nxt
        if idx + 1 < len(order):
            nxt = _dot(h, w_ref[:, order[idx + 1]:order[idx + 1] + N_CHUNK])
        if n0 < di:
            o_ref[:, n0:n0 + N_CHUNK] = _silu_of_half(a).astype(BF16)
            continue
        c0 = n0 - di
        tail = tail_ref[:, c0:c0 + N_CHUNK][None]
        tail_ref[:, c0:c0 + N_CHUNK] = a[rows - SUBLANES:rows, :]
        a3 = a.reshape(rows // SUBLANES, SUBLANES, N_CHUNK)
        for g0 in range(0, rows // SUBLANES, CONV_GROUPS):
            rot = a3[g0:g0 + CONV_GROUPS]
            prev = tail if g0 == 0 else a3[g0 - 1:g0]
            acc = cb_ref[:, c0:c0 + N_CHUNK] + cw_ref[SSM_CONV - 1:SSM_CONV, c0:c0 + N_CHUNK] * rot
            for j in range(1, SSM_CONV):
                rot = pltpu.roll(rot, 1, axis=1)
                prev = pltpu.roll(prev, 1, axis=1)
                above = jnp.concatenate([prev, rot[:-1]], axis=0)
                k = SSM_CONV - 1 - j
                acc = acc + cw_ref[k:k + 1, c0:c0 + N_CHUNK] * jnp.where(sub >= j, rot, above)
            r0 = g0 * SUBLANES
            o_ref[r0:r0 + CONV_GROUPS * SUBLANES, n0:n0 + N_CHUNK] = (
                _silu_of_half(acc).reshape(CONV_GROUPS * SUBLANES, N_CHUNK).astype(BF16))
    dt_ref[...] = _dot(h, wdt_ref[...])


def _a_in_proj(x2, g, w_main, w_dt, conv_w, conv_b, seq):
    t, d = x2.shape
    n_main = w_main.shape[1]
    return pl.pallas_call(
        functools.partial(_a_in_kernel, tiles_per_seq=seq // ROW_TILE),
        grid=(t // ROW_TILE,),
        in_specs=[
            pl.BlockSpec((ROW_TILE, d), lambda i: (i, 0)),
            pl.BlockSpec((1, d), lambda i: (0, 0)),
            pl.BlockSpec((d, n_main), lambda i: (0, 0)),
            pl.BlockSpec((d, LANES), lambda i: (0, 0)),
            pl.BlockSpec((SSM_CONV, SSM_CONV_DIM), lambda i: (0, 0)),
            pl.BlockSpec((1, SSM_CONV_DIM), lambda i: (0, 0)),
        ],
        out_specs=[
            pl.BlockSpec((ROW_TILE, n_main), lambda i: (i, 0)),
            pl.BlockSpec((ROW_TILE, LANES), lambda i: (i, 0)),
        ],
        out_shape=[
            jax.ShapeDtypeStruct((t, n_main), BF16),
            jax.ShapeDtypeStruct((t, LANES), F32),
        ],
        scratch_shapes=[pltpu.VMEM((SUBLANES, SSM_CONV_DIM), F32)],
        compiler_params=_cparams(("arbitrary",)),
        name="mamba_in_proj",
    )(x2, g, w_main, w_dt, conv_w, conv_b)


def _ssd_kernel(zx_ref, dtc_ref, dtr_ref, hp_row_ref, hp_col_ref, gng_ref, o_ref, state_ref):
    q = zx_ref.shape[0]
    g_n, r_n, p_n, n_n = SSM_GROUPS, SSM_HPG, SSM_HEAD_DIM, SSM_STATE
    di = SSM_D_INNER
    hn = SSM_HEADS
    gw = r_n * p_n

    @pl.when(pl.program_id(1) == 0)
    def _():
        state_ref[...] = jnp.zeros_like(state_ref)

    dtb_row, alog_row, dskip_row = hp_row_ref[0:1, :], hp_row_ref[1:2, :], hp_row_ref[2:3, :]
    dtb_col, alog_col = hp_col_ref[:, 0:1], hp_col_ref[:, 1:2]
    dt_c = _softplus(dtc_ref[:, 0:hn] + dtb_row)
    adt_c = dt_c * (-jnp.exp(alog_row))
    dt_r = _softplus(dtr_ref[...] + dtb_col)
    adt_r = dt_r * (-jnp.exp(alog_col))
    ri = lax.broadcasted_iota(jnp.int32, (q, q), 0)
    ci = lax.broadcasted_iota(jnp.int32, (q, q), 1)
    causal = ri >= ci
    tril = jnp.where(causal, 1.0, 0.0).astype(BF16)
    triu = jnp.where(ri <= ci, 1.0, 0.0).astype(BF16)
    acs_c = sum(_dot(tril, part) for part in _split3(adt_c))
    acs_r = sum(_dot(part, triu) for part in _split3(adt_r))
    tot_row = acs_c[q - 1:q, :]
    eacs_c = jnp.exp(acs_c)
    toend_c = jnp.exp(tot_row - acs_c) * dt_c
    etot_row = jnp.exp(tot_row)
    acs2_c = acs_c * LOG2E
    src2_r = (acs_r - jnp.log(dt_r)) * LOG2E

    lane_head = lax.broadcasted_iota(jnp.int32, (hn, gw), 1) // p_n
    row_head = lax.broadcasted_iota(jnp.int32, (hn, gw), 0)
    toend16 = toend_c.astype(BF16)
    eacs16 = eacs_c.astype(BF16)
    lane_head_row = lane_head[0:1, :]
    head_mask16 = [jnp.where(lane_head_row == r, 1.0, 0.0).astype(BF16) for r in range(r_n)]

    def per_head_row(row, g):
        out = jnp.broadcast_to(row[:, g * r_n:g * r_n + 1], (1, gw))
        for r in range(1, r_n):
            hval = jnp.broadcast_to(row[:, g * r_n + r:g * r_n + r + 1], (1, gw))
            out = jnp.where(lane_head_row == r, hval, out)
        return out

    for g in range(g_n):
        spread = jnp.where(row_head == lane_head + g * r_n, 1.0, 0.0).astype(BF16)
        xs16 = zx_ref[:, di + g * gw:di + (g + 1) * gw]
        bm16 = zx_ref[:, 2 * di + g * n_n:2 * di + (g + 1) * n_n]
        cm16 = zx_ref[:, 2 * di + SSM_GN + g * n_n:2 * di + SSM_GN + (g + 1) * n_n]
        xs = xs16.astype(F32)
        cb = _dot_nt(cm16, bm16)
        st = state_ref[g]
        y = _dot(cm16, st.astype(BF16)) * _dot(eacs16, spread)
        for r in range(r_n):
            h = g * r_n + r
            seg2 = jnp.broadcast_to(acs2_c[:, h:h + 1], (q, q)) - src2_r[h:h + 1, :]
            lmat = (cb * jnp.where(causal, jnp.exp2(seg2), 0.0)).astype(BF16)
            y = y + _dot(lmat, xs16 * head_mask16[r])
        y = y + per_head_row(dskip_row, g) * xs
        xend16 = (xs * _dot(toend16, spread)).astype(BF16)
        state_ref[g] = st * per_head_row(etot_row, g) + _dot_tn(bm16, xend16)
        u = y * zx_ref[:, g * gw:(g + 1) * gw].astype(F32)
        u = u * lax.rsqrt(jnp.mean(u * u, axis=-1, keepdims=True) + EPS)
        o_ref[:, g * gw:(g + 1) * gw] = (u * gng_ref[:, g * gw:(g + 1) * gw]).astype(BF16)


def _ssd(zx, dt_col, dt_row, hp_row, hp_col, gate_g, batch, seq):
    q = SSD_Q
    nc = seq // q
    n_main = zx.shape[1]
    hn = SSM_HEADS
    return pl.pallas_call(
        _ssd_kernel,
        grid=(batch, nc),
        in_specs=[
            pl.BlockSpec((q, n_main), lambda b, c: (b * nc + c, 0)),
            pl.BlockSpec((q, LANES), lambda b, c: (b * nc + c, 0)),
            pl.BlockSpec((None, hn, q), lambda b, c: (b, 0, c)),
            pl.BlockSpec((SUBLANES, hn), lambda b, c: (0, 0)),
            pl.BlockSpec((hn, SUBLANES), lambda b, c: (0, 0)),
            pl.BlockSpec((1, SSM_D_INNER), lambda b, c: (0, 0)),
        ],
        out_specs=pl.BlockSpec((q, SSM_D_INNER), lambda b, c: (b * nc + c, 0)),
        out_shape=jax.ShapeDtypeStruct((batch * seq, SSM_D_INNER), BF16),
        scratch_shapes=[
            pltpu.VMEM((SSM_GROUPS, SSM_STATE, SSM_HPG * SSM_HEAD_DIM), F32),
        ],
        compiler_params=_cparams(("parallel", "arbitrary")),
        name="mamba_ssd",
    )(zx, dt_col, dt_row, hp_row, hp_col, gate_g)


def _out_proj_kernel(y_ref, w_ref, x_ref, o_ref):
    o_ref[...] = x_ref[...] + _dot(y_ref[...], w_ref[...])


def _out_proj(y, w, x2):
    t, k = y.shape
    d = w.shape[1]
    return pl.pallas_call(
        _out_proj_kernel,
        grid=(t // ROW_TILE,),
        in_specs=[
            pl.BlockSpec((ROW_TILE, k), lambda i: (i, 0)),
            pl.BlockSpec((k, d), lambda i: (0, 0)),
            pl.BlockSpec((ROW_TILE, d), lambda i: (i, 0)),
        ],
        out_specs=pl.BlockSpec((ROW_TILE, d), lambda i: (i, 0)),
        out_shape=jax.ShapeDtypeStruct((t, d), F32),
        compiler_params=_cparams(("parallel",)),
        name="mamba_out_proj",
    )(y, w, x2)


def _gated_out_proj_kernel(a_ref, gate_ref, w_ref, x_ref, fg_ref, o_ref, *, final_norm):
    gate = gate_ref[...].astype(F32)
    y = (a_ref[...].astype(F32) * _silu(gate)).astype(BF16)
    out = x_ref[...] + _dot(y, w_ref[...])
    if final_norm:
        out = _rms(out, fg_ref[...])
    o_ref[...] = out


def _gated_out_proj(attn, gate, w, x2, final_g, final_norm):
    t, k = attn.shape
    d = w.shape[1]
    return pl.pallas_call(
        functools.partial(_gated_out_proj_kernel, final_norm=final_norm),
        grid=(t // ROW_TILE,),
        in_specs=[
            pl.BlockSpec((ROW_TILE, k), lambda i: (i, 0)),
            pl.BlockSpec((ROW_TILE, k), lambda i: (i, 0)),
            pl.BlockSpec((k, d), lambda i: (0, 0)),
            pl.BlockSpec((ROW_TILE, d), lambda i: (i, 0)),
            pl.BlockSpec((1, d), lambda i: (0, 0)),
        ],
        out_specs=pl.BlockSpec((ROW_TILE, d), lambda i: (i, 0)),
        out_shape=jax.ShapeDtypeStruct((t, d), F32),
        compiler_params=_cparams(("parallel",)),
        name="attn_out_proj",
    )(attn, gate, w, x2, final_g)


def _kv_kernel(x_ref, g_ref, wk_ref, wv_ref, aug_ref, k_ref, vt_ref):
    h = _rms(x_ref[...], g_ref[...]).astype(BF16)
    hn = DIFF_HEADS
    for n0 in range(0, wk_ref.shape[1], N_CHUNK):
        kc = _dot(h, wk_ref[:, n0:n0 + N_CHUNK])
        for j in range(N_CHUNK // QK_PAD):
            idx = n0 // QK_PAD + j
            k_ref[idx // 2, idx % 2] = (kc[:, j * QK_PAD:(j + 1) * QK_PAD]
                                        + aug_ref[idx // 2]).astype(BF16)
    for n0 in range(0, wv_ref.shape[1], N_CHUNK):
        vc = _dot(h, wv_ref[:, n0:n0 + N_CHUNK])
        for j in range(N_CHUNK // DIFF_V_DIM):
            head = n0 // DIFF_V_DIM + j
            vt_ref[head, 0:DIFF_V_DIM, :] = vc[:, j * DIFF_V_DIM:(j + 1) * DIFF_V_DIM].T.astype(BF16)
    extra = (ATT_VROWS - DIFF_V_DIM, vt_ref.shape[-1])
    ones_row = jnp.where(lax.broadcasted_iota(jnp.int32, extra, 0) == 0, 1.0, 0.0).astype(BF16)
    for head in range(hn):
        vt_ref[head, DIFF_V_DIM:ATT_VROWS, :] = ones_row


def _kv_proj(x2, g, wk, wv, k_aug, batch, seq):
    d = x2.shape[1]
    nt = seq // ROW_TILE
    hn = DIFF_HEADS
    return pl.pallas_call(
        _kv_kernel,
        grid=(batch, nt),
        in_specs=[
            pl.BlockSpec((ROW_TILE, d), lambda b, i: (b * nt + i, 0)),
            pl.BlockSpec((1, d), lambda b, i: (0, 0)),
            pl.BlockSpec(wk.shape, lambda b, i: (0, 0)),
            pl.BlockSpec(wv.shape, lambda b, i: (0, 0)),
            pl.BlockSpec((hn, ROW_TILE, QK_PAD), lambda b, i: (0, i, 0)),
        ],
        out_specs=[
            pl.BlockSpec((None, hn, 2, ROW_TILE, QK_PAD), lambda b, i: (b, 0, 0, i, 0)),
            pl.BlockSpec((None, hn, ATT_VROWS, ROW_TILE), lambda b, i: (b, 0, 0, i)),
        ],
        out_shape=[
            jax.ShapeDtypeStruct((batch, hn, 2, seq, QK_PAD), BF16),
            jax.ShapeDtypeStruct((batch, hn, ATT_VROWS, seq), BF16),
        ],
        compiler_params=_cparams(("parallel", "parallel")),
        name="kv_proj",
    )(x2, g, wk, wv, k_aug)


def _b_in_kernel(x_ref, g_ref, wq_ref, wg_ref, aug_ref, q_ref, gate_ref):
    h = _rms(x_ref[...], g_ref[...]).astype(BF16)
    for n0 in range(0, wq_ref.shape[1], N_CHUNK):
        qc = _dot(h, wq_ref[:, n0:n0 + N_CHUNK]) * QK_SCALE
        for j in range(N_CHUNK // QK_PAD):
            idx = n0 // QK_PAD + j
            q_ref[idx // 2, idx % 2] = (qc[:, j * QK_PAD:(j + 1) * QK_PAD] + aug_ref[...]).astype(BF16)
    for n0 in range(0, wg_ref.shape[1], N_CHUNK):
        gate_ref[:, n0:n0 + N_CHUNK] = _dot(h, wg_ref[:, n0:n0 + N_CHUNK]).astype(BF16)


def _b_in_proj(x2, g, wq, wg, q_aug, batch, seq):
    d = x2.shape[1]
    nt = seq // ROW_TILE
    hn = DIFF_HEADS
    return pl.pallas_call(
        _b_in_kernel,
        grid=(batch, nt),
        in_specs=[
            pl.BlockSpec((ROW_TILE, d), lambda b, i: (b * nt + i, 0)),
            pl.BlockSpec((1, d), lambda b, i: (0, 0)),
            pl.BlockSpec(wq.shape, lambda b, i: (0, 0)),
            pl.BlockSpec(wg.shape, lambda b, i: (0, 0)),
            pl.BlockSpec((ROW_TILE, QK_PAD), lambda b, i: (i, 0)),
        ],
        out_specs=[
            pl.BlockSpec((None, hn, 2, ROW_TILE, QK_PAD), lambda b, i: (b, 0, 0, i, 0)),
            pl.BlockSpec((ROW_TILE, DIFF_WIDTH), lambda b, i: (b * nt + i, 0)),
        ],
        out_shape=[
            jax.ShapeDtypeStruct((batch, hn, 2, seq, QK_PAD), BF16),
            jax.ShapeDtypeStruct((batch * seq, DIFF_WIDTH), BF16),
        ],
        compiler_params=_cparams(("parallel", "parallel")),
        name="attn_in_proj",
    )(x2, g, wq, wg, q_aug)


def _attn_kernel(lam_ref, q_ref, k_ref, vt_ref, dbias_ref, subg_ref, o_ref,
                 s_a, s_b, p_a, p_b, p_last, m_scr, acc_scr, al_scr, *, lambda_init):
    tq, tk = ATT_TQ, ATT_TK
    seq = q_ref.shape[2]
    places = tk // tq
    assert places * 2 * ATT_UNROLL >= seq // tq
    streams = [(hh, i) for hh in range(ATT_HEADS) for i in range(2)]
    n_qt = seq // tq
    lf = lam_ref[...]
    lam = (jnp.exp(jnp.sum(lf[0:1, :] * lf[1:2, :], axis=-1, keepdims=True))
           - jnp.exp(jnp.sum(lf[2:3, :] * lf[3:4, :], axis=-1, keepdims=True)) + lambda_init)
    neg = jnp.float32(-1e30)

    def fold(x, op):
        return op(x.reshape(x.shape[0] // SUBLANES, SUBLANES, tq), axis=0)

    def load_q(qi):
        q0 = pl.multiple_of(qi * tq, tq)
        return [q_ref[hh, i, pl.ds(q0, tq), :] for hh, i in streams]

    def key_rows(t, rows=tk):
        return pl.ds(t * tk if isinstance(t, int) else pl.multiple_of(t * tk, tk), rows)

    def scores_into(qs, t, s_buf, rows=tk, want_max=True):
        mparts = []
        for n, (hh, i) in enumerate(streams):
            s = _dot_nt(k_ref[hh, i, key_rows(t, rows), :], qs[n])
            s_buf[n, 0:rows, :] = s
            if want_max:
                mparts.append(fold(s, jnp.max))
        return tuple(mparts)

    def stage_c(t, p_buf, alphas, accs, rows=tk):
        vts = [vt_ref[hh, :, key_rows(t, rows)] for hh in range(ATT_HEADS)]
        return tuple(alphas[n] * accs[n] + _dot(vts[hh], p_buf[n, 0:rows, :])
                     for n, (hh, _) in enumerate(streams))

    def flush_previous(qi, old_place):
        q_old = jnp.maximum(qi - 1, 0)
        o0 = pl.multiple_of(q_old * tq, tq)
        slot = q_old % 2
        accs = stage_c(q_old // (tk // tq), p_last.at[slot],
                       tuple(al_scr[slot, n, 0:1, :] for n in range(len(streams))),
                       tuple(acc_scr[slot, n] for n in range(len(streams))),
                       rows=(old_place + 1) * tq)
        e = DIFF_V_DIM
        for hh in range(ATT_HEADS):
            a1, a2 = accs[2 * hh], accs[2 * hh + 1]
            out_t = a1[0:e] / a1[e:e + 1] - lam * (a2[0:e] / a2[e:e + 1])
            out = _rms(out_t.T, subg_ref[...]) * (1.0 - lambda_init)
            o_ref[pl.ds(o0, tq), hh * e:(hh + 1) * e] = out.astype(BF16)
        return tuple(jnp.zeros((ATT_VROWS, tq), F32) for _ in streams)

    def q_tile(qi, carry):
        n_full = qi // places
        which = qi % places
        qs = load_q(qi)

        def stage_a(t, s_buf, rows=tk, want_max=True):
            return scores_into(qs, t, s_buf, rows, want_max)

        def stage_b(s_buf, p_buf, mparts, ms, ls, place=None):
            diag = place is not None
            rows = (place + 1) * tq if diag else tk
            new_m, new_l, alphas = [], [], []
            for n, (hh, _) in enumerate(streams):
                def scores(r0):
                    s = s_buf[n, r0:r0 + ATT_ROWS, :]
                    return s + dbias_ref[hh, place, r0:r0 + ATT_ROWS, :] if diag else s

                if diag:
                    mpart = fold(scores(0), jnp.max)
                    for r0 in range(ATT_ROWS, rows, ATT_ROWS):
                        mpart = jnp.maximum(mpart, fold(scores(r0), jnp.max))
                else:
                    mpart = mparts[n]
                m_new = jnp.maximum(ms[n], jnp.max(mpart, axis=0, keepdims=True))
                alpha = jnp.exp2(ms[n] - m_new)
                for r0 in range(0, rows, ATT_ROWS):
                    p = jnp.exp2(scores(r0) - m_new)
                    p_buf[n, r0:r0 + ATT_ROWS, :] = p.astype(BF16)
                new_l.append(ls[n])
                new_m.append(m_new)
                alphas.append(alpha)
            return tuple(new_m), tuple(new_l), tuple(alphas)

        def step(j, bufs, st, place, last):
            s_cur, s_nxt, p_cur, p_prv = bufs
            mparts, ms, ls, alphas, accs = st
            if j + 1 == last:
                mparts_next = stage_a(j + 1, s_nxt, (place + 1) * tq, False)
            else:
                mparts_next = stage_a(j + 1, s_nxt)
            ms, ls, alphas_new = stage_b(s_cur, p_cur, mparts, ms, ls)
            accs = (flush_previous(qi, (place - 1) % places) if j == 0
                    else stage_c(j - 1, p_prv, alphas, accs))
            return mparts_next, ms, ls, alphas_new, accs

        def steps(j0, j1, st, place, last=None):
            for j in range(j0, j1):
                st = step(j, odd if j % 2 else even, st, place, last)
            return st

        def finish(v, st, place):
            s_cur, _, _, p_prv = odd if v % 2 else even
            _, ms, ls, alphas, accs = st
            accs = (flush_previous(qi, (place - 1) % places) if v == 0
                    else stage_c(v - 1, p_prv, alphas, accs))
            slot = qi % 2
            _, ls, alphas_new = stage_b(s_cur, p_last.at[slot], None, ms, ls, place)
            nxt = scores_into(load_q(jnp.minimum(qi + 1, n_qt - 1)), 0, s_a)
            for n in range(len(streams)):
                m_scr[n] = nxt[n]
                acc_scr[slot, n] = accs[n]
                al_scr[slot, n] = jnp.broadcast_to(alphas_new[n], (SUBLANES, tq))

        even = (s_a, s_b, p_a, p_b)
        odd = (s_b, s_a, p_b, p_a)
        init = (tuple(m_scr[n] for n in range(len(streams))),
                tuple(jnp.full((1, tq), neg, F32) for _ in streams),
                tuple(jnp.zeros((SUBLANES, tq), F32) for _ in streams),
                tuple(jnp.ones((1, tq), F32) for _ in streams),
                tuple(jnp.zeros((ATT_VROWS, tq), F32) for _ in streams))
        for place in range(places):
            @pl.when(which == place)
            def _(place=place):
                st = init
                if ATT_UNROLL < n_qt // places:
                    st = lax.fori_loop(0, n_full // ATT_UNROLL,
                                       lambda _, c: steps(0, ATT_UNROLL, c, place), init)
                for v in range(n_qt // places):
                    @pl.when(n_full == v)
                    def _(v=v):
                        lo = v - v % ATT_UNROLL if v >= ATT_UNROLL else 0
                        finish(v, steps(lo, v, st, place, last=v), place)

        return carry

    first = scores_into(load_q(0), 0, s_a)
    for n in range(len(streams)):
        m_scr[n] = first[n]
        acc_scr[0, n] = jnp.where(
            lax.broadcasted_iota(jnp.int32, (ATT_VROWS, tq), 0) == DIFF_V_DIM, 1.0, 0.0)
        al_scr[0, n] = jnp.ones((SUBLANES, tq), F32)
    p_last[0] = jnp.zeros(p_last.shape[1:], BF16)
    lax.fori_loop(0, n_qt, q_tile, 0)
    flush_previous(n_qt, (n_qt - 1) % places)


def _attention(lam_qk, q, k, vt, dbias, sub_g, lambda_init):
    batch, hn, _, seq, _ = q.shape
    return pl.pallas_call(
        functools.partial(_attn_kernel, lambda_init=lambda_init),
        grid=(batch, hn // ATT_HEADS),
        in_specs=[
            pl.BlockSpec(lam_qk.shape, lambda b, h: (0, 0)),
            pl.BlockSpec((None, ATT_HEADS, 2, seq, QK_PAD), lambda b, h: (b, h, 0, 0, 0)),
            pl.BlockSpec((None, ATT_HEADS, 2, seq, QK_PAD), lambda b, h: (b, h, 0, 0, 0)),
            pl.BlockSpec((None, ATT_HEADS, ATT_VROWS, seq), lambda b, h: (b, h, 0, 0)),
            pl.BlockSpec((ATT_HEADS, ATT_TK // ATT_TQ, ATT_TK, ATT_TQ), lambda b, h: (h, 0, 0, 0)),
            pl.BlockSpec((1, DIFF_V_DIM), lambda b, h: (0, 0)),
        ],
        out_specs=pl.BlockSpec((None, seq, ATT_HEADS * DIFF_V_DIM), lambda b, h: (b, 0, h)),
        out_shape=jax.ShapeDtypeStruct((batch, seq, DIFF_WIDTH), BF16),
        scratch_shapes=[
            pltpu.VMEM((2 * ATT_HEADS, ATT_TK, ATT_TQ), F32),
            pltpu.VMEM((2 * ATT_HEADS, ATT_TK, ATT_TQ), F32),
            pltpu.VMEM((2 * ATT_HEADS, ATT_TK, ATT_TQ), BF16),
            pltpu.VMEM((2 * ATT_HEADS, ATT_TK, ATT_TQ), BF16),
            pltpu.VMEM((2, 2 * ATT_HEADS, ATT_TK, ATT_TQ), BF16),
            pltpu.VMEM((2 * ATT_HEADS, SUBLANES, ATT_TQ), F32),
            pltpu.VMEM((2, 2 * ATT_HEADS, ATT_VROWS, ATT_TQ), F32),
            pltpu.VMEM((2, 2 * ATT_HEADS, SUBLANES, ATT_TQ), F32),
        ],
        compiler_params=_cparams(("parallel", "parallel")),
        name="diff_attention",
    )(lam_qk, q, k, vt, dbias, sub_g)


def _np_split3(a):
    a = a.astype(np.float32)
    a1 = a.astype(ml_dtypes.bfloat16).astype(np.float32)
    r1 = a - a1
    a2 = r1.astype(ml_dtypes.bfloat16).astype(np.float32)
    a3 = (r1 - a2).astype(ml_dtypes.bfloat16).astype(np.float32)
    return a1, a2, a3


def _alibi_tables(seq):
    hn = DIFF_HEADS
    slopes = np.float32(2.0) ** (-np.float32(ALIBI_MAX_EXP) * np.arange(1, hn + 1, dtype=np.float32)
                                 / np.float32(hn))
    slopes = (slopes.astype(np.float64) * LOG2E).astype(np.float32)
    pos = np.arange(seq, dtype=np.float32)
    kp = _np_split3(slopes[:, None] * pos[None, :])
    sl = _np_split3(np.broadcast_to(slopes[:, None], (hn, seq)))
    pos_hi = np.floor(pos / 256.0) * 256.0
    pos_lo = pos - pos_hi
    ones = np.ones((seq,), np.float32)
    k_cols = list(kp)
    q_cols = [ones, ones, ones]
    for t in sl:
        k_cols += [-t, -t]
        q_cols += [pos_hi, pos_lo]
    lane_pad = (DIFF_HEAD_DIM, QK_PAD - DIFF_HEAD_DIM - len(k_cols))
    k_aug = jnp.pad(jnp.asarray(np.stack(k_cols, axis=-1)), ((0, 0), (0, 0), lane_pad))
    q_aug = jnp.pad(jnp.asarray(np.stack(q_cols, axis=-1)), ((0, 0), lane_pad))
    kk = np.arange(ATT_TK)[None, :, None]
    qq = np.arange(ATT_TQ)[None, None, :] + ATT_TQ * np.arange(ATT_TK // ATT_TQ)[:, None, None]
    allowed = (kk // CHUNK) <= (qq // CHUNK)
    fix = np.where(kk > qq, -2.0 * (kk - qq), 0.0).astype(np.float32)
    dbias = np.where(allowed[None], slopes[:, None, None, None] * fix[None], -np.inf).astype(np.float32)
    return k_aug, q_aug, jnp.asarray(dbias)


def _pad_heads(w, n_blocks):
    d = w.shape[0]
    w = w.reshape(d, n_blocks, DIFF_HEAD_DIM)
    w = jnp.concatenate([w, jnp.zeros_like(w)], axis=-1)
    return w.reshape(d, n_blocks * QK_PAD).astype(BF16)


def kernel(x, a_norm_g, a_w_in, a_conv_w, a_conv_b, a_dt_bias, a_a_log, a_d_skip, a_gate_norm_g,
           a_w_out, kv_norm_g, w_kv, b_norm_g, b_w_in, b_lambda, b_sub_g, b_w_out, final_norm_g):
    batch, seq, d = x.shape
    t = batch * seq
    hn = SSM_HEADS
    x2 = x.reshape(t, d)

    for i in range(N_A_LAYERS):
        w_in = a_w_in[i]
        n_main = SSM_D_INNER + SSM_CONV_DIM
        w_main = jnp.concatenate([0.5 * w_in[:, :SSM_D_INNER], w_in[:, SSM_D_INNER:n_main]], axis=1).astype(BF16)
        w_dt = jnp.pad(w_in[:, n_main:], ((0, 0), (0, LANES - hn))).astype(BF16)
        zx, dt_raw = _a_in_proj(x2, a_norm_g[i][None], w_main, w_dt, 0.5 * a_conv_w[i],
                                0.5 * a_conv_b[i][None], seq)
        dt_row = dt_raw[:, :hn].reshape(batch, seq, hn).transpose(0, 2, 1)
        hp_row = jnp.zeros((SUBLANES, hn), F32).at[0].set(a_dt_bias[i]).at[1].set(a_a_log[i]).at[2].set(a_d_skip[i])
        hp_col = hp_row.T
        y = _ssd(zx, dt_raw, dt_row, hp_row, hp_col, a_gate_norm_g[i][None], batch, seq)
        x2 = _out_proj(y, a_w_out[i].astype(BF16), x2)

    k_aug, q_aug, dbias = _alibi_tables(seq)
    n_k = DIFF_HEADS * 2 * DIFF_HEAD_DIM
    wk = _pad_heads(w_kv[:, :n_k], DIFF_HEADS * 2)
    wv = w_kv[:, n_k:].astype(BF16)
    k, vt = _kv_proj(x2, kv_norm_g[None], wk, wv, k_aug, batch, seq)

    for j in range(N_B_LAYERS):
        layer = N_A_LAYERS + j
        lambda_init = 0.8 - 0.6 * math.exp(-0.3 * layer)
        wq = _pad_heads(b_w_in[j][:, :DIFF_WIDTH], DIFF_HEADS * 2)
        wg = b_w_in[j][:, DIFF_WIDTH:].astype(BF16)
        q, gate = _b_in_proj(x2, b_norm_g[j][None], wq, wg, q_aug, batch, seq)
        attn = _attention(b_lambda[j], q, k, vt, dbias, b_sub_g[j][None], lambda_init)
        x2 = _gated_out_proj(attn.reshape(t, DIFF_WIDTH), gate, b_w_out[j].astype(BF16), x2,
                             final_norm_g[None], final_norm=(j == N_B_LAYERS - 1))
    return x2.reshape(batch, seq, d)
```

```python
import functools
import math

import jax
import jax.numpy as jnp
import ml_dtypes
import numpy as np
from jax import lax
from jax.experimental import pallas as pl
from jax.experimental.pallas import tpu as pltpu

F32 = jnp.float32
BF16 = jnp.bfloat16

D_MODEL = 1024
DEPTH = 4
CHUNK = 64
N_B_LAYERS = DEPTH // 2
N_A_LAYERS = DEPTH - N_B_LAYERS
SSM_D_INNER = 2 * D_MODEL
SSM_HEAD_DIM = 64
SSM_HEADS = SSM_D_INNER // SSM_HEAD_DIM
SSM_GROUPS = 8
SSM_HPG = SSM_HEADS // SSM_GROUPS
SSM_STATE = 128
SSM_CONV = 4
SSM_GN = SSM_GROUPS * SSM_STATE
SSM_CONV_DIM = SSM_D_INNER + 2 * SSM_GN
DIFF_HEAD_DIM = 64
DIFF_V_DIM = 2 * DIFF_HEAD_DIM
DIFF_HEADS = D_MODEL // DIFF_V_DIM
DIFF_WIDTH = DIFF_HEADS * DIFF_V_DIM
ALIBI_MAX_EXP = 8.0
EPS = 1e-5
LOG2E = math.log2(math.e)
QK_SCALE = LOG2E / math.sqrt(DIFF_HEAD_DIM)

LANES = 128
SUBLANES = 8
VMEM_LIMIT = 56 * 1024 * 1024

ROW_TILE = 512
SSD_Q = 256
ATT_TQ = 256
ATT_TK = 512
ATT_UNROLL = 4
ATT_HEADS = 2
ATT_ROWS = 64
ATT_VROWS = DIFF_V_DIM + 16
N_CHUNK = 256
CONV_GROUPS = 8
QK_PAD = 2 * DIFF_HEAD_DIM


def _cparams(sem):
    return pltpu.CompilerParams(dimension_semantics=sem, vmem_limit_bytes=VMEM_LIMIT)


def _rms(x, g):
    return x * lax.rsqrt(jnp.mean(x * x, axis=-1, keepdims=True) + EPS) * g


def _silu_of_half(h):
    return h * jnp.tanh(h) + h


def _silu(x):
    return _silu_of_half(0.5 * x)


def _softplus(x):
    return jnp.maximum(x, 0.0) + jnp.log1p(jnp.exp(-jnp.abs(x)))


def _trunc_bf16(a):
    bits = lax.bitcast_convert_type(a, jnp.int32) & jnp.int32(-65536)
    return lax.bitcast_convert_type(bits, F32)


def _split3(a):
    a1 = _trunc_bf16(a)
    r1 = a - a1
    a2 = _trunc_bf16(r1)
    return a1.astype(BF16), a2.astype(BF16), (r1 - a2).astype(BF16)


def _dot(a, b):
    return jnp.dot(a, b, preferred_element_type=F32)


def _dot_nt(a, b):
    return lax.dot_general(a, b, (((1,), (1,)), ((), ())), preferred_element_type=F32)


def _dot_tn(a, b):
    return lax.dot_general(a, b, (((0,), (0,)), ((), ())), preferred_element_type=F32)


def _a_in_kernel(x_ref, g_ref, w_ref, wdt_ref, cw_ref, cb_ref, o_ref, dt_ref, tail_ref,
                 *, tiles_per_seq):
    rows = x_ref.shape[0]
    di = SSM_D_INNER
    n_main = o_ref.shape[-1]

    @pl.when(pl.program_id(0) % tiles_per_seq == 0)
    def _():
        tail_ref[...] = jnp.zeros_like(tail_ref)

    h = _rms(x_ref[...], g_ref[...]).astype(BF16)
    sub = lax.broadcasted_iota(jnp.int32, (1, SUBLANES, N_CHUNK), 1)
    z_chunks = list(range(0, di, N_CHUNK))
    conv_chunks = list(range(di, n_main, N_CHUNK))
    order = []
    while z_chunks or conv_chunks:
        order += conv_chunks[:2] + z_chunks[:1]
        conv_chunks, z_chunks = conv_chunks[2:], z_chunks[1:]
    nxt = _dot(h, w_ref[:, order[0]:order[0] + N_CHUNK])
    for idx, n0 in enumerate(order):
        a = nxt
        if idx + 1 < len(order):
            nxt = _dot(h, w_ref[:, order[idx + 1]:order[idx + 1] + N_CHUNK])
        if n0 < di:
            o_ref[:, n0:n0 + N_CHUNK] = _silu_of_half(a).astype(BF16)
            continue
        c0 = n0 - di
        tail = tail_ref[:, c0:c0 + N_CHUNK][None]
        tail_ref[:, c0:c0 + N_CHUNK] = a[rows - SUBLANES:rows, :]
        a3 = a.reshape(rows // SUBLANES, SUBLANES, N_CHUNK)
        for g0 in range(0, rows // SUBLANES, CONV_GROUPS):
            rot = a3[g0:g0 + CONV_GROUPS]
            prev = tail if g0 == 0 else a3[g0 - 1:g0]
            acc = cb_ref[:, c0:c0 + N_CHUNK] + cw_ref[SSM_CONV - 1:SSM_CONV, c0:c0 + N_CHUNK] * rot
            for j in range(1, SSM_CONV):
                rot = pltpu.roll(rot, 1, axis=1)
                prev = pltpu.roll(prev, 1, axis=1)
                above = jnp.concatenate([prev, rot[:-1]], axis=0)
                k = SSM_CONV - 1 - j
                acc = acc + cw_ref[k:k + 1, c0:c0 + N_CHUNK] * jnp.where(sub >= j, rot, above)
            r0 = g0 * SUBLANES
            o_ref[r0:r0 + CONV_GROUPS * SUBLANES, n0:n0 + N_CHUNK] = (
                _silu_of_half(acc).reshape(CONV_GROUPS * SUBLANES, N_CHUNK).astype(BF16))
    dt_ref[...] = _dot(h, wdt_ref[...])


def _a_in_proj(x2, g, w_main, w_dt, conv_w, conv_b, seq):
    t, d = x2.shape
    n_main = w_main.shape[1]
    return pl.pallas_call(
        functools.partial(_a_in_kernel, tiles_per_seq=seq // ROW_TILE),
        grid=(t // ROW_TILE,),
        in_specs=[
            pl.BlockSpec((ROW_TILE, d), lambda i: (i, 0)),
            pl.BlockSpec((1, d), lambda i: (0, 0)),
            pl.BlockSpec((d, n_main), lambda i: (0, 0)),
            pl.BlockSpec((d, LANES), lambda i: (0, 0)),
            pl.BlockSpec((SSM_CONV, SSM_CONV_DIM), lambda i: (0, 0)),
            pl.BlockSpec((1, SSM_CONV_DIM), lambda i: (0, 0)),
        ],
        out_specs=[
            pl.BlockSpec((ROW_TILE, n_main), lambda i: (i, 0)),
            pl.BlockSpec((ROW_TILE, LANES), lambda i: (i, 0)),
        ],
        out_shape=[
            jax.ShapeDtypeStruct((t, n_main), BF16),
            jax.ShapeDtypeStruct((t, LANES), F32),
        ],
        scratch_shapes=[pltpu.VMEM((SUBLANES, SSM_CONV_DIM), F32)],
        compiler_params=_cparams(("arbitrary",)),
        name="mamba_in_proj",
    )(x2, g, w_main, w_dt, conv_w, conv_b)


def _ssd_kernel(zx_ref, dtc_ref, dtr_ref, hp_row_ref, hp_col_ref, gng_ref, o_ref, state_ref):
    q = zx_ref.shape[0]
    g_n, r_n, p_n, n_n = SSM_GROUPS, SSM_HPG, SSM_HEAD_DIM, SSM_STATE
    di = SSM_D_INNER
    hn = SSM_HEADS
    gw = r_n * p_n

    @pl.when(pl.program_id(1) == 0)
    def _():
        state_ref[...] = jnp.zeros_like(state_ref)

    dtb_row, alog_row, dskip_row = hp_row_ref[0:1, :], hp_row_ref[1:2, :], hp_row_ref[2:3, :]
    dtb_col, alog_col = hp_col_ref[:, 0:1], hp_col_ref[:, 1:2]
    dt_c = _softplus(dtc_ref[:, 0:hn] + dtb_row)
    adt_c = dt_c * (-jnp.exp(alog_row))
    dt_r = _softplus(dtr_ref[...] + dtb_col)
    adt_r = dt_r * (-jnp.exp(alog_col))
    ri = lax.broadcasted_iota(jnp.int32, (q, q), 0)
    ci = lax.broadcasted_iota(jnp.int32, (q, q), 1)
    causal = ri >= ci
    tril = jnp.where(causal, 1.0, 0.0).astype(BF16)
    triu = jnp.where(ri <= ci, 1.0, 0.0).astype(BF16)
    acs_c = sum(_dot(tril, part) for part in _split3(adt_c))
    acs_r = sum(_dot(part, triu) for part in _split3(adt_r))
    tot_row = acs_c[q - 1:q, :]
    eacs_c = jnp.exp(acs_c)
    toend_c = jnp.exp(tot_row - acs_c) * dt_c
    etot_row = jnp.exp(tot_row)
    acs2_c = acs_c * LOG2E
    src2_r = (acs_r - jnp.log(dt_r)) * LOG2E

    lane_head = lax.broadcasted_iota(jnp.int32, (hn, gw), 1) // p_n
    row_head = lax.broadcasted_iota(jnp.int32, (hn, gw), 0)
    toend16 = toend_c.astype(BF16)
    eacs16 = eacs_c.astype(BF16)
    lane_head_row = lane_head[0:1, :]
    head_mask16 = [jnp.where(lane_head_row == r, 1.0, 0.0).astype(BF16) for r in range(r_n)]

    def per_head_row(row, g):
        out = jnp.broadcast_to(row[:, g * r_n:g * r_n + 1], (1, gw))
        for r in range(1, r_n):
            hval = jnp.broadcast_to(row[:, g * r_n + r:g * r_n + r + 1], (1, gw))
            out = jnp.where(lane_head_row == r, hval, out)
        return out

    for g in range(g_n):
        spread = jnp.where(row_head == lane_head + g * r_n, 1.0, 0.0).astype(BF16)
        xs16 = zx_ref[:, di + g * gw:di + (g + 1) * gw]
        bm16 = zx_ref[:, 2 * di + g * n_n:2 * di + (g + 1) * n_n]
        cm16 = zx_ref[:, 2 * di + SSM_GN + g * n_n:2 * di + SSM_GN + (g + 1) * n_n]
        xs = xs16.astype(F32)
        cb = _dot_nt(cm16, bm16)
        st = state_ref[g]
        y = _dot(cm16, st.astype(BF16)) * _dot(eacs16, spread)
        for r in range(r_n):
            h = g * r_n + r
            seg2 = jnp.broadcast_to(acs2_c[:, h:h + 1], (q, q)) - src2_r[h:h + 1, :]
            lmat = (cb * jnp.where(causal, jnp.exp2(seg2), 0.0)).astype(BF16)
            y = y + _dot(lmat, xs16 * head_mask16[r])
        y = y + per_head_row(dskip_row, g) * xs
        xend16 = (xs * _dot(toend16, spread)).astype(BF16)
        state_ref[g] = st * per_head_row(etot_row, g) + _dot_tn(bm16, xend16)
        u = y * zx_ref[:, g * gw:(g + 1) * gw].astype(F32)
        u = u * lax.rsqrt(jnp.mean(u * u, axis=-1, keepdims=True) + EPS)
        o_ref[:, g * gw:(g + 1) * gw] = (u * gng_ref[:, g * gw:(g + 1) * gw]).astype(BF16)


def _ssd(zx, dt_col, dt_row, hp_row, hp_col, gate_g, batch, seq):
    q = SSD_Q
    nc = seq // q
    n_main = zx.shape[1]
    hn = SSM_HEADS
    return pl.pallas_call(
        _ssd_kernel,
        grid=(batch, nc),
        in_specs=[
            pl.BlockSpec((q, n_main), lambda b, c: (b * nc + c, 0)),
            pl.BlockSpec((q, LANES), lambda b, c: (b * nc + c, 0)),
            pl.BlockSpec((None, hn, q), lambda b, c: (b, 0, c)),
            pl.BlockSpec((SUBLANES, hn), lambda b, c: (0, 0)),
            pl.BlockSpec((hn, SUBLANES), lambda b, c: (0, 0)),
            pl.BlockSpec((1, SSM_D_INNER), lambda b, c: (0, 0)),
        ],
        out_specs=pl.BlockSpec((q, SSM_D_INNER), lambda b, c: (b * nc + c, 0)),
        out_shape=jax.ShapeDtypeStruct((batch * seq, SSM_D_INNER), BF16),
        scratch_shapes=[
            pltpu.VMEM((SSM_GROUPS, SSM_STATE, SSM_HPG * SSM_HEAD_DIM), F32),
        ],
        compiler_params=_cparams(("parallel", "arbitrary")),
        name="mamba_ssd",
    )(zx, dt_col, dt_row, hp_row, hp_col, gate_g)


def _out_proj_kernel(y_ref, w_ref, x_ref, o_ref):
    o_ref[...] = x_ref[...] + _dot(y_ref[...], w_ref[...])


def _out_proj(y, w, x2):
    t, k = y.shape
    d = w.shape[1]
    return pl.pallas_call(
        _out_proj_kernel,
        grid=(t // ROW_TILE,),
        in_specs=[
            pl.BlockSpec((ROW_TILE, k), lambda i: (i, 0)),
            pl.BlockSpec((k, d), lambda i: (0, 0)),
            pl.BlockSpec((ROW_TILE, d), lambda i: (i, 0)),
        ],
        out_specs=pl.BlockSpec((ROW_TILE, d), lambda i: (i, 0)),
        out_shape=jax.ShapeDtypeStruct((t, d), F32),
        compiler_params=_cparams(("parallel",)),
        name="mamba_out_proj",
    )(y, w, x2)


def _gated_out_proj_kernel(a_ref, gate_ref, w_ref, x_ref, fg_ref, o_ref, *, final_norm):
    gate = gate_ref[...].astype(F32)
    y = (a_ref[...].astype(F32) * _silu(gate)).astype(BF16)
    out = x_ref[...] + _dot(y, w_ref[...])
    if final_norm:
        out = _rms(out, fg_ref[...])
    o_ref[...] = out


def _gated_out_proj(attn, gate, w, x2, final_g, final_norm):
    t, k = attn.shape
    d = w.shape[1]
    return pl.pallas_call(
        functools.partial(_gated_out_proj_kernel, final_norm=final_norm),
        grid=(t // ROW_TILE,),
        in_specs=[
            pl.BlockSpec((ROW_TILE, k), lambda i: (i, 0)),
            pl.BlockSpec((ROW_TILE, k), lambda i: (i, 0)),
            pl.BlockSpec((k, d), lambda i: (0, 0)),
            pl.BlockSpec((ROW_TILE, d), lambda i: (i, 0)),
            pl.BlockSpec((1, d), lambda i: (0, 0)),
        ],
        out_specs=pl.BlockSpec((ROW_TILE, d), lambda i: (i, 0)),
        out_shape=jax.ShapeDtypeStruct((t, d), F32),
        compiler_params=_cparams(("parallel",)),
        name="attn_out_proj",
    )(attn, gate, w, x2, final_g)


def _kv_kernel(x_ref, g_ref, wk_ref, wv_ref, aug_ref, k_ref, vt_ref):
    h = _rms(x_ref[...], g_ref[...]).astype(BF16)
    hn = DIFF_HEADS
    for n0 in range(0, wk_ref.shape[1], N_CHUNK):
        kc = _dot(h, wk_ref[:, n0:n0 + N_CHUNK])
        for j in range(N_CHUNK // QK_PAD):
            idx = n0 // QK_PAD + j
            k_ref[idx // 2, idx % 2] = (kc[:, j * QK_PAD:(j + 1) * QK_PAD]
                                        + aug_ref[idx // 2]).astype(BF16)
    for n0 in range(0, wv_ref.shape[1], N_CHUNK):
        vc = _dot(h, wv_ref[:, n0:n0 + N_CHUNK])
        for j in range(N_CHUNK // DIFF_V_DIM):
            head = n0 // DIFF_V_DIM + j
            vt_ref[head, 0:DIFF_V_DIM, :] = vc[:, j * DIFF_V_DIM:(j + 1) * DIFF_V_DIM].T.astype(BF16)
    extra = (ATT_VROWS - DIFF_V_DIM, vt_ref.shape[-1])
    ones_row = jnp.where(lax.broadcasted_iota(jnp.int32, extra, 0) == 0, 1.0, 0.0).astype(BF16)
    for head in range(hn):
        vt_ref[head, DIFF_V_DIM:ATT_VROWS, :] = ones_row


def _kv_proj(x2, g, wk, wv, k_aug, batch, seq):
    d = x2.shape[1]
    nt = seq // ROW_TILE
    hn = DIFF_HEADS
    return pl.pallas_call(
        _kv_kernel,
        grid=(batch, nt),
        in_specs=[
            pl.BlockSpec((ROW_TILE, d), lambda b, i: (b * nt + i, 0)),
            pl.BlockSpec((1, d), lambda b, i: (0, 0)),
            pl.BlockSpec(wk.shape, lambda b, i: (0, 0)),
            pl.BlockSpec(wv.shape, lambda b, i: (0, 0)),
            pl.BlockSpec((hn, ROW_TILE, QK_PAD), lambda b, i: (0, i, 0)),
        ],
        out_specs=[
            pl.BlockSpec((None, hn, 2, ROW_TILE, QK_PAD), lambda b, i: (b, 0, 0, i, 0)),
            pl.BlockSpec((None, hn, ATT_VROWS, ROW_TILE), lambda b, i: (b, 0, 0, i)),
        ],
        out_shape=[
            jax.ShapeDtypeStruct((batch, hn, 2, seq, QK_PAD), BF16),
            jax.ShapeDtypeStruct((batch, hn, ATT_VROWS, seq), BF16),
        ],
        compiler_params=_cparams(("parallel", "parallel")),
        name="kv_proj",
    )(x2, g, wk, wv, k_aug)


def _b_in_kernel(x_ref, g_ref, wq_ref, wg_ref, aug_ref, q_ref, gate_ref):
    h = _rms(x_ref[...], g_ref[...]).astype(BF16)
    for n0 in range(0, wq_ref.shape[1], N_CHUNK):
        qc = _dot(h, wq_ref[:, n0:n0 + N_CHUNK]) * QK_SCALE
        for j in range(N_CHUNK // QK_PAD):
            idx = n0 // QK_PAD + j
            q_ref[idx // 2, idx % 2] = (qc[:, j * QK_PAD:(j + 1) * QK_PAD] + aug_ref[...]).T.astype(BF16)
    for n0 in range(0, wg_ref.shape[1], N_CHUNK):
        gate_ref[:, n0:n0 + N_CHUNK] = _dot(h, wg_ref[:, n0:n0 + N_CHUNK]).astype(BF16)


def _b_in_proj(x2, g, wq, wg, q_aug, batch, seq):
    d = x2.shape[1]
    nt = seq // ROW_TILE
    hn = DIFF_HEADS
    return pl.pallas_call(
        _b_in_kernel,
        grid=(batch, nt),
        in_specs=[
            pl.BlockSpec((ROW_TILE, d), lambda b, i: (b * nt + i, 0)),
            pl.BlockSpec((1, d), lambda b, i: (0, 0)),
            pl.BlockSpec(wq.shape, lambda b, i: (0, 0)),
            pl.BlockSpec(wg.shape, lambda b, i: (0, 0)),
            pl.BlockSpec((ROW_TILE, QK_PAD), lambda b, i: (i, 0)),
        ],
        out_specs=[
            pl.BlockSpec((None, hn, 2, QK_PAD, ROW_TILE), lambda b, i: (b, 0, 0, 0, i)),
            pl.BlockSpec((ROW_TILE, DIFF_WIDTH), lambda b, i: (b * nt + i, 0)),
        ],
        out_shape=[
            jax.ShapeDtypeStruct((batch, hn, 2, QK_PAD, seq), BF16),
            jax.ShapeDtypeStruct((batch * seq, DIFF_WIDTH), BF16),
        ],
        compiler_params=_cparams(("parallel", "parallel")),
        name="attn_in_proj",
    )(x2, g, wq, wg, q_aug)


def _attn_kernel(lam_ref, q_ref, k_ref, vt_ref, dbias_ref, subg_ref, o_ref,
                 s_a, s_b, p_a, p_b, p_last, m_scr, acc_scr, al_scr, *, lambda_init):
    tq, tk = ATT_TQ, ATT_TK
    seq = k_ref.shape[2]
    places = tk // tq
    assert places * 2 * ATT_UNROLL >= seq // tq
    streams = [(hh, i) for hh in range(ATT_HEADS) for i in range(2)]
    n_qt = seq // tq
    lf = lam_ref[...]
    lam = (jnp.exp(jnp.sum(lf[0:1, :] * lf[1:2, :], axis=-1, keepdims=True))
           - jnp.exp(jnp.sum(lf[2:3, :] * lf[3:4, :], axis=-1, keepdims=True)) + lambda_init)
    neg = jnp.float32(-1e30)

    def fold(x, op):
        return op(x.reshape(x.shape[0] // SUBLANES, SUBLANES, tq), axis=0)

    def load_q(qi):
        q0 = pl.multiple_of(qi * tq, tq)
        return [q_ref[hh, i, :, pl.ds(q0, tq)] for hh, i in streams]

    def key_rows(t, rows=tk):
        return pl.ds(t * tk if isinstance(t, int) else pl.multiple_of(t * tk, tk), rows)

    def scores_into(qs, t, s_buf, rows=tk, want_max=True):
        mparts = []
        for n, (hh, i) in enumerate(streams):
            s = _dot(k_ref[hh, i, key_rows(t, rows), :], qs[n])
            s_buf[n, 0:rows, :] = s
            if want_max:
                mparts.append(fold(s, jnp.max))
        return tuple(mparts)

    def stage_c(t, p_buf, alphas, accs, rows=tk):
        vts = [vt_ref[hh, :, key_rows(t, rows)] for hh in range(ATT_HEADS)]
        return tuple(alphas[n] * accs[n] + _dot(vts[hh], p_buf[n, 0:rows, :])
                     for n, (hh, _) in enumerate(streams))

    def flush_previous(qi, old_place):
        q_old = jnp.maximum(qi - 1, 0)
        o0 = pl.multiple_of(q_old * tq, tq)
        slot = q_old % 2
        accs = stage_c(q_old // (tk // tq), p_last.at[slot],
                       tuple(al_scr[slot, n, 0:1, :] for n in range(len(streams))),
                       tuple(acc_scr[slot, n] for n in range(len(streams))),
                       rows=(old_place + 1) * tq)
        e = DIFF_V_DIM
        for hh in range(ATT_HEADS):
            a1, a2 = accs[2 * hh], accs[2 * hh + 1]
            out_t = a1[0:e] / a1[e:e + 1] - lam * (a2[0:e] / a2[e:e + 1])
            out = _rms(out_t.T, subg_ref[...]) * (1.0 - lambda_init)
            o_ref[pl.ds(o0, tq), hh * e:(hh + 1) * e] = out.astype(BF16)
        return tuple(jnp.zeros((ATT_VROWS, tq), F32) for _ in streams)

    def q_tile(qi, carry):
        n_full = qi // places
        which = qi % places
        qs = load_q(qi)

        def stage_a(t, s_buf, rows=tk, want_max=True):
            return scores_into(qs, t, s_buf, rows, want_max)

        def stage_b(s_buf, p_buf, mparts, ms, ls, place=None):
            diag = place is not None
            rows = (place + 1) * tq if diag else tk
            new_m, new_l, alphas = [], [], []
            for n, (hh, _) in enumerate(streams):
                def scores(r0):
                    s = s_buf[n, r0:r0 + ATT_ROWS, :]
                    return s + dbias_ref[hh, place, r0:r0 + ATT_ROWS, :] if diag else s

                if diag:
                    mpart = fold(scores(0), jnp.max)
                    for r0 in range(ATT_ROWS, rows, ATT_ROWS):
                        mpart = jnp.maximum(mpart, fold(scores(r0), jnp.max))
                else:
                    mpart = mparts[n]
                m_new = jnp.maximum(ms[n], jnp.max(mpart, axis=0, keepdims=True))
                alpha = jnp.exp2(ms[n] - m_new)
                for r0 in range(0, rows, ATT_ROWS):
                    p = jnp.exp2(scores(r0) - m_new)
                    p_buf[n, r0:r0 + ATT_ROWS, :] = p.astype(BF16)
                new_l.append(ls[n])
                new_m.append(m_new)
                alphas.append(alpha)
            return tuple(new_m), tuple(new_l), tuple(alphas)

        def step(j, bufs, st, place, last):
            s_cur, s_nxt, p_cur, p_prv = bufs
            mparts, ms, ls, alphas, accs = st
            if j + 1 == last:
                mparts_next = stage_a(j + 1, s_nxt, (place + 1) * tq, False)
            else:
                mparts_next = stage_a(j + 1, s_nxt)
            ms, ls, alphas_new = stage_b(s_cur, p_cur, mparts, ms, ls)
            accs = (flush_previous(qi, (place - 1) % places) if j == 0
                    else stage_c(j - 1, p_prv, alphas, accs))
            return mparts_next, ms, ls, alphas_new, accs

        def steps(j0, j1, st, place, last=None):
            for j in range(j0, j1):
                st = step(j, odd if j % 2 else even, st, place, last)
            return st

        def finish(v, st, place):
            s_cur, _, _, p_prv = odd if v % 2 else even
            _, ms, ls, alphas, accs = st
            accs = (flush_previous(qi, (place - 1) % places) if v == 0
                    else stage_c(v - 1, p_prv, alphas, accs))
            slot = qi % 2
            _, ls, alphas_new = stage_b(s_cur, p_last.at[slot], None, ms, ls, place)
            nxt = scores_into(load_q(jnp.minimum(qi + 1, n_qt - 1)), 0, s_a)
            for n in range(len(streams)):
                m_scr[n] = nxt[n]
                acc_scr[slot, n] = accs[n]
                al_scr[slot, n] = jnp.broadcast_to(alphas_new[n], (SUBLANES, tq))

        even = (s_a, s_b, p_a, p_b)
        odd = (s_b, s_a, p_b, p_a)
        init = (tuple(m_scr[n] for n in range(len(streams))),
                tuple(jnp.full((1, tq), neg, F32) for _ in streams),
                tuple(jnp.zeros((SUBLANES, tq), F32) for _ in streams),
                tuple(jnp.ones((1, tq), F32) for _ in streams),
                tuple(jnp.zeros((ATT_VROWS, tq), F32) for _ in streams))
        for place in range(places):
            @pl.when(which == place)
            def _(place=place):
                st = init
                if ATT_UNROLL < n_qt // places:
                    st = lax.fori_loop(0, n_full // ATT_UNROLL,
                                       lambda _, c: steps(0, ATT_UNROLL, c, place), init)
                for v in range(n_qt // places):
                    @pl.when(n_full == v)
                    def _(v=v):
                        lo = v - v % ATT_UNROLL if v >= ATT_UNROLL else 0
                        finish(v, steps(lo, v, st, place, last=v), place)

        return carry

    first = scores_into(load_q(0), 0, s_a)
    for n in range(len(streams)):
        m_scr[n] = first[n]
        acc_scr[0, n] = jnp.where(
            lax.broadcasted_iota(jnp.int32, (ATT_VROWS, tq), 0) == DIFF_V_DIM, 1.0, 0.0)
        al_scr[0, n] = jnp.ones((SUBLANES, tq), F32)
    p_last[0] = jnp.zeros(p_last.shape[1:], BF16)
    lax.fori_loop(0, n_qt, q_tile, 0)
    flush_previous(n_qt, (n_qt - 1) % places)


def _attention(lam_qk, q, k, vt, dbias, sub_g, lambda_init):
    batch, hn, _, _, seq = q.shape
    return pl.pallas_call(
        functools.partial(_attn_kernel, lambda_init=lambda_init),
        grid=(batch, hn // ATT_HEADS),
        in_specs=[
            pl.BlockSpec(lam_qk.shape, lambda b, h: (0, 0)),
            pl.BlockSpec((None, ATT_HEADS, 2, QK_PAD, seq), lambda b, h: (b, h, 0, 0, 0)),
            pl.BlockSpec((None, ATT_HEADS, 2, seq, QK_PAD), lambda b, h: (b, h, 0, 0, 0)),
            pl.BlockSpec((None, ATT_HEADS, ATT_VROWS, seq), lambda b, h: (b, h, 0, 0)),
            pl.BlockSpec((ATT_HEADS, ATT_TK // ATT_TQ, ATT_TK, ATT_TQ), lambda b, h: (h, 0, 0, 0)),
            pl.BlockSpec((1, DIFF_V_DIM), lambda b, h: (0, 0)),
        ],
        out_specs=pl.BlockSpec((None, seq, ATT_HEADS * DIFF_V_DIM), lambda b, h: (b, 0, h)),
        out_shape=jax.ShapeDtypeStruct((batch, seq, DIFF_WIDTH), BF16),
        scratch_shapes=[
            pltpu.VMEM((2 * ATT_HEADS, ATT_TK, ATT_TQ), F32),
            pltpu.VMEM((2 * ATT_HEADS, ATT_TK, ATT_TQ), F32),
            pltpu.VMEM((2 * ATT_HEADS, ATT_TK, ATT_TQ), BF16),
            pltpu.VMEM((2 * ATT_HEADS, ATT_TK, ATT_TQ), BF16),
            pltpu.VMEM((2, 2 * ATT_HEADS, ATT_TK, ATT_TQ), BF16),
            pltpu.VMEM((2 * ATT_HEADS, SUBLANES, ATT_TQ), F32),
            pltpu.VMEM((2, 2 * ATT_HEADS, ATT_VROWS, ATT_TQ), F32),
            pltpu.VMEM((2, 2 * ATT_HEADS, SUBLANES, ATT_TQ), F32),
        ],
        compiler_params=_cparams(("parallel", "parallel")),
        name="diff_attention",
    )(lam_qk, q, k, vt, dbias, sub_g)


def _np_split3(a):
    a = a.astype(np.float32)
    a1 = a.astype(ml_dtypes.bfloat16).astype(np.float32)
    r1 = a - a1
    a2 = r1.astype(ml_dtypes.bfloat16).astype(np.float32)
    a3 = (r1 - a2).astype(ml_dtypes.bfloat16).astype(np.float32)
    return a1, a2, a3


def _alibi_tables(seq):
    hn = DIFF_HEADS
    slopes = np.float32(2.0) ** (-np.float32(ALIBI_MAX_EXP) * np.arange(1, hn + 1, dtype=np.float32)
                                 / np.float32(hn))
    slopes = (slopes.astype(np.float64) * LOG2E).astype(np.float32)
    pos = np.arange(seq, dtype=np.float32)
    kp = _np_split3(slopes[:, None] * pos[None, :])
    sl = _np_split3(np.broadcast_to(slopes[:, None], (hn, seq)))
    pos_hi = np.floor(pos / 256.0) * 256.0
    pos_lo = pos - pos_hi
    ones = np.ones((seq,), np.float32)
    k_cols = list(kp)
    q_cols = [ones, ones, ones]
    for t in sl:
        k_cols += [-t, -t]
        q_cols += [pos_hi, pos_lo]
    lane_pad = (DIFF_HEAD_DIM, QK_PAD - DIFF_HEAD_DIM - len(k_cols))
    k_aug = jnp.pad(jnp.asarray(np.stack(k_cols, axis=-1)), ((0, 0), (0, 0), lane_pad))
    q_aug = jnp.pad(jnp.asarray(np.stack(q_cols, axis=-1)), ((0, 0), lane_pad))
    kk = np.arange(ATT_TK)[None, :, None]
    qq = np.arange(ATT_TQ)[None, None, :] + ATT_TQ * np.arange(ATT_TK // ATT_TQ)[:, None, None]
    allowed = (kk // CHUNK) <= (qq // CHUNK)
    fix = np.where(kk > qq, -2.0 * (kk - qq), 0.0).astype(np.float32)
    dbias = np.where(allowed[None], slopes[:, None, None, None] * fix[None], -np.inf).astype(np.float32)
    return k_aug, q_aug, jnp.asarray(dbias)


def _pad_heads(w, n_blocks):
    d = w.shape[0]
    w = w.reshape(d, n_blocks, DIFF_HEAD_DIM)
    w = jnp.concatenate([w, jnp.zeros_like(w)], axis=-1)
    return w.reshape(d, n_blocks * QK_PAD).astype(BF16)


def kernel(x, a_norm_g, a_w_in, a_conv_w, a_conv_b, a_dt_bias, a_a_log, a_d_skip, a_gate_norm_g,
           a_w_out, kv_norm_g, w_kv, b_norm_g, b_w_in, b_lambda, b_sub_g, b_w_out, final_norm_g):
    batch, seq, d = x.shape
    t = batch * seq
    hn = SSM_HEADS
    x2 = x.reshape(t, d)

    for i in range(N_A_LAYERS):
        w_in = a_w_in[i]
        n_main = SSM_D_INNER + SSM_CONV_DIM
        w_main = jnp.concatenate([0.5 * w_in[:, :SSM_D_INNER], w_in[:, SSM_D_INNER:n_main]], axis=1).astype(BF16)
        w_dt = jnp.pad(w_in[:, n_main:], ((0, 0), (0, LANES - hn))).astype(BF16)
        zx, dt_raw = _a_in_proj(x2, a_norm_g[i][None], w_main, w_dt, 0.5 * a_conv_w[i],
                                0.5 * a_conv_b[i][None], seq)
        dt_row = dt_raw[:, :hn].reshape(batch, seq, hn).transpose(0, 2, 1)
        hp_row = jnp.zeros((SUBLANES, hn), F32).at[0].set(a_dt_bias[i]).at[1].set(a_a_log[i]).at[2].set(a_d_skip[i])
        hp_col = hp_row.T
        y = _ssd(zx, dt_raw, dt_row, hp_row, hp_col, a_gate_norm_g[i][None], batch, seq)
        x2 = _out_proj(y, a_w_out[i].astype(BF16), x2)

    k_aug, q_aug, dbias = _alibi_tables(seq)
    n_k = DIFF_HEADS * 2 * DIFF_HEAD_DIM
    wk = _pad_heads(w_kv[:, :n_k], DIFF_HEADS * 2)
    wv = w_kv[:, n_k:].astype(BF16)
    k, vt = _kv_proj(x2, kv_norm_g[None], wk, wv, k_aug, batch, seq)

    for j in range(N_B_LAYERS):
        layer = N_A_LAYERS + j
        lambda_init = 0.8 - 0.6 * math.exp(-0.3 * layer)
        wq = _pad_heads(b_w_in[j][:, :DIFF_WIDTH], DIFF_HEADS * 2)
        wg = b_w_in[j][:, DIFF_WIDTH:].astype(BF16)
        q, gate = _b_in_proj(x2, b_norm_g[j][None], wq, wg, q_aug, batch, seq)
        attn = _attention(b_lambda[j], q, k, vt, dbias, b_sub_g[j][None], lambda_init)
        x2 = _gated_out_proj(attn.reshape(t, DIFF_WIDTH), gate, b_w_out[j].astype(BF16), x2,
                             final_norm_g[None], final_norm=(j == N_B_LAYERS - 1))
    return x2.reshape(batch, seq, d)
```

```python
import functools
import math

import jax
import jax.numpy as jnp
import ml_dtypes
import numpy as np
from jax import lax
from jax.experimental import pallas as pl
from jax.experimental.pallas import tpu as pltpu

F32 = jnp.float32
BF16 = jnp.bfloat16

D_MODEL = 1024
DEPTH = 4
CHUNK = 64
N_B_LAYERS = DEPTH // 2
N_A_LAYERS = DEPTH - N_B_LAYERS
SSM_D_INNER = 2 * D_MODEL
SSM_HEAD_DIM = 64
SSM_HEADS = SSM_D_INNER // SSM_HEAD_DIM
SSM_GROUPS = 8
SSM_HPG = SSM_HEADS // SSM_GROUPS
SSM_STATE = 128
SSM_CONV = 4
SSM_GN = SSM_GROUPS * SSM_STATE
SSM_CONV_DIM = SSM_D_INNER + 2 * SSM_GN
DIFF_HEAD_DIM = 64
DIFF_V_DIM = 2 * DIFF_HEAD_DIM
DIFF_HEADS = D_MODEL // DIFF_V_DIM
DIFF_WIDTH = DIFF_HEADS * DIFF_V_DIM
ALIBI_MAX_EXP = 8.0
EPS = 1e-5
LOG2E = math.log2(math.e)
QK_SCALE = LOG2E / math.sqrt(DIFF_HEAD_DIM)

LANES = 128
SUBLANES = 8
VMEM_LIMIT = 56 * 1024 * 1024

ROW_TILE = 512
SSD_Q = 256
ATT_TQ = 256
ATT_TK = 512
ATT_UNROLL = 4
ATT_HEADS = 2
ATT_ROWS = 64
ATT_VROWS = DIFF_V_DIM + 2 * SUBLANES
BF16_EXACT_INT = 256.0
N_CHUNK = 256
CONV_GROUPS = 8
QK_PAD = 2 * DIFF_HEAD_DIM


def _cparams(sem):
    return pltpu.CompilerParams(dimension_semantics=sem, vmem_limit_bytes=VMEM_LIMIT)


def _rms(x, g):
    return x * lax.rsqrt(jnp.mean(x * x, axis=-1, keepdims=True) + EPS) * g


def _silu_of_half(h):
    return h * jnp.tanh(h) + h


def _silu(x):
    return _silu_of_half(0.5 * x)


def _softplus(x):
    return jnp.maximum(x, 0.0) + jnp.log1p(jnp.exp(-jnp.abs(x)))


def _trunc_bf16(a):
    bits = lax.bitcast_convert_type(a, jnp.int32) & jnp.int32(-65536)
    return lax.bitcast_convert_type(bits, F32)


def _split3(a):
    a1 = _trunc_bf16(a)
    r1 = a - a1
    a2 = _trunc_bf16(r1)
    return a1.astype(BF16), a2.astype(BF16), (r1 - a2).astype(BF16)


def _dot(a, b):
    return jnp.dot(a, b, preferred_element_type=F32)


def _dot_nt(a, b):
    return lax.dot_general(a, b, (((1,), (1,)), ((), ())), preferred_element_type=F32)


def _dot_tn(a, b):
    return lax.dot_general(a, b, (((0,), (0,)), ((), ())), preferred_element_type=F32)


def _a_in_kernel(x_ref, g_ref, w_ref, wdt_ref, cw_ref, cb_ref, o_ref, dt_ref, tail_ref,
                 *, tiles_per_seq):
    rows = x_ref.shape[0]
    di = SSM_D_INNER
    n_main = o_ref.shape[-1]

    @pl.when(pl.program_id(0) % tiles_per_seq == 0)
    def _():
        tail_ref[...] = jnp.zeros_like(tail_ref)

    h = _rms(x_ref[...], g_ref[...]).astype(BF16)
    sub = lax.broadcasted_iota(jnp.int32, (1, SUBLANES, N_CHUNK), 1)
    z_chunks = list(range(0, di, N_CHUNK))
    conv_chunks = list(range(di, n_main, N_CHUNK))
    order = []
    while z_chunks or conv_chunks:
        order += conv_chunks[:2] + z_chunks[:1]
        conv_chunks, z_chunks = conv_chunks[2:], z_chunks[1:]
    nxt = _dot(h, w_ref[:, order[0]:order[0] + N_CHUNK])
    for idx, n0 in enumerate(order):
        a = nxt
        if idx + 1 < len(order):
            nxt = _dot(h, w_ref[:, order[idx + 1]:order[idx + 1] + N_CHUNK])
        if n0 < di:
            o_ref[:, n0:n0 + N_CHUNK] = _silu_of_half(a).astype(BF16)
            continue
        c0 = n0 - di
        tail = tail_ref[:, c0:c0 + N_CHUNK][None]
        tail_ref[:, c0:c0 + N_CHUNK] = a[rows - SUBLANES:rows, :]
        a3 = a.reshape(rows // SUBLANES, SUBLANES, N_CHUNK)
        for g0 in range(0, rows // SUBLANES, CONV_GROUPS):
            rot = a3[g0:g0 + CONV_GROUPS]
            prev = tail if g0 == 0 else a3[g0 - 1:g0]
            acc = cb_ref[:, c0:c0 + N_CHUNK] + cw_ref[SSM_CONV - 1:SSM_CONV, c0:c0 + N_CHUNK] * rot
            for j in range(1, SSM_CONV):
                rot = pltpu.roll(rot, 1, axis=1)
                prev = pltpu.roll(prev, 1, axis=1)
                above = jnp.concatenate([prev, rot[:-1]], axis=0)
                k = SSM_CONV - 1 - j
                acc = acc + cw_ref[k:k + 1, c0:c0 + N_CHUNK] * jnp.where(sub >= j, rot, above)
            r0 = g0 * SUBLANES
            o_ref[r0:r0 + CONV_GROUPS * SUBLANES, n0:n0 + N_CHUNK] = (
                _silu_of_half(acc).reshape(CONV_GROUPS * SUBLANES, N_CHUNK).astype(BF16))
    dt_ref[...] = _dot(h, wdt_ref[...])


def _a_in_proj(x2, g, w_main, w_dt, conv_w, conv_b, seq):
    t, d = x2.shape
    n_main = w_main.shape[1]
    return pl.pallas_call(
        functools.partial(_a_in_kernel, tiles_per_seq=seq // ROW_TILE),
        grid=(t // ROW_TILE,),
        in_specs=[
            pl.BlockSpec((ROW_TILE, d), lambda i: (i, 0)),
            pl.BlockSpec((1, d), lambda i: (0, 0)),
            pl.BlockSpec((d, n_main), lambda i: (0, 0)),
            pl.BlockSpec((d, LANES), lambda i: (0, 0)),
            pl.BlockSpec((SSM_CONV, SSM_CONV_DIM), lambda i: (0, 0)),
            pl.BlockSpec((1, SSM_CONV_DIM), lambda i: (0, 0)),
        ],
        out_specs=[
            pl.BlockSpec((ROW_TILE, n_main), lambda i: (i, 0)),
            pl.BlockSpec((ROW_TILE, LANES), lambda i: (i, 0)),
        ],
        out_shape=[
            jax.ShapeDtypeStruct((t, n_main), BF16),
            jax.ShapeDtypeStruct((t, LANES), F32),
        ],
        scratch_shapes=[pltpu.VMEM((SUBLANES, SSM_CONV_DIM), F32)],
        compiler_params=_cparams(("arbitrary",)),
        name="mamba_in_proj",
    )(x2, g, w_main, w_dt, conv_w, conv_b)


def _ssd_kernel(zx_ref, dtc_ref, dtr_ref, hp_row_ref, hp_col_ref, gng_ref, o_ref, state_ref):
    q = zx_ref.shape[0]
    g_n, r_n, p_n, n_n = SSM_GROUPS, SSM_HPG, SSM_HEAD_DIM, SSM_STATE
    di = SSM_D_INNER
    hn = SSM_HEADS
    gw = r_n * p_n

    @pl.when(pl.program_id(1) == 0)
    def _():
        state_ref[...] = jnp.zeros_like(state_ref)

    dtb_row, alog_row, dskip_row = hp_row_ref[0:1, :], hp_row_ref[1:2, :], hp_row_ref[2:3, :]
    dtb_col, alog_col = hp_col_ref[:, 0:1], hp_col_ref[:, 1:2]
    dt_c = _softplus(dtc_ref[:, 0:hn] + dtb_row)
    adt_c = dt_c * (-jnp.exp(alog_row))
    dt_r = _softplus(dtr_ref[...] + dtb_col)
    adt_r = dt_r * (-jnp.exp(alog_col))
    ri = lax.broadcasted_iota(jnp.int32, (q, q), 0)
    ci = lax.broadcasted_iota(jnp.int32, (q, q), 1)
    causal = ri >= ci
    tril = jnp.where(causal, 1.0, 0.0).astype(BF16)
    triu = jnp.where(ri <= ci, 1.0, 0.0).astype(BF16)
    acs_c = sum(_dot(tril, part) for part in _split3(adt_c))
    acs_r = sum(_dot(part, triu) for part in _split3(adt_r))
    tot_row = acs_c[q - 1:q, :]
    eacs_c = jnp.exp(acs_c)
    toend_c = jnp.exp(tot_row - acs_c) * dt_c
    etot_row = jnp.exp(tot_row)
    acs2_c = acs_c * LOG2E
    src2_r = (acs_r - jnp.log(dt_r)) * LOG2E

    lane_head = lax.broadcasted_iota(jnp.int32, (hn, gw), 1) // p_n
    row_head = lax.broadcasted_iota(jnp.int32, (hn, gw), 0)
    toend16 = toend_c.astype(BF16)
    eacs16 = eacs_c.astype(BF16)
    lane_head_row = lane_head[0:1, :]
    head_mask16 = [jnp.where(lane_head_row == r, 1.0, 0.0).astype(BF16) for r in range(r_n)]

    def per_head_row(row, g):
        out = jnp.broadcast_to(row[:, g * r_n:g * r_n + 1], (1, gw))
        for r in range(1, r_n):
            hval = jnp.broadcast_to(row[:, g * r_n + r:g * r_n + r + 1], (1, gw))
            out = jnp.where(lane_head_row == r, hval, out)
        return out

    for g in range(g_n):
        spread = jnp.where(row_head == lane_head + g * r_n, 1.0, 0.0).astype(BF16)
        xs16 = zx_ref[:, di + g * gw:di + (g + 1) * gw]
        bm16 = zx_ref[:, 2 * di + g * n_n:2 * di + (g + 1) * n_n]
        cm16 = zx_ref[:, 2 * di + SSM_GN + g * n_n:2 * di + SSM_GN + (g + 1) * n_n]
        xs = xs16.astype(F32)
        cb = _dot_nt(cm16, bm16)
        st = state_ref[g]
        y = _dot(cm16, st.astype(BF16)) * _dot(eacs16, spread)
        for r in range(r_n):
            h = g * r_n + r
            seg2 = jnp.broadcast_to(acs2_c[:, h:h + 1], (q, q)) - src2_r[h:h + 1, :]
            lmat = (cb * jnp.where(causal, jnp.exp2(seg2), 0.0)).astype(BF16)
            y = y + _dot(lmat, xs16 * head_mask16[r])
        y = y + per_head_row(dskip_row, g) * xs
        xend16 = (xs * _dot(toend16, spread)).astype(BF16)
        state_ref[g] = st * per_head_row(etot_row, g) + _dot_tn(bm16, xend16)
        u = y * zx_ref[:, g * gw:(g + 1) * gw].astype(F32)
        u = u * lax.rsqrt(jnp.mean(u * u, axis=-1, keepdims=True) + EPS)
        o_ref[:, g * gw:(g + 1) * gw] = (u * gng_ref[:, g * gw:(g + 1) * gw]).astype(BF16)


def _ssd(zx, dt_col, dt_row, hp_row, hp_col, gate_g, batch, seq):
    q = SSD_Q
    nc = seq // q
    n_main = zx.shape[1]
    hn = SSM_HEADS
    return pl.pallas_call(
        _ssd_kernel,
        grid=(batch, nc),
        in_specs=[
            pl.BlockSpec((q, n_main), lambda b, c: (b * nc + c, 0)),
            pl.BlockSpec((q, LANES), lambda b, c: (b * nc + c, 0)),
            pl.BlockSpec((None, hn, q), lambda b, c: (b, 0, c)),
            pl.BlockSpec((SUBLANES, hn), lambda b, c: (0, 0)),
            pl.BlockSpec((hn, SUBLANES), lambda b, c: (0, 0)),
            pl.BlockSpec((1, SSM_D_INNER), lambda b, c: (0, 0)),
        ],
        out_specs=pl.BlockSpec((q, SSM_D_INNER), lambda b, c: (b * nc + c, 0)),
        out_shape=jax.ShapeDtypeStruct((batch * seq, SSM_D_INNER), BF16),
        scratch_shapes=[
            pltpu.VMEM((SSM_GROUPS, SSM_STATE, SSM_HPG * SSM_HEAD_DIM), F32),
        ],
        compiler_params=_cparams(("parallel", "arbitrary")),
        name="mamba_ssd",
    )(zx, dt_col, dt_row, hp_row, hp_col, gate_g)


def _out_proj_kernel(y_ref, w_ref, x_ref, o_ref):
    o_ref[...] = x_ref[...] + _dot(y_ref[...], w_ref[...])


def _out_proj(y, w, x2):
    t, k = y.shape
    d = w.shape[1]
    return pl.pallas_call(
        _out_proj_kernel,
        grid=(t // ROW_TILE,),
        in_specs=[
            pl.BlockSpec((ROW_TILE, k), lambda i: (i, 0)),
            pl.BlockSpec((k, d), lambda i: (0, 0)),
            pl.BlockSpec((ROW_TILE, d), lambda i: (i, 0)),
        ],
        out_specs=pl.BlockSpec((ROW_TILE, d), lambda i: (i, 0)),
        out_shape=jax.ShapeDtypeStruct((t, d), F32),
        compiler_params=_cparams(("parallel",)),
        name="mamba_out_proj",
    )(y, w, x2)


def _gated_out_proj_kernel(a_ref, gate_ref, w_ref, x_ref, fg_ref, o_ref, *, final_norm):
    gate = gate_ref[...].astype(F32)
    y = (a_ref[...].astype(F32) * _silu(gate)).astype(BF16)
    out = x_ref[...] + _dot(y, w_ref[...])
    if final_norm:
        out = _rms(out, fg_ref[...])
    o_ref[...] = out


def _gated_out_proj(attn, gate, w, x2, final_g, final_norm):
    t, k = attn.shape
    d = w.shape[1]
    return pl.pallas_call(
        functools.partial(_gated_out_proj_kernel, final_norm=final_norm),
        grid=(t // ROW_TILE,),
        in_specs=[
            pl.BlockSpec((ROW_TILE, k), lambda i: (i, 0)),
            pl.BlockSpec((ROW_TILE, k), lambda i: (i, 0)),
            pl.BlockSpec((k, d), lambda i: (0, 0)),
            pl.BlockSpec((ROW_TILE, d), lambda i: (i, 0)),
            pl.BlockSpec((1, d), lambda i: (0, 0)),
        ],
        out_specs=pl.BlockSpec((ROW_TILE, d), lambda i: (i, 0)),
        out_shape=jax.ShapeDtypeStruct((t, d), F32),
        compiler_params=_cparams(("parallel",)),
        name="attn_out_proj",
    )(attn, gate, w, x2, final_g)


def _kv_kernel(x_ref, g_ref, wk_ref, wv_ref, aug_ref, k_ref, vt_ref):
    h = _rms(x_ref[...], g_ref[...]).astype(BF16)
    hn = DIFF_HEADS
    for n0 in range(0, wk_ref.shape[1], N_CHUNK):
        kc = _dot(h, wk_ref[:, n0:n0 + N_CHUNK])
        for j in range(N_CHUNK // QK_PAD):
            idx = n0 // QK_PAD + j
            k_ref[idx // 2, idx % 2] = (kc[:, j * QK_PAD:(j + 1) * QK_PAD]
                                        + aug_ref[idx // 2]).astype(BF16)
    for n0 in range(0, wv_ref.shape[1], N_CHUNK):
        vc = _dot(h, wv_ref[:, n0:n0 + N_CHUNK])
        for j in range(N_CHUNK // DIFF_V_DIM):
            head = n0 // DIFF_V_DIM + j
            vt_ref[head, 0:DIFF_V_DIM, :] = vc[:, j * DIFF_V_DIM:(j + 1) * DIFF_V_DIM].T.astype(BF16)
    extra = (ATT_VROWS - DIFF_V_DIM, vt_ref.shape[-1])
    ones_row = jnp.where(lax.broadcasted_iota(jnp.int32, extra, 0) == 0, 1.0, 0.0).astype(BF16)
    for head in range(hn):
        vt_ref[head, DIFF_V_DIM:ATT_VROWS, :] = ones_row


def _kv_proj(x2, g, wk, wv, k_aug, batch, seq):
    d = x2.shape[1]
    nt = seq // ROW_TILE
    hn = DIFF_HEADS
    return pl.pallas_call(
        _kv_kernel,
        grid=(batch, nt),
        in_specs=[
            pl.BlockSpec((ROW_TILE, d), lambda b, i: (b * nt + i, 0)),
            pl.BlockSpec((1, d), lambda b, i: (0, 0)),
            pl.BlockSpec(wk.shape, lambda b, i: (0, 0)),
            pl.BlockSpec(wv.shape, lambda b, i: (0, 0)),
            pl.BlockSpec((hn, ROW_TILE, QK_PAD), lambda b, i: (0, i, 0)),
        ],
        out_specs=[
            pl.BlockSpec((None, hn, 2, ROW_TILE, QK_PAD), lambda b, i: (b, 0, 0, i, 0)),
            pl.BlockSpec((None, hn, ATT_VROWS, ROW_TILE), lambda b, i: (b, 0, 0, i)),
        ],
        out_shape=[
            jax.ShapeDtypeStruct((batch, hn, 2, seq, QK_PAD), BF16),
            jax.ShapeDtypeStruct((batch, hn, ATT_VROWS, seq), BF16),
        ],
        compiler_params=_cparams(("parallel", "parallel")),
        name="kv_proj",
    )(x2, g, wk, wv, k_aug)


def _b_in_kernel(x_ref, g_ref, wq_ref, wg_ref, aug_ref, q_ref, gate_ref):
    h = _rms(x_ref[...], g_ref[...]).astype(BF16)
    for n0 in range(0, wq_ref.shape[1], N_CHUNK):
        qc = _dot(h, wq_ref[:, n0:n0 + N_CHUNK]) * QK_SCALE
        for j in range(N_CHUNK // QK_PAD):
            idx = n0 // QK_PAD + j
            q_ref[idx // 2, idx % 2] = (qc[:, j * QK_PAD:(j + 1) * QK_PAD] + aug_ref[...]).astype(BF16)
    for n0 in range(0, wg_ref.shape[1], N_CHUNK):
        gate_ref[:, n0:n0 + N_CHUNK] = _dot(h, wg_ref[:, n0:n0 + N_CHUNK]).astype(BF16)


def _b_in_proj(x2, g, wq, wg, q_aug, batch, seq):
    d = x2.shape[1]
    nt = seq // ROW_TILE
    hn = DIFF_HEADS
    return pl.pallas_call(
        _b_in_kernel,
        grid=(batch, nt),
        in_specs=[
            pl.BlockSpec((ROW_TILE, d), lambda b, i: (b * nt + i, 0)),
            pl.BlockSpec((1, d), lambda b, i: (0, 0)),
            pl.BlockSpec(wq.shape, lambda b, i: (0, 0)),
            pl.BlockSpec(wg.shape, lambda b, i: (0, 0)),
            pl.BlockSpec((ROW_TILE, QK_PAD), lambda b, i: (i, 0)),
        ],
        out_specs=[
            pl.BlockSpec((None, hn, 2, ROW_TILE, QK_PAD), lambda b, i: (b, 0, 0, i, 0)),
            pl.BlockSpec((ROW_TILE, DIFF_WIDTH), lambda b, i: (b * nt + i, 0)),
        ],
        out_shape=[
            jax.ShapeDtypeStruct((batch, hn, 2, seq, QK_PAD), BF16),
            jax.ShapeDtypeStruct((batch * seq, DIFF_WIDTH), BF16),
        ],
        compiler_params=_cparams(("parallel", "parallel")),
        name="attn_in_proj",
    )(x2, g, wq, wg, q_aug)


def _attn_kernel(lam_ref, q_ref, k_ref, vt_ref, dbias_ref, subg_ref, o_ref,
                 s_a, s_b, p_a, p_b, p_last, m_scr, acc_scr, al_scr, *, lambda_init):
    tq, tk = ATT_TQ, ATT_TK
    seq = q_ref.shape[2]
    places = tk // tq
    assert places * 2 * ATT_UNROLL >= seq // tq
    streams = [(hh, i) for hh in range(ATT_HEADS) for i in range(2)]
    n_qt = seq // tq
    lf = lam_ref[...]
    lam = (jnp.exp(jnp.sum(lf[0:1, :] * lf[1:2, :], axis=-1, keepdims=True))
           - jnp.exp(jnp.sum(lf[2:3, :] * lf[3:4, :], axis=-1, keepdims=True)) + lambda_init)
    neg = jnp.float32(-1e30)

    def fold(x, op):
        return op(x.reshape(x.shape[0] // SUBLANES, SUBLANES, tq), axis=0)

    def load_q(qi):
        q0 = pl.multiple_of(qi * tq, tq)
        return [q_ref[hh, i, pl.ds(q0, tq), :] for hh, i in streams]

    def key_rows(t, rows=tk):
        return pl.ds(t * tk if isinstance(t, int) else pl.multiple_of(t * tk, tk), rows)

    def scores_into(qs, t, s_buf, rows=tk, want_max=True):
        mparts = []
        for n, (hh, i) in enumerate(streams):
            s = _dot_nt(k_ref[hh, i, key_rows(t, rows), :], qs[n])
            s_buf[n, 0:rows, :] = s
            if want_max:
                mparts.append(fold(s, jnp.max))
        return tuple(mparts)

    def stage_c(t, p_buf, alphas, accs, rows=tk):
        vts = [vt_ref[hh, :, key_rows(t, rows)] for hh in range(ATT_HEADS)]
        return tuple(alphas[n] * accs[n] + _dot(vts[hh], p_buf[n, 0:rows, :])
                     for n, (hh, _) in enumerate(streams))

    def flush_previous(qi, old_place):
        q_old = jnp.maximum(qi - 1, 0)
        o0 = pl.multiple_of(q_old * tq, tq)
        slot = q_old % 2
        accs = stage_c(q_old // (tk // tq), p_last.at[slot],
                       tuple(al_scr[slot, n, 0:1, :] for n in range(len(streams))),
                       tuple(acc_scr[slot, n] for n in range(len(streams))),
                       rows=(old_place + 1) * tq)
        e = DIFF_V_DIM
        for hh in range(ATT_HEADS):
            a1, a2 = accs[2 * hh], accs[2 * hh + 1]
            out_t = a1[0:e] / a1[e:e + 1] - lam * (a2[0:e] / a2[e:e + 1])
            out = _rms(out_t.T, subg_ref[...]) * (1.0 - lambda_init)
            o_ref[pl.ds(o0, tq), hh * e:(hh + 1) * e] = out.astype(BF16)
        return tuple(jnp.zeros((ATT_VROWS, tq), F32) for _ in streams)

    def q_tile(qi, carry):
        n_full = qi // places
        which = qi % places
        qs = load_q(qi)

        def stage_a(t, s_buf, rows=tk, want_max=True):
            return scores_into(qs, t, s_buf, rows, want_max)

        def stage_b(s_buf, p_buf, mparts, ms, ls, place=None):
            diag = place is not None
            rows = (place + 1) * tq if diag else tk
            new_m, new_l, alphas = [], [], []
            for n, (hh, _) in enumerate(streams):
                def scores(r0):
                    s = s_buf[n, r0:r0 + ATT_ROWS, :]
                    return s + dbias_ref[hh, place, r0:r0 + ATT_ROWS, :] if diag else s

                if diag:
                    mpart = fold(scores(0), jnp.max)
                    for r0 in range(ATT_ROWS, rows, ATT_ROWS):
                        mpart = jnp.maximum(mpart, fold(scores(r0), jnp.max))
                else:
                    mpart = mparts[n]
                m_new = jnp.maximum(ms[n], jnp.max(mpart, axis=0, keepdims=True))
                alpha = jnp.exp2(ms[n] - m_new)
                for r0 in range(0, rows, ATT_ROWS):
                    p = jnp.exp2(scores(r0) - m_new)
                    p_buf[n, r0:r0 + ATT_ROWS, :] = p.astype(BF16)
                new_l.append(ls[n])
                new_m.append(m_new)
                alphas.append(alpha)
            return tuple(new_m), tuple(new_l), tuple(alphas)

        def step(j, bufs, st, place, last):
            s_cur, s_nxt, p_cur, p_prv = bufs
            mparts, ms, ls, alphas, accs = st
            if j + 1 == last:
                mparts_next = stage_a(j + 1, s_nxt, (place + 1) * tq, False)
            else:
                mparts_next = stage_a(j + 1, s_nxt)
            ms, ls, alphas_new = stage_b(s_cur, p_cur, mparts, ms, ls)
            accs = (flush_previous(qi, (place - 1) % places) if j == 0
                    else stage_c(j - 1, p_prv, alphas, accs))
            return mparts_next, ms, ls, alphas_new, accs

        def steps(j0, j1, st, place, last=None):
            for j in range(j0, j1):
                st = step(j, odd if j % 2 else even, st, place, last)
            return st

        def finish(v, st, place):
            s_cur, _, _, p_prv = odd if v % 2 else even
            _, ms, ls, alphas, accs = st
            accs = (flush_previous(qi, (place - 1) % places) if v == 0
                    else stage_c(v - 1, p_prv, alphas, accs))
            slot = qi % 2
            _, ls, alphas_new = stage_b(s_cur, p_last.at[slot], None, ms, ls, place)
            nxt = scores_into(load_q(jnp.minimum(qi + 1, n_qt - 1)), 0, s_a)
            for n in range(len(streams)):
                m_scr[n] = nxt[n]
                acc_scr[slot, n] = accs[n]
                al_scr[slot, n] = jnp.broadcast_to(alphas_new[n], (SUBLANES, tq))

        even = (s_a, s_b, p_a, p_b)
        odd = (s_b, s_a, p_b, p_a)
        init = (tuple(m_scr[n] for n in range(len(streams))),
                tuple(jnp.full((1, tq), neg, F32) for _ in streams),
                tuple(jnp.zeros((SUBLANES, tq), F32) for _ in streams),
                tuple(jnp.ones((1, tq), F32) for _ in streams),
                tuple(jnp.zeros((ATT_VROWS, tq), F32) for _ in streams))
        for place in range(places):
            @pl.when(which == place)
            def _(place=place):
                st = init
                if ATT_UNROLL < n_qt // places:
                    st = lax.fori_loop(0, n_full // ATT_UNROLL,
                                       lambda _, c: steps(0, ATT_UNROLL, c, place), init)
                for v in range(n_qt // places):
                    @pl.when(n_full == v)
                    def _(v=v):
                        lo = v - v % ATT_UNROLL if v >= ATT_UNROLL else 0
                        finish(v, steps(lo, v, st, place, last=v), place)

        return carry

    first = scores_into(load_q(0), 0, s_a)
    for n in range(len(streams)):
        m_scr[n] = first[n]
        acc_scr[0, n] = jnp.where(
            lax.broadcasted_iota(jnp.int32, (ATT_VROWS, tq), 0) == DIFF_V_DIM, 1.0, 0.0)
        al_scr[0, n] = jnp.ones((SUBLANES, tq), F32)
    p_last[0] = jnp.zeros(p_last.shape[1:], BF16)
    lax.fori_loop(0, n_qt, q_tile, 0)
    flush_previous(n_qt, (n_qt - 1) % places)


def _attention(lam_qk, q, k, vt, dbias, sub_g, lambda_init):
    batch, hn, _, seq, _ = q.shape
    return pl.pallas_call(
        functools.partial(_attn_kernel, lambda_init=lambda_init),
        grid=(batch, hn // ATT_HEADS),
        in_specs=[
            pl.BlockSpec(lam_qk.shape, lambda b, h: (0, 0)),
            pl.BlockSpec((None, ATT_HEADS, 2, seq, QK_PAD), lambda b, h: (b, h, 0, 0, 0)),
            pl.BlockSpec((None, ATT_HEADS, 2, seq, QK_PAD), lambda b, h: (b, h, 0, 0, 0)),
            pl.BlockSpec((None, ATT_HEADS, ATT_VROWS, seq), lambda b, h: (b, h, 0, 0)),
            pl.BlockSpec((ATT_HEADS, ATT_TK // ATT_TQ, ATT_TK, ATT_TQ), lambda b, h: (h, 0, 0, 0)),
            pl.BlockSpec((1, DIFF_V_DIM), lambda b, h: (0, 0)),
        ],
        out_specs=pl.BlockSpec((None, seq, ATT_HEADS * DIFF_V_DIM), lambda b, h: (b, 0, h)),
        out_shape=jax.ShapeDtypeStruct((batch, seq, DIFF_WIDTH), BF16),
        scratch_shapes=[
            pltpu.VMEM((2 * ATT_HEADS, ATT_TK, ATT_TQ), F32),
            pltpu.VMEM((2 * ATT_HEADS, ATT_TK, ATT_TQ), F32),
            pltpu.VMEM((2 * ATT_HEADS, ATT_TK, ATT_TQ), BF16),
            pltpu.VMEM((2 * ATT_HEADS, ATT_TK, ATT_TQ), BF16),
            pltpu.VMEM((2, 2 * ATT_HEADS, ATT_TK, ATT_TQ), BF16),
            pltpu.VMEM((2 * ATT_HEADS, SUBLANES, ATT_TQ), F32),
            pltpu.VMEM((2, 2 * ATT_HEADS, ATT_VROWS, ATT_TQ), F32),
            pltpu.VMEM((2, 2 * ATT_HEADS, SUBLANES, ATT_TQ), F32),
        ],
        compiler_params=_cparams(("parallel", "parallel")),
        name="diff_attention",
    )(lam_qk, q, k, vt, dbias, sub_g)


def _np_split3(a):
    a = a.astype(np.float32)
    a1 = a.astype(ml_dtypes.bfloat16).astype(np.float32)
    r1 = a - a1
    a2 = r1.astype(ml_dtypes.bfloat16).astype(np.float32)
    a3 = (r1 - a2).astype(ml_dtypes.bfloat16).astype(np.float32)
    return a1, a2, a3


def _alibi_tables(seq):
    hn = DIFF_HEADS
    slopes = np.float32(2.0) ** (-np.float32(ALIBI_MAX_EXP) * np.arange(1, hn + 1, dtype=np.float32)
                                 / np.float32(hn))
    slopes = (slopes.astype(np.float64) * LOG2E).astype(np.float32)
    pos = np.arange(seq, dtype=np.float32)
    kp = _np_split3(slopes[:, None] * pos[None, :])
    sl = _np_split3(np.broadcast_to(slopes[:, None], (hn, seq)))
    pos_hi = np.floor(pos / BF16_EXACT_INT) * BF16_EXACT_INT
    pos_lo = pos - pos_hi
    ones = np.ones((seq,), np.float32)
    k_cols = list(kp)
    q_cols = [ones, ones, ones]
    for t in sl:
        k_cols += [-t, -t]
        q_cols += [pos_hi, pos_lo]
    lane_pad = (DIFF_HEAD_DIM, QK_PAD - DIFF_HEAD_DIM - len(k_cols))
    k_aug = jnp.pad(jnp.asarray(np.stack(k_cols, axis=-1)), ((0, 0), (0, 0), lane_pad))
    q_aug = jnp.pad(jnp.asarray(np.stack(q_cols, axis=-1)), ((0, 0), lane_pad))
    kk = np.arange(ATT_TK)[None, :, None]
    qq = np.arange(ATT_TQ)[None, None, :] + ATT_TQ * np.arange(ATT_TK // ATT_TQ)[:, None, None]
    allowed = (kk // CHUNK) <= (qq // CHUNK)
    fix = np.where(kk > qq, -2.0 * (kk - qq), 0.0).astype(np.float32)
    dbias = np.where(allowed[None], slopes[:, None, None, None] * fix[None], -np.inf).astype(np.float32)
    return k_aug, q_aug, jnp.asarray(dbias)


def _pad_heads(w, n_blocks):
    d = w.shape[0]
    w = w.reshape(d, n_blocks, DIFF_HEAD_DIM)
    w = jnp.concatenate([w, jnp.zeros_like(w)], axis=-1)
    return w.reshape(d, n_blocks * QK_PAD).astype(BF16)


def kernel(x, a_norm_g, a_w_in, a_conv_w, a_conv_b, a_dt_bias, a_a_log, a_d_skip, a_gate_norm_g,
           a_w_out, kv_norm_g, w_kv, b_norm_g, b_w_in, b_lambda, b_sub_g, b_w_out, final_norm_g):
    batch, seq, d = x.shape
    t = batch * seq
    hn = SSM_HEADS
    x2 = x.reshape(t, d)

    for i in range(N_A_LAYERS):
        w_in = a_w_in[i]
        n_main = SSM_D_INNER + SSM_CONV_DIM
        w_main = jnp.concatenate([0.5 * w_in[:, :SSM_D_INNER], w_in[:, SSM_D_INNER:n_main]], axis=1).astype(BF16)
        w_dt = jnp.pad(w_in[:, n_main:], ((0, 0), (0, LANES - hn))).astype(BF16)
        zx, dt_raw = _a_in_proj(x2, a_norm_g[i][None], w_main, w_dt, 0.5 * a_conv_w[i],
                                0.5 * a_conv_b[i][None], seq)
        dt_row = dt_raw[:, :hn].reshape(batch, seq, hn).transpose(0, 2, 1)
        hp_row = jnp.zeros((SUBLANES, hn), F32).at[0].set(a_dt_bias[i]).at[1].set(a_a_log[i]).at[2].set(a_d_skip[i])
        hp_col = hp_row.T
        y = _ssd(zx, dt_raw, dt_row, hp_row, hp_col, a_gate_norm_g[i][None], batch, seq)
        x2 = _out_proj(y, a_w_out[i].astype(BF16), x2)

    k_aug, q_aug, dbias = _alibi_tables(seq)
    n_k = DIFF_HEADS * 2 * DIFF_HEAD_DIM
    wk = _pad_heads(w_kv[:, :n_k], DIFF_HEADS * 2)
    wv = w_kv[:, n_k:].astype(BF16)
    k, vt = _kv_proj(x2, kv_norm_g[None], wk, wv, k_aug, batch, seq)

    for j in range(N_B_LAYERS):
        layer = N_A_LAYERS + j
        lambda_init = 0.8 - 0.6 * math.exp(-0.3 * layer)
        wq = _pad_heads(b_w_in[j][:, :DIFF_WIDTH], DIFF_HEADS * 2)
        wg = b_w_in[j][:, DIFF_WIDTH:].astype(BF16)
        q, gate = _b_in_proj(x2, b_norm_g[j][None], wq, wg, q_aug, batch, seq)
        attn = _attention(b_lambda[j], q, k, vt, dbias, b_sub_g[j][None], lambda_init)
        x2 = _gated_out_proj(attn.reshape(t, DIFF_WIDTH), gate, b_w_out[j].astype(BF16), x2,
                             final_norm_g[None], final_norm=(j == N_B_LAYERS - 1))
    return x2.reshape(batch, seq, d)
```

```python
import functools
import math

import jax
import jax.numpy as jnp
import ml_dtypes
import numpy as np
from jax import lax
from jax.experimental import pallas as pl
from jax.experimental.pallas import tpu as pltpu

F32 = jnp.float32
BF16 = jnp.bfloat16

D_MODEL = 1024
DEPTH = 4
CHUNK = 64
N_B_LAYERS = DEPTH // 2
N_A_LAYERS = DEPTH - N_B_LAYERS
SSM_D_INNER = 2 * D_MODEL
SSM_HEAD_DIM = 64
SSM_HEADS = SSM_D_INNER // SSM_HEAD_DIM
SSM_GROUPS = 8
SSM_HPG = SSM_HEADS // SSM_GROUPS
SSM_STATE = 128
SSM_CONV = 4
SSM_GN = SSM_GROUPS * SSM_STATE
SSM_CONV_DIM = SSM_D_INNER + 2 * SSM_GN
DIFF_HEAD_DIM = 64
DIFF_V_DIM = 2 * DIFF_HEAD_DIM
DIFF_HEADS = D_MODEL // DIFF_V_DIM
DIFF_WIDTH = DIFF_HEADS * DIFF_V_DIM
ALIBI_MAX_EXP = 8.0
EPS = 1e-5
LOG2E = math.log2(math.e)
QK_SCALE = LOG2E / math.sqrt(DIFF_HEAD_DIM)

LANES = 128
SUBLANES = 8
VMEM_LIMIT = 56 * 1024 * 1024

ROW_TILE = 512
OUT_ROW_TILE = 1024
SSD_Q = 256
ATT_TQ = 256
ATT_TK = 512
ATT_UNROLL = 4
ATT_HEADS = 2
ATT_ROWS = 64
ATT_VROWS = DIFF_V_DIM + 2 * SUBLANES
BF16_EXACT_INT = 256.0
N_CHUNK = 256
CONV_GROUPS = 8
QK_PAD = 2 * DIFF_HEAD_DIM


def _cparams(sem):
    return pltpu.CompilerParams(dimension_semantics=sem, vmem_limit_bytes=VMEM_LIMIT)


def _rms(x, g):
    return x * lax.rsqrt(jnp.mean(x * x, axis=-1, keepdims=True) + EPS) * g


def _silu_of_half(h):
    return h * jnp.tanh(h) + h


def _silu(x):
    return _silu_of_half(0.5 * x)


def _softplus(x):
    return jnp.maximum(x, 0.0) + jnp.log1p(jnp.exp(-jnp.abs(x)))


def _trunc_bf16(a):
    bits = lax.bitcast_convert_type(a, jnp.int32) & jnp.int32(-65536)
    return lax.bitcast_convert_type(bits, F32)


def _split3(a):
    a1 = _trunc_bf16(a)
    r1 = a - a1
    a2 = _trunc_bf16(r1)
    return a1.astype(BF16), a2.astype(BF16), (r1 - a2).astype(BF16)


def _dot(a, b):
    return jnp.dot(a, b, preferred_element_type=F32)


def _dot_nt(a, b):
    return lax.dot_general(a, b, (((1,), (1,)), ((), ())), preferred_element_type=F32)


def _dot_tn(a, b):
    return lax.dot_general(a, b, (((0,), (0,)), ((), ())), preferred_element_type=F32)


def _a_in_kernel(x_ref, g_ref, w_ref, wdt_ref, cw_ref, cb_ref, o_ref, dt_ref, tail_ref,
                 *, tiles_per_seq):
    rows = x_ref.shape[0]
    di = SSM_D_INNER
    n_main = o_ref.shape[-1]

    @pl.when(pl.program_id(0) % tiles_per_seq == 0)
    def _():
        tail_ref[...] = jnp.zeros_like(tail_ref)

    h = _rms(x_ref[...], g_ref[...]).astype(BF16)
    sub = lax.broadcasted_iota(jnp.int32, (1, SUBLANES, N_CHUNK), 1)
    z_chunks = list(range(0, di, N_CHUNK))
    conv_chunks = list(range(di, n_main, N_CHUNK))
    order = []
    while z_chunks or conv_chunks:
        order += conv_chunks[:2] + z_chunks[:1]
        conv_chunks, z_chunks = conv_chunks[2:], z_chunks[1:]
    nxt = _dot(h, w_ref[:, order[0]:order[0] + N_CHUNK])
    for idx, n0 in enumerate(order):
        a = nxt
        if idx + 1 < len(order):
            nxt = _dot(h, w_ref[:, order[idx + 1]:order[idx + 1] + N_CHUNK])
        if n0 < di:
            o_ref[:, n0:n0 + N_CHUNK] = _silu_of_half(a).astype(BF16)
            continue
        c0 = n0 - di
        tail = tail_ref[:, c0:c0 + N_CHUNK][None]
        tail_ref[:, c0:c0 + N_CHUNK] = a[rows - SUBLANES:rows, :]
        a3 = a.reshape(rows // SUBLANES, SUBLANES, N_CHUNK)
        for g0 in range(0, rows // SUBLANES, CONV_GROUPS):
            rot = a3[g0:g0 + CONV_GROUPS]
            prev = tail if g0 == 0 else a3[g0 - 1:g0]
            acc = cb_ref[:, c0:c0 + N_CHUNK] + cw_ref[SSM_CONV - 1:SSM_CONV, c0:c0 + N_CHUNK] * rot
            for j in range(1, SSM_CONV):
                rot = pltpu.roll(rot, 1, axis=1)
                prev = pltpu.roll(prev, 1, axis=1)
                above = jnp.concatenate([prev, rot[:-1]], axis=0)
                k = SSM_CONV - 1 - j
                acc = acc + cw_ref[k:k + 1, c0:c0 + N_CHUNK] * jnp.where(sub >= j, rot, above)
            r0 = g0 * SUBLANES
            o_ref[r0:r0 + CONV_GROUPS * SUBLANES, n0:n0 + N_CHUNK] = (
                _silu_of_half(acc).reshape(CONV_GROUPS * SUBLANES, N_CHUNK).astype(BF16))
    dt_ref[...] = _dot(h, wdt_ref[...])


def _a_in_proj(x2, g, w_main, w_dt, conv_w, conv_b, seq):
    t, d = x2.shape
    n_main = w_main.shape[1]
    return pl.pallas_call(
        functools.partial(_a_in_kernel, tiles_per_seq=seq // ROW_TILE),
        grid=(t // ROW_TILE,),
        in_specs=[
            pl.BlockSpec((ROW_TILE, d), lambda i: (i, 0)),
            pl.BlockSpec((1, d), lambda i: (0, 0)),
            pl.BlockSpec((d, n_main), lambda i: (0, 0)),
            pl.BlockSpec((d, LANES), lambda i: (0, 0)),
            pl.BlockSpec((SSM_CONV, SSM_CONV_DIM), lambda i: (0, 0)),
            pl.BlockSpec((1, SSM_CONV_DIM), lambda i: (0, 0)),
        ],
        out_specs=[
            pl.BlockSpec((ROW_TILE, n_main), lambda i: (i, 0)),
            pl.BlockSpec((ROW_TILE, LANES), lambda i: (i, 0)),
        ],
        out_shape=[
            jax.ShapeDtypeStruct((t, n_main), BF16),
            jax.ShapeDtypeStruct((t, LANES), F32),
        ],
        scratch_shapes=[pltpu.VMEM((SUBLANES, SSM_CONV_DIM), F32)],
        compiler_params=_cparams(("arbitrary",)),
        name="mamba_in_proj",
    )(x2, g, w_main, w_dt, conv_w, conv_b)


def _ssd_kernel(zx_ref, dtc_ref, dtr_ref, hp_row_ref, hp_col_ref, gng_ref, o_ref, state_ref):
    q = zx_ref.shape[0]
    g_n, r_n, p_n, n_n = SSM_GROUPS, SSM_HPG, SSM_HEAD_DIM, SSM_STATE
    di = SSM_D_INNER
    hn = SSM_HEADS
    gw = r_n * p_n

    @pl.when(pl.program_id(1) == 0)
    def _():
        state_ref[...] = jnp.zeros_like(state_ref)

    dtb_row, alog_row, dskip_row = hp_row_ref[0:1, :], hp_row_ref[1:2, :], hp_row_ref[2:3, :]
    dtb_col, alog_col = hp_col_ref[:, 0:1], hp_col_ref[:, 1:2]
    dt_c = _softplus(dtc_ref[:, 0:hn] + dtb_row)
    adt_c = dt_c * (-jnp.exp(alog_row))
    dt_r = _softplus(dtr_ref[...] + dtb_col)
    adt_r = dt_r * (-jnp.exp(alog_col))
    ri = lax.broadcasted_iota(jnp.int32, (q, q), 0)
    ci = lax.broadcasted_iota(jnp.int32, (q, q), 1)
    causal = ri >= ci
    tril = jnp.where(causal, 1.0, 0.0).astype(BF16)
    triu = jnp.where(ri <= ci, 1.0, 0.0).astype(BF16)
    acs_c = sum(_dot(tril, part) for part in _split3(adt_c))
    acs_r = sum(_dot(part, triu) for part in _split3(adt_r))
    tot_row = acs_c[q - 1:q, :]
    eacs_c = jnp.exp(acs_c)
    toend_c = jnp.exp(tot_row - acs_c) * dt_c
    etot_row = jnp.exp(tot_row)
    acs2_c = acs_c * LOG2E
    src2_r = (acs_r - jnp.log(dt_r)) * LOG2E

    lane_head = lax.broadcasted_iota(jnp.int32, (hn, gw), 1) // p_n
    row_head = lax.broadcasted_iota(jnp.int32, (hn, gw), 0)
    toend16 = toend_c.astype(BF16)
    eacs16 = eacs_c.astype(BF16)
    lane_head_row = lane_head[0:1, :]
    head_mask16 = [jnp.where(lane_head_row == r, 1.0, 0.0).astype(BF16) for r in range(r_n)]

    def per_head_row(row, g):
        out = jnp.broadcast_to(row[:, g * r_n:g * r_n + 1], (1, gw))
        for r in range(1, r_n):
            hval = jnp.broadcast_to(row[:, g * r_n + r:g * r_n + r + 1], (1, gw))
            out = jnp.where(lane_head_row == r, hval, out)
        return out

    for g in range(g_n):
        spread = jnp.where(row_head == lane_head + g * r_n, 1.0, 0.0).astype(BF16)
        xs16 = zx_ref[:, di + g * gw:di + (g + 1) * gw]
        bm16 = zx_ref[:, 2 * di + g * n_n:2 * di + (g + 1) * n_n]
        cm16 = zx_ref[:, 2 * di + SSM_GN + g * n_n:2 * di + SSM_GN + (g + 1) * n_n]
        xs = xs16.astype(F32)
        cb = _dot_nt(cm16, bm16)
        st = state_ref[g]
        y = _dot(cm16, st.astype(BF16)) * _dot(eacs16, spread)
        for r in range(r_n):
            h = g * r_n + r
            seg2 = jnp.broadcast_to(acs2_c[:, h:h + 1], (q, q)) - src2_r[h:h + 1, :]
            lmat = (cb * jnp.where(causal, jnp.exp2(seg2), 0.0)).astype(BF16)
            y = y + _dot(lmat, xs16 * head_mask16[r])
        y = y + per_head_row(dskip_row, g) * xs
        xend16 = (xs * _dot(toend16, spread)).astype(BF16)
        state_ref[g] = st * per_head_row(etot_row, g) + _dot_tn(bm16, xend16)
        u = y * zx_ref[:, g * gw:(g + 1) * gw].astype(F32)
        u = u * lax.rsqrt(jnp.mean(u * u, axis=-1, keepdims=True) + EPS)
        o_ref[:, g * gw:(g + 1) * gw] = (u * gng_ref[:, g * gw:(g + 1) * gw]).astype(BF16)


def _ssd(zx, dt_col, dt_row, hp_row, hp_col, gate_g, batch, seq):
    q = SSD_Q
    nc = seq // q
    n_main = zx.shape[1]
    hn = SSM_HEADS
    return pl.pallas_call(
        _ssd_kernel,
        grid=(batch, nc),
        in_specs=[
            pl.BlockSpec((q, n_main), lambda b, c: (b * nc + c, 0)),
            pl.BlockSpec((q, LANES), lambda b, c: (b * nc + c, 0)),
            pl.BlockSpec((None, hn, q), lambda b, c: (b, 0, c)),
            pl.BlockSpec((SUBLANES, hn), lambda b, c: (0, 0)),
            pl.BlockSpec((hn, SUBLANES), lambda b, c: (0, 0)),
            pl.BlockSpec((1, SSM_D_INNER), lambda b, c: (0, 0)),
        ],
        out_specs=pl.BlockSpec((q, SSM_D_INNER), lambda b, c: (b * nc + c, 0)),
        out_shape=jax.ShapeDtypeStruct((batch * seq, SSM_D_INNER), BF16),
        scratch_shapes=[
            pltpu.VMEM((SSM_GROUPS, SSM_STATE, SSM_HPG * SSM_HEAD_DIM), F32),
        ],
        compiler_params=_cparams(("parallel", "arbitrary")),
        name="mamba_ssd",
    )(zx, dt_col, dt_row, hp_row, hp_col, gate_g)


def _out_proj_kernel(y_ref, w_ref, x_ref, o_ref):
    o_ref[...] = x_ref[...] + _dot(y_ref[...], w_ref[...])


def _out_proj(y, w, x2):
    t, k = y.shape
    d = w.shape[1]
    return pl.pallas_call(
        _out_proj_kernel,
        grid=(t // OUT_ROW_TILE,),
        in_specs=[
            pl.BlockSpec((OUT_ROW_TILE, k), lambda i: (i, 0)),
            pl.BlockSpec((k, d), lambda i: (0, 0), pipeline_mode=pl.Buffered(1)),
            pl.BlockSpec((OUT_ROW_TILE, d), lambda i: (i, 0)),
        ],
        out_specs=pl.BlockSpec((OUT_ROW_TILE, d), lambda i: (i, 0)),
        out_shape=jax.ShapeDtypeStruct((t, d), F32),
        compiler_params=_cparams(("parallel",)),
        name="mamba_out_proj",
    )(y, w, x2)


def _gated_out_proj_kernel(a_ref, gate_ref, w_ref, x_ref, fg_ref, o_ref, *, final_norm):
    gate = gate_ref[...].astype(F32)
    y = (a_ref[...].astype(F32) * _silu(gate)).astype(BF16)
    out = x_ref[...] + _dot(y, w_ref[...])
    if final_norm:
        out = _rms(out, fg_ref[...])
    o_ref[...] = out


def _gated_out_proj(attn, gate, w, x2, final_g, final_norm):
    t, k = attn.shape
    d = w.shape[1]
    return pl.pallas_call(
        functools.partial(_gated_out_proj_kernel, final_norm=final_norm),
        grid=(t // OUT_ROW_TILE,),
        in_specs=[
            pl.BlockSpec((OUT_ROW_TILE, k), lambda i: (i, 0)),
            pl.BlockSpec((OUT_ROW_TILE, k), lambda i: (i, 0)),
            pl.BlockSpec((k, d), lambda i: (0, 0), pipeline_mode=pl.Buffered(1)),
            pl.BlockSpec((OUT_ROW_TILE, d), lambda i: (i, 0)),
            pl.BlockSpec((1, d), lambda i: (0, 0)),
        ],
        out_specs=pl.BlockSpec((OUT_ROW_TILE, d), lambda i: (i, 0)),
        out_shape=jax.ShapeDtypeStruct((t, d), F32),
        compiler_params=_cparams(("parallel",)),
        name="attn_out_proj",
    )(attn, gate, w, x2, final_g)


def _kv_kernel(x_ref, g_ref, wk_ref, wv_ref, aug_ref, k_ref, vt_ref):
    h = _rms(x_ref[...], g_ref[...]).astype(BF16)
    hn = DIFF_HEADS
    for n0 in range(0, wk_ref.shape[1], N_CHUNK):
        kc = _dot(h, wk_ref[:, n0:n0 + N_CHUNK])
        for j in range(N_CHUNK // QK_PAD):
            idx = n0 // QK_PAD + j
            k_ref[idx // 2, idx % 2] = (kc[:, j * QK_PAD:(j + 1) * QK_PAD]
                                        + aug_ref[idx // 2]).astype(BF16)
    for n0 in range(0, wv_ref.shape[1], N_CHUNK):
        vc = _dot(h, wv_ref[:, n0:n0 + N_CHUNK])
        for j in range(N_CHUNK // DIFF_V_DIM):
            head = n0 // DIFF_V_DIM + j
            vt_ref[head, 0:DIFF_V_DIM, :] = vc[:, j * DIFF_V_DIM:(j + 1) * DIFF_V_DIM].T.astype(BF16)
    extra = (ATT_VROWS - DIFF_V_DIM, vt_ref.shape[-1])
    ones_row = jnp.where(lax.broadcasted_iota(jnp.int32, extra, 0) == 0, 1.0, 0.0).astype(BF16)
    for head in range(hn):
        vt_ref[head, DIFF_V_DIM:ATT_VROWS, :] = ones_row


def _kv_proj(x2, g, wk, wv, k_aug, batch, seq):
    d = x2.shape[1]
    nt = seq // ROW_TILE
    hn = DIFF_HEADS
    return pl.pallas_call(
        _kv_kernel,
        grid=(batch, nt),
        in_specs=[
            pl.BlockSpec((ROW_TILE, d), lambda b, i: (b * nt + i, 0)),
            pl.BlockSpec((1, d), lambda b, i: (0, 0)),
            pl.BlockSpec(wk.shape, lambda b, i: (0, 0)),
            pl.BlockSpec(wv.shape, lambda b, i: (0, 0)),
            pl.BlockSpec((hn, ROW_TILE, QK_PAD), lambda b, i: (0, i, 0)),
        ],
        out_specs=[
            pl.BlockSpec((None, hn, 2, ROW_TILE, QK_PAD), lambda b, i: (b, 0, 0, i, 0)),
            pl.BlockSpec((None, hn, ATT_VROWS, ROW_TILE), lambda b, i: (b, 0, 0, i)),
        ],
        out_shape=[
            jax.ShapeDtypeStruct((batch, hn, 2, seq, QK_PAD), BF16),
            jax.ShapeDtypeStruct((batch, hn, ATT_VROWS, seq), BF16),
        ],
        compiler_params=_cparams(("parallel", "parallel")),
        name="kv_proj",
    )(x2, g, wk, wv, k_aug)


def _b_in_kernel(x_ref, g_ref, wq_ref, wg_ref, aug_ref, q_ref, gate_ref):
    h = _rms(x_ref[...], g_ref[...]).astype(BF16)
    for n0 in range(0, wq_ref.shape[1], N_CHUNK):
        qc = _dot(h, wq_ref[:, n0:n0 + N_CHUNK]) * QK_SCALE
        for j in range(N_CHUNK // QK_PAD):
            idx = n0 // QK_PAD + j
            q_ref[idx // 2, idx % 2] = (qc[:, j * QK_PAD:(j + 1) * QK_PAD] + aug_ref[...]).astype(BF16)
    for n0 in range(0, wg_ref.shape[1], N_CHUNK):
        gate_ref[:, n0:n0 + N_CHUNK] = _dot(h, wg_ref[:, n0:n0 + N_CHUNK]).astype(BF16)


def _b_in_proj(x2, g, wq, wg, q_aug, batch, seq):
    d = x2.shape[1]
    nt = seq // ROW_TILE
    hn = DIFF_HEADS
    return pl.pallas_call(
        _b_in_kernel,
        grid=(batch, nt),
        in_specs=[
            pl.BlockSpec((ROW_TILE, d), lambda b, i: (b * nt + i, 0)),
            pl.BlockSpec((1, d), lambda b, i: (0, 0)),
            pl.BlockSpec(wq.shape, lambda b, i: (0, 0)),
            pl.BlockSpec(wg.shape, lambda b, i: (0, 0)),
            pl.BlockSpec((ROW_TILE, QK_PAD), lambda b, i: (i, 0)),
        ],
        out_specs=[
            pl.BlockSpec((None, hn, 2, ROW_TILE, QK_PAD), lambda b, i: (b, 0, 0, i, 0)),
            pl.BlockSpec((ROW_TILE, DIFF_WIDTH), lambda b, i: (b * nt + i, 0)),
        ],
        out_shape=[
            jax.ShapeDtypeStruct((batch, hn, 2, seq, QK_PAD), BF16),
            jax.ShapeDtypeStruct((batch * seq, DIFF_WIDTH), BF16),
        ],
        compiler_params=_cparams(("parallel", "parallel")),
        name="attn_in_proj",
    )(x2, g, wq, wg, q_aug)


def _attn_kernel(lam_ref, q_ref, k_ref, vt_ref, dbias_ref, subg_ref, o_ref,
                 s_a, s_b, p_a, p_b, p_last, m_scr, acc_scr, al_scr, *, lambda_init):
    tq, tk = ATT_TQ, ATT_TK
    seq = q_ref.shape[2]
    places = tk // tq
    assert places * 2 * ATT_UNROLL >= seq // tq
    streams = [(hh, i) for hh in range(ATT_HEADS) for i in range(2)]
    n_qt = seq // tq
    lf = lam_ref[...]
    lam = (jnp.exp(jnp.sum(lf[0:1, :] * lf[1:2, :], axis=-1, keepdims=True))
           - jnp.exp(jnp.sum(lf[2:3, :] * lf[3:4, :], axis=-1, keepdims=True)) + lambda_init)
    neg = jnp.float32(-1e30)

    def fold(x, op):
        return op(x.reshape(x.shape[0] // SUBLANES, SUBLANES, tq), axis=0)

    def load_q(qi):
        q0 = pl.multiple_of(qi * tq, tq)
        return [q_ref[hh, i, pl.ds(q0, tq), :] for hh, i in streams]

    def key_rows(t, rows=tk):
        return pl.ds(t * tk if isinstance(t, int) else pl.multiple_of(t * tk, tk), rows)

    def scores_into(qs, t, s_buf, rows=tk, want_max=True):
        mparts = []
        for n, (hh, i) in enumerate(streams):
            s = _dot_nt(k_ref[hh, i, key_rows(t, rows), :], qs[n])
            s_buf[n, 0:rows, :] = s
            if want_max:
                mparts.append(fold(s, jnp.max))
        return tuple(mparts)

    def stage_c(t, p_buf, alphas, accs, rows=tk):
        vts = [vt_ref[hh, :, key_rows(t, rows)] for hh in range(ATT_HEADS)]
        return tuple(alphas[n] * accs[n] + _dot(vts[hh], p_buf[n, 0:rows, :])
                     for n, (hh, _) in enumerate(streams))

    def flush_previous(qi, old_place):
        q_old = jnp.maximum(qi - 1, 0)
        o0 = pl.multiple_of(q_old * tq, tq)
        slot = q_old % 2
        accs = stage_c(q_old // (tk // tq), p_last.at[slot],
                       tuple(al_scr[slot, n, 0:1, :] for n in range(len(streams))),
                       tuple(acc_scr[slot, n] for n in range(len(streams))),
                       rows=(old_place + 1) * tq)
        e = DIFF_V_DIM
        for hh in range(ATT_HEADS):
            a1, a2 = accs[2 * hh], accs[2 * hh + 1]
            out_t = a1[0:e] / a1[e:e + 1] - lam * (a2[0:e] / a2[e:e + 1])
            out = _rms(out_t.T, subg_ref[...]) * (1.0 - lambda_init)
            o_ref[pl.ds(o0, tq), hh * e:(hh + 1) * e] = out.astype(BF16)
        return tuple(jnp.zeros((ATT_VROWS, tq), F32) for _ in streams)

    def q_tile(qi, carry):
        n_full = qi // places
        which = qi % places
        qs = load_q(qi)

        def stage_a(t, s_buf, rows=tk, want_max=True):
            return scores_into(qs, t, s_buf, rows, want_max)

        def stage_b(s_buf, p_buf, mparts, ms, ls, place=None):
            diag = place is not None
            rows = (place + 1) * tq if diag else tk
            new_m, new_l, alphas = [], [], []
            for n, (hh, _) in enumerate(streams):
                def scores(r0):
                    s = s_buf[n, r0:r0 + ATT_ROWS, :]
                    return s + dbias_ref[hh, place, r0:r0 + ATT_ROWS, :] if diag else s

                if diag:
                    mpart = fold(scores(0), jnp.max)
                    for r0 in range(ATT_ROWS, rows, ATT_ROWS):
                        mpart = jnp.maximum(mpart, fold(scores(r0), jnp.max))
                else:
                    mpart = mparts[n]
                m_new = jnp.maximum(ms[n], jnp.max(mpart, axis=0, keepdims=True))
                alpha = jnp.exp2(ms[n] - m_new)
                for r0 in range(0, rows, ATT_ROWS):
                    p = jnp.exp2(scores(r0) - m_new)
                    p_buf[n, r0:r0 + ATT_ROWS, :] = p.astype(BF16)
                new_l.append(ls[n])
                new_m.append(m_new)
                alphas.append(alpha)
            return tuple(new_m), tuple(new_l), tuple(alphas)

        def step(j, bufs, st, place, last):
            s_cur, s_nxt, p_cur, p_prv = bufs
            mparts, ms, ls, alphas, accs = st
            if j + 1 == last:
                mparts_next = stage_a(j + 1, s_nxt, (place + 1) * tq, False)
            else:
                mparts_next = stage_a(j + 1, s_nxt)
            ms, ls, alphas_new = stage_b(s_cur, p_cur, mparts, ms, ls)
            accs = (flush_previous(qi, (place - 1) % places) if j == 0
                    else stage_c(j - 1, p_prv, alphas, accs))
            return mparts_next, ms, ls, alphas_new, accs

        def steps(j0, j1, st, place, last=None):
            for j in range(j0, j1):
                st = step(j, odd if j % 2 else even, st, place, last)
            return st

        def finish(v, st, place):
            s_cur, _, _, p_prv = odd if v % 2 else even
            _, ms, ls, alphas, accs = st
            accs = (flush_previous(qi, (place - 1) % places) if v == 0
                    else stage_c(v - 1, p_prv, alphas, accs))
            slot = qi % 2
            _, ls, alphas_new = stage_b(s_cur, p_last.at[slot], None, ms, ls, place)
            nxt = scores_into(load_q(jnp.minimum(qi + 1, n_qt - 1)), 0, s_a)
            for n in range(len(streams)):
                m_scr[n] = nxt[n]
                acc_scr[slot, n] = accs[n]
                al_scr[slot, n] = jnp.broadcast_to(alphas_new[n], (SUBLANES, tq))

        even = (s_a, s_b, p_a, p_b)
        odd = (s_b, s_a, p_b, p_a)
        init = (tuple(m_scr[n] for n in range(len(streams))),
                tuple(jnp.full((1, tq), neg, F32) for _ in streams),
                tuple(jnp.zeros((SUBLANES, tq), F32) for _ in streams),
                tuple(jnp.ones((1, tq), F32) for _ in streams),
                tuple(jnp.zeros((ATT_VROWS, tq), F32) for _ in streams))
        for place in range(places):
            @pl.when(which == place)
            def _(place=place):
                st = init
                if ATT_UNROLL < n_qt // places:
                    st = lax.fori_loop(0, n_full // ATT_UNROLL,
                                       lambda _, c: steps(0, ATT_UNROLL, c, place), init)
                for v in range(n_qt // places):
                    @pl.when(n_full == v)
                    def _(v=v):
                        lo = v - v % ATT_UNROLL if v >= ATT_UNROLL else 0
                        finish(v, steps(lo, v, st, place, last=v), place)

        return carry

    first = scores_into(load_q(0), 0, s_a)
    for n in range(len(streams)):
        m_scr[n] = first[n]
        acc_scr[0, n] = jnp.where(
            lax.broadcasted_iota(jnp.int32, (ATT_VROWS, tq), 0) == DIFF_V_DIM, 1.0, 0.0)
        al_scr[0, n] = jnp.ones((SUBLANES, tq), F32)
    p_last[0] = jnp.zeros(p_last.shape[1:], BF16)
    lax.fori_loop(0, n_qt, q_tile, 0)
    flush_previous(n_qt, (n_qt - 1) % places)


def _attention(lam_qk, q, k, vt, dbias, sub_g, lambda_init):
    batch, hn, _, seq, _ = q.shape
    return pl.pallas_call(
        functools.partial(_attn_kernel, lambda_init=lambda_init),
        grid=(batch, hn // ATT_HEADS),
        in_specs=[
            pl.BlockSpec(lam_qk.shape, lambda b, h: (0, 0)),
            pl.BlockSpec((None, ATT_HEADS, 2, seq, QK_PAD), lambda b, h: (b, h, 0, 0, 0)),
            pl.BlockSpec((None, ATT_HEADS, 2, seq, QK_PAD), lambda b, h: (b, h, 0, 0, 0)),
            pl.BlockSpec((None, ATT_HEADS, ATT_VROWS, seq), lambda b, h: (b, h, 0, 0)),
            pl.BlockSpec((ATT_HEADS, ATT_TK // ATT_TQ, ATT_TK, ATT_TQ), lambda b, h: (h, 0, 0, 0)),
            pl.BlockSpec((1, DIFF_V_DIM), lambda b, h: (0, 0)),
        ],
        out_specs=pl.BlockSpec((None, seq, ATT_HEADS * DIFF_V_DIM), lambda b, h: (b, 0, h)),
        out_shape=jax.ShapeDtypeStruct((batch, seq, DIFF_WIDTH), BF16),
        scratch_shapes=[
            pltpu.VMEM((2 * ATT_HEADS, ATT_TK, ATT_TQ), F32),
            pltpu.VMEM((2 * ATT_HEADS, ATT_TK, ATT_TQ), F32),
            pltpu.VMEM((2 * ATT_HEADS, ATT_TK, ATT_TQ), BF16),
            pltpu.VMEM((2 * ATT_HEADS, ATT_TK, ATT_TQ), BF16),
            pltpu.VMEM((2, 2 * ATT_HEADS, ATT_TK, ATT_TQ), BF16),
            pltpu.VMEM((2 * ATT_HEADS, SUBLANES, ATT_TQ), F32),
            pltpu.VMEM((2, 2 * ATT_HEADS, ATT_VROWS, ATT_TQ), F32),
            pltpu.VMEM((2, 2 * ATT_HEADS, SUBLANES, ATT_TQ), F32),
        ],
        compiler_params=_cparams(("parallel", "parallel")),
        name="diff_attention",
    )(lam_qk, q, k, vt, dbias, sub_g)


def _np_split3(a):
    a = a.astype(np.float32)
    a1 = a.astype(ml_dtypes.bfloat16).astype(np.float32)
    r1 = a - a1
    a2 = r1.astype(ml_dtypes.bfloat16).astype(np.float32)
    a3 = (r1 - a2).astype(ml_dtypes.bfloat16).astype(np.float32)
    return a1, a2, a3


def _alibi_tables(seq):
    hn = DIFF_HEADS
    slopes = np.float32(2.0) ** (-np.float32(ALIBI_MAX_EXP) * np.arange(1, hn + 1, dtype=np.float32)
                                 / np.float32(hn))
    slopes = (slopes.astype(np.float64) * LOG2E).astype(np.float32)
    pos = np.arange(seq, dtype=np.float32)
    kp = _np_split3(slopes[:, None] * pos[None, :])
    sl = _np_split3(np.broadcast_to(slopes[:, None], (hn, seq)))
    pos_hi = np.floor(pos / BF16_EXACT_INT) * BF16_EXACT_INT
    pos_lo = pos - pos_hi
    ones = np.ones((seq,), np.float32)
    k_cols = list(kp)
    q_cols = [ones, ones, ones]
    for t in sl:
        k_cols += [-t, -t]
        q_cols += [pos_hi, pos_lo]
    lane_pad = (DIFF_HEAD_DIM, QK_PAD - DIFF_HEAD_DIM - len(k_cols))
    k_aug = jnp.pad(jnp.asarray(np.stack(k_cols, axis=-1)), ((0, 0), (0, 0), lane_pad))
    q_aug = jnp.pad(jnp.asarray(np.stack(q_cols, axis=-1)), ((0, 0), lane_pad))
    kk = np.arange(ATT_TK)[None, :, None]
    qq = np.arange(ATT_TQ)[None, None, :] + ATT_TQ * np.arange(ATT_TK // ATT_TQ)[:, None, None]
    allowed = (kk // CHUNK) <= (qq // CHUNK)
    fix = np.where(kk > qq, -2.0 * (kk - qq), 0.0).astype(np.float32)
    dbias = np.where(allowed[None], slopes[:, None, None, None] * fix[None], -np.inf).astype(np.float32)
    return k_aug, q_aug, jnp.asarray(dbias)


def _pad_heads(w, n_blocks):
    d = w.shape[0]
    w = w.reshape(d, n_blocks, DIFF_HEAD_DIM)
    w = jnp.concatenate([w, jnp.zeros_like(w)], axis=-1)
    return w.reshape(d, n_blocks * QK_PAD).astype(BF16)


def kernel(x, a_norm_g, a_w_in, a_conv_w, a_conv_b, a_dt_bias, a_a_log, a_d_skip, a_gate_norm_g,
           a_w_out, kv_norm_g, w_kv, b_norm_g, b_w_in, b_lambda, b_sub_g, b_w_out, final_norm_g):
    batch, seq, d = x.shape
    t = batch * seq
    hn = SSM_HEADS
    x2 = x.reshape(t, d)

    for i in range(N_A_LAYERS):
        w_in = a_w_in[i]
        n_main = SSM_D_INNER + SSM_CONV_DIM
        w_main = jnp.concatenate([0.5 * w_in[:, :SSM_D_INNER], w_in[:, SSM_D_INNER:n_main]], axis=1).astype(BF16)
        w_dt = jnp.pad(w_in[:, n_main:], ((0, 0), (0, LANES - hn))).astype(BF16)
        zx, dt_raw = _a_in_proj(x2, a_norm_g[i][None], w_main, w_dt, 0.5 * a_conv_w[i],
                                0.5 * a_conv_b[i][None], seq)
        dt_row = dt_raw[:, :hn].reshape(batch, seq, hn).transpose(0, 2, 1)
        hp_row = jnp.zeros((SUBLANES, hn), F32).at[0].set(a_dt_bias[i]).at[1].set(a_a_log[i]).at[2].set(a_d_skip[i])
        hp_col = hp_row.T
        y = _ssd(zx, dt_raw, dt_row, hp_row, hp_col, a_gate_norm_g[i][None], batch, seq)
        x2 = _out_proj(y, a_w_out[i].astype(BF16), x2)

    k_aug, q_aug, dbias = _alibi_tables(seq)
    n_k = DIFF_HEADS * 2 * DIFF_HEAD_DIM
    wk = _pad_heads(w_kv[:, :n_k], DIFF_HEADS * 2)
    wv = w_kv[:, n_k:].astype(BF16)
    k, vt = _kv_proj(x2, kv_norm_g[None], wk, wv, k_aug, batch, seq)

    for j in range(N_B_LAYERS):
        layer = N_A_LAYERS + j
        lambda_init = 0.8 - 0.6 * math.exp(-0.3 * layer)
        wq = _pad_heads(b_w_in[j][:, :DIFF_WIDTH], DIFF_HEADS * 2)
        wg = b_w_in[j][:, DIFF_WIDTH:].astype(BF16)
        q, gate = _b_in_proj(x2, b_norm_g[j][None], wq, wg, q_aug, batch, seq)
        attn = _attention(b_lambda[j], q, k, vt, dbias, b_sub_g[j][None], lambda_init)
        x2 = _gated_out_proj(attn.reshape(t, DIFF_WIDTH), gate, b_w_out[j].astype(BF16), x2,
                             final_norm_g[None], final_norm=(j == N_B_LAYERS - 1))
    return x2.reshape(batch, seq, d)
```
